```python
import jax, jax.numpy as jnp
from jax import lax
import numpy as np

D_MODEL = 2048
BATCH = 8
SEQ = 2048
DEPTH = 2

GRID_W = 64
CTX_LEN = 256
N_MIXERS = 2
N_MOD = 6
EPS = 1e-6

SSD_D_INNER = 2 * D_MODEL
SSD_HEAD_DIM = 64
SSD_HEADS = SSD_D_INNER // SSD_HEAD_DIM
SSD_GROUPS = 8
SSD_D_STATE = 128
SSD_CONV_W = 5
SSD_CHUNK = 128
SSD_BC = SSD_GROUPS * SSD_D_STATE
SSD_XBC = SSD_D_INNER + 2 * SSD_BC
SSD_IN = SSD_D_INNER + SSD_XBC + 2 * SSD_HEADS

RET_HEADS = 8
RET_QK_DIM = D_MODEL // RET_HEADS
RET_V_DIM = 2 * RET_QK_DIM
RET_D_V = RET_HEADS * RET_V_DIM
RET_CHUNK = 128
RET_IN = 2 * D_MODEL + 2 * RET_D_V
ROPE_BASE = 10000.0

N_EXPERTS = 16
EC_CAPACITY = 2
D_EXPERT = D_MODEL

N_SSD_LAYERS = (DEPTH + N_MIXERS - 1) // N_MIXERS
N_RET_LAYERS = DEPTH // N_MIXERS

F32 = jnp.float32

kernel_name = "hybrid_ssd_retention_ecmoe_diffusion"


def rmsnorm(x, g):
    xf = x.astype(F32)
    y = xf * lax.rsqrt(jnp.mean(xf * xf, axis=-1, keepdims=True) + EPS)
    return (y * g.astype(F32)).astype(x.dtype)


def flip(t):
    return jnp.flip(t, axis=1)


def dwconv_centred(u, w, b):
    pad = w.shape[0] // 2
    y = lax.conv_general_dilated(u, w[:, None, :].astype(u.dtype), window_strides=(1,), padding=[(pad, pad)],
                                 dimension_numbers=('NWC', 'WIO', 'NWC'), feature_group_count=u.shape[-1])
    return y + b.astype(u.dtype)


def to_chunks(t, q):
    b, l = t.shape[:2]
    return jnp.moveaxis(t.reshape(b, l // q, q, *t.shape[2:]), 1, 0)


def from_chunks(t):
    nc, b, q = t.shape[:3]
    return jnp.moveaxis(t, 0, 1).reshape(b, nc * q, *t.shape[3:])


def ssd_final_state(x, dt, A, bm):
    bsz, L, H, P = x.shape
    G, N = bm.shape[2:]
    cs = jnp.cumsum(dt * A, axis=1)
    w = (jnp.exp(cs[:, -1:] - cs) * dt).reshape(bsz, L, G, H // G)
    h = jnp.einsum('blgn,blge,blgep->bgepn', bm.astype(F32), w, x.astype(F32).reshape(bsz, L, G, H // G, P))
    return h.reshape(bsz, H, P, N)


def ssd_scan(x, dt, A, bm, cm, h0):
    bsz, L, H, P = x.shape
    G, N = bm.shape[2:]
    E = H // G
    q = SSD_CHUNK
    mask = jnp.tril(jnp.ones((q, q), dtype=bool))[None, :, :, None, None]
    a_ge = A.reshape(G, E)
    xdt = (x.astype(F32) * dt[..., None]).reshape(bsz, L, G, E, P)

    def step(h, inp):
        xc, dtc, bc, cc = inp
        cs = jnp.cumsum(dtc * a_ge, axis=1)
        seg = jnp.exp(jnp.where(mask, cs[:, :, None] - cs[:, None, :], -jnp.inf))
        cb = jnp.einsum('blgn,bsgn->blsg', cc, bc)
        y = (jnp.einsum('blsg,blsge,bsgep->blgep', cb, seg, xc)
             + jnp.einsum('blgn,bgepn,blge->blgep', cc, h, jnp.exp(cs)))
        h = (h * jnp.exp(cs[:, -1])[..., None, None]
             + jnp.einsum('bsgn,bsge,bsgep->bgepn', bc, jnp.exp(cs[:, -1:] - cs), xc))
        return h, y

    _, y = lax.scan(step, h0.reshape(bsz, G, E, P, N),
                    (to_chunks(xdt, q), to_chunks(dt.reshape(bsz, L, G, E), q),
                     to_chunks(bm.astype(F32), q), to_chunks(cm.astype(F32), q)))
    return from_chunks(y).reshape(bsz, L, H, P)


def ssd_mixer(a_l, a_c, w_in, conv_w, conv_b, dt_bias, a_log, d_skip, norm_g, w_out, need_ctx):
    A = -jnp.exp(a_log.astype(F32))

    def streams(p):
        bsz, L = p.shape[:2]
        xbc = jax.nn.silu(dwconv_centred(p[..., :SSD_XBC], conv_w, conv_b))
        xs = xbc[..., :SSD_D_INNER].reshape(bsz, L, SSD_HEADS, SSD_HEAD_DIM)
        bm = xbc[..., SSD_D_INNER:SSD_D_INNER + SSD_BC].reshape(bsz, L, SSD_GROUPS, SSD_D_STATE)
        cm = xbc[..., SSD_D_INNER + SSD_BC:].reshape(bsz, L, SSD_GROUPS, SSD_D_STATE)
        dt = jax.nn.softplus(p[..., SSD_XBC:].astype(F32).reshape(bsz, L, 2, SSD_HEADS) + dt_bias.astype(F32))
        return xs, bm, cm, dt

    def output(y, xs, z):
        bsz, L = y.shape[:2]
        y = (y + d_skip.astype(F32)[:, None] * xs.astype(F32)).astype(z.dtype).reshape(bsz, L, SSD_D_INNER)
        return rmsnorm(y * jax.nn.silu(z), norm_g) @ w_out

    def bidir(xs, bm, cm, dt, s_f, s_b):
        return (ssd_scan(xs, dt[:, :, 0], A[0], bm, cm, s_f)
                + flip(ssd_scan(flip(xs), flip(dt[:, :, 1]), A[1], flip(bm), flip(cm), s_b)))

    p_l = a_l @ w_in
    z_l = p_l[..., :SSD_D_INNER]
    x_l, b_l, c_l, dt_l = streams(p_l[..., SSD_D_INNER:])
    x_c, b_c, c_c, dt_c = streams(a_c @ w_in[:, SSD_D_INNER:])
    s_f = ssd_final_state(x_c, dt_c[:, :, 0], A[0], b_c)
    s_b = ssd_final_state(flip(x_c), flip(dt_c[:, :, 1]), A[1], flip(b_c))
    out_l = output(bidir(x_l, b_l, c_l, dt_l, s_f, s_b), x_l, z_l)
    out_c = None
    if need_ctx:
        z_c = a_c @ w_in[:, :SSD_D_INNER]
        zero = jnp.zeros_like(s_f)
        out_c = output(bidir(x_c, b_c, c_c, dt_c, zero, zero), x_c, z_c)
    return out_l, out_c


def rope_axis(u, pos):
    half = u.shape[-1] // 2
    freqs = ROPE_BASE ** (-jnp.arange(half, dtype=F32) / half)
    ang = pos.astype(F32)[:, None] * freqs[None, :]
    cos = jnp.cos(ang)[None, :, None, :]
    sin = jnp.sin(ang)[None, :, None, :]
    u1, u2 = u[..., :half], u[..., half:]
    return jnp.concatenate([u1 * cos - u2 * sin, u1 * sin + u2 * cos], axis=-1).astype(u.dtype)


def rope_2d(u, row_ids, col_ids):
    d2 = u.shape[-1] // 2
    return jnp.concatenate([rope_axis(u[..., :d2], row_ids), rope_axis(u[..., d2:], col_ids)], axis=-1)


def retention_final_state(k, v, log_decay):
    l = k.shape[1]
    w = jnp.exp((l - 1.0 - jnp.arange(l, dtype=F32))[None, :] * log_decay[:, None])
    return jnp.einsum('blhd,hl,blhv->bhdv', k.astype(F32), w, v.astype(F32))


def retention_scan(q, k, v, log_decay, s0):
    qn = RET_CHUNK
    pos = jnp.arange(qn, dtype=F32)
    rel = pos[:, None] - pos[None, :]
    decay_in = jnp.exp(jnp.where(rel[None] >= 0, rel[None] * log_decay[:, None, None], -jnp.inf))
    decay_from_state = jnp.exp((pos[:, None] + 1.0) * log_decay[None, :])
    decay_to_end = jnp.exp((qn - 1.0 - pos)[None, :] * log_decay[:, None])
    decay_chunk = jnp.exp(qn * log_decay)

    def step(s, inp):
        qc, kc, vc = inp
        scores = jnp.einsum('blhd,bshd->bhls', qc, kc) * decay_in
        o = (jnp.einsum('bhls,bshv->blhv', scores, vc)
             + jnp.einsum('blhd,bhdv->blhv', qc, s) * decay_from_state[None, :, :, None])
        s = s * decay_chunk[None, :, None, None] + jnp.einsum('bshd,hs,bshv->bhdv', kc, decay_to_end, vc)
        return s, o

    _, o = lax.scan(step, s0, (to_chunks(q.astype(F32), qn), to_chunks(k.astype(F32), qn),
                               to_chunks(v.astype(F32), qn)))
    return from_chunks(o)


def retention_output(o, g, gn_g, w_out):
    b, n = o.shape[:2]
    mu = jnp.mean(o, axis=-1, keepdims=True)
    var = jnp.mean(jnp.square(o - mu), axis=-1, keepdims=True)
    on = ((o - mu) * lax.rsqrt(var + EPS)).reshape(b, n, RET_D_V) * gn_g.astype(F32)
    return (on.astype(g.dtype) * jax.nn.silu(g)) @ w_out


def retention_mixer(a_l, a_c, row_ids, col_ids, w_in, decay_param, gn_g, w_out, need_ctx):
    b, n, _ = a_l.shape
    m = a_c.shape[1]
    D = D_MODEL
    ld = -jnp.exp(decay_param.astype(F32))
    scale = RET_QK_DIM ** -0.5

    def heads(t, L, dh):
        return t.reshape(b, L, RET_HEADS, dh)

    def bidir(qq, kk, vv, s_f, s_b):
        return (retention_scan(qq, kk, vv, ld[0], s_f)
                + flip(retention_scan(flip(qq), flip(kk), flip(vv), ld[1], s_b)))

    p_l = a_l @ w_in
    q_l = rope_2d(heads(p_l[..., :D], n, RET_QK_DIM), row_ids, col_ids)
    k_l = rope_2d(heads(p_l[..., D:2 * D], n, RET_QK_DIM), row_ids, col_ids) * scale
    v_l = heads(p_l[..., 2 * D:2 * D + RET_D_V], n, RET_V_DIM)
    g_l = p_l[..., 2 * D + RET_D_V:]
    p_c = a_c @ w_in[:, D:2 * D + RET_D_V]
    k_c = heads(p_c[..., :D], m, RET_QK_DIM) * scale
    v_c = heads(p_c[..., D:], m, RET_V_DIM)
    s_f = retention_final_state(k_c, v_c, ld[0])
    s_b = retention_final_state(flip(k_c), flip(v_c), ld[1])
    y_l = retention_output(bidir(q_l, k_l, v_l, s_f, s_b), g_l, gn_g, w_out)
    y_c = None
    if need_ctx:
        q_c = heads(a_c @ w_in[:, :D], m, RET_QK_DIM)
        g_c = a_c @ w_in[:, 2 * D + RET_D_V:]
        zero = jnp.zeros_like(s_f)
        y_c = retention_output(bidir(q_c, k_c, v_c, zero, zero), g_c, gn_g, w_out)
    return y_l, y_c


def ec_moe(h, w_router, w_gate, w_up, w_down):
    b, n, d = h.shape
    cap = EC_CAPACITY * n // N_EXPERTS
    aff = jax.nn.softmax(jnp.einsum('bnd,de->bne', h, w_router).astype(F32), axis=-1)
    gate, idx = lax.top_k(jnp.swapaxes(aff, 1, 2), cap)
    xe = jax.vmap(lambda hb, ib: hb[ib])(h, idx)
    hid = jax.nn.silu(jnp.einsum('becd,edf->becf', xe, w_gate)) * jnp.einsum('becd,edf->becf', xe, w_up)
    ye = jnp.einsum('becf,efd->becd', hid, w_down) * gate[..., None].astype(h.dtype)
    return jax.vmap(lambda yb, ib: jnp.zeros((n, d), h.dtype).at[ib.reshape(-1)].add(yb.reshape(-1, d)))(ye, idx)


def setup_inputs(seed: int = 0) -> dict:
    key = jax.random.key(seed)
    ks = jax.random.split(key, 26)
    D = D_MODEL

    def nrm(k, shape, fan):
        return jax.random.normal(k, shape, F32) * fan ** -0.5

    def gain(k, shape):
        return 1.0 + 0.01 * jax.random.normal(k, shape, F32)

    def small(k, shape):
        return 0.01 * jax.random.normal(k, shape, F32)

    x = jax.random.normal(ks[0], (BATCH, SEQ, D), F32)
    c = jax.random.normal(ks[1], (BATCH, D), F32)
    ctx = jax.random.normal(ks[2], (BATCH, CTX_LEN, D), F32)
    c_ctx = jax.random.normal(ks[3], (D,), F32)
    ada_w = nrm(ks[4], (DEPTH, D, N_MOD * D), D)
    ada_b = small(ks[5], (DEPTH, N_MOD * D))
    norm_mix_g = gain(ks[6], (DEPTH, D))
    norm_ffn_g = gain(ks[7], (DEPTH, D))
    ssd_w_in = nrm(ks[8], (N_SSD_LAYERS, D, SSD_IN), D)
    ssd_conv_w = nrm(ks[9], (N_SSD_LAYERS, SSD_CONV_W, SSD_XBC), SSD_CONV_W)
    ssd_conv_b = small(ks[10], (N_SSD_LAYERS, SSD_XBC))
    dt0 = jnp.exp(jax.random.uniform(ks[11], (N_SSD_LAYERS, 2, SSD_HEADS), F32, np.log(1e-3), np.log(1e-1)))
    ssd_dt_bias = dt0 + jnp.log(-jnp.expm1(-dt0))
    ssd_a_log = jnp.log(jax.random.uniform(ks[12], (N_SSD_LAYERS, 2, SSD_HEADS), F32, 1.0, 16.0))
    ssd_d = gain(ks[13], (N_SSD_LAYERS, SSD_HEADS))
    ssd_norm_g = gain(ks[14], (N_SSD_LAYERS, SSD_D_INNER))
    ssd_w_out = nrm(ks[15], (N_SSD_LAYERS, SSD_D_INNER, D), SSD_D_INNER)
    ret_w_in = nrm(ks[16], (N_RET_LAYERS, D, RET_IN), D)
    base = np.log(-np.log(1.0 - 2.0 ** (-5.0 - np.arange(RET_HEADS)))).astype(np.float32)
    ret_decay = jnp.asarray(base) + 0.05 * jax.random.normal(ks[17], (N_RET_LAYERS, 2, RET_HEADS), F32)
    ret_gn_g = gain(ks[18], (N_RET_LAYERS, RET_D_V))
    ret_w_out = nrm(ks[19], (N_RET_LAYERS, RET_D_V, D), RET_D_V)
    moe_w_router = nrm(ks[20], (DEPTH, D, N_EXPERTS), D)
    moe_w_gate = nrm(ks[21], (DEPTH, N_EXPERTS, D, D_EXPERT), D)
    moe_w_up = nrm(ks[22], (DEPTH, N_EXPERTS, D, D_EXPERT), D)
    moe_w_down = nrm(ks[23], (DEPTH, N_EXPERTS, D_EXPERT, D), D_EXPERT)
    final_norm_g = gain(ks[24], (D,))
    return {"x": x, "c": c, "ctx": ctx, "c_ctx": c_ctx, "ada_w": ada_w, "ada_b": ada_b,
            "norm_mix_g": norm_mix_g, "norm_ffn_g": norm_ffn_g,
            "ssd_w_in": ssd_w_in, "ssd_conv_w": ssd_conv_w, "ssd_conv_b": ssd_conv_b,
            "ssd_dt_bias": ssd_dt_bias, "ssd_a_log": ssd_a_log, "ssd_d": ssd_d,
            "ssd_norm_g": ssd_norm_g, "ssd_w_out": ssd_w_out,
            "ret_w_in": ret_w_in, "ret_decay": ret_decay, "ret_gn_g": ret_gn_g, "ret_w_out": ret_w_out,
            "moe_w_router": moe_w_router, "moe_w_gate": moe_w_gate, "moe_w_up": moe_w_up,
            "moe_w_down": moe_w_down, "final_norm_g": final_norm_g}


def reference(x, c, ctx, c_ctx, ada_w, ada_b, norm_mix_g, norm_ffn_g,
              ssd_w_in, ssd_conv_w, ssd_conv_b, ssd_dt_bias, ssd_a_log, ssd_d, ssd_norm_g, ssd_w_out,
              ret_w_in, ret_decay, ret_gn_g, ret_w_out,
              moe_w_router, moe_w_gate, moe_w_up, moe_w_down, final_norm_g):
    n_lat = x.shape[1]
    ROWS = n_lat // GRID_W
    row_ids = jnp.repeat(jnp.arange(ROWS), GRID_W)
    col_ids = jnp.tile(jnp.arange(GRID_W), ROWS)
    silu_c = jax.nn.silu(c)
    silu_cc = jax.nn.silu(c_ctx)
    h_lat, h_ctx = x, ctx
    for i in range(DEPTH):
        need_ctx = i < DEPTH - 1
        j = i // N_MIXERS
        sh1, sc1, g1, sh2, sc2, g2 = jnp.split((silu_c @ ada_w[i] + ada_b[i])[:, None, :], N_MOD, axis=-1)
        csh1, csc1, cg1, csh2, csc2, cg2 = jnp.split(silu_cc @ ada_w[i] + ada_b[i], N_MOD, axis=-1)
        a_l = rmsnorm(h_lat, norm_mix_g[i]) * (1.0 + sc1) + sh1
        a_c = rmsnorm(h_ctx, norm_mix_g[i]) * (1.0 + csc1) + csh1
        if i % N_MIXERS == 0:
            y_l, y_c = ssd_mixer(a_l, a_c, ssd_w_in[j], ssd_conv_w[j], ssd_conv_b[j], ssd_dt_bias[j],
                                 ssd_a_log[j], ssd_d[j], ssd_norm_g[j], ssd_w_out[j], need_ctx)
        else:
            y_l, y_c = retention_mixer(a_l, a_c, row_ids, col_ids, ret_w_in[j], ret_decay[j],
                                       ret_gn_g[j], ret_w_out[j], need_ctx)
        h_lat = h_lat + g1 * y_l
        f_l = rmsnorm(h_lat, norm_ffn_g[i]) * (1.0 + sc2) + sh2
        h_lat = h_lat + g2 * ec_moe(f_l, moe_w_router[i], moe_w_gate[i], moe_w_up[i], moe_w_down[i])
        if need_ctx:
            h_ctx = h_ctx + cg1 * y_c
            f_c = rmsnorm(h_ctx, norm_ffn_g[i]) * (1.0 + csc2) + csh2
            h_ctx = h_ctx + cg2 * ec_moe(f_c, moe_w_router[i], moe_w_gate[i], moe_w_up[i], moe_w_down[i])
    return rmsnorm(h_lat, final_norm_g)
```

```python
import functools

import jax
import jax.numpy as jnp
from jax import lax
from jax.experimental import pallas as pl
from jax.experimental.pallas import tpu as pltpu

F32 = jnp.float32
BF16 = jnp.bfloat16
HIGHEST = lax.Precision.HIGHEST

D_MODEL = 2048
N_MOD = 6
EPS = 1e-6

SSD_D_INNER = 2 * D_MODEL
SSD_HEAD_DIM = 64
SSD_HEADS = SSD_D_INNER // SSD_HEAD_DIM
SSD_GROUPS = 8
SSD_HEADS_PER_GROUP = SSD_HEADS // SSD_GROUPS
SSD_D_STATE = 128
SSD_CONV_W = 5
SSD_CHUNK = 128
SSD_BC = SSD_GROUPS * SSD_D_STATE
SSD_XBC = SSD_D_INNER + 2 * SSD_BC
SSD_IN = SSD_D_INNER + SSD_XBC + 2 * SSD_HEADS
SSD_GROUP_WIDTH = SSD_HEADS_PER_GROUP * SSD_HEAD_DIM

RET_HEADS = 8
RET_QK_DIM = D_MODEL // RET_HEADS
RET_V_DIM = 2 * RET_QK_DIM
RET_D_V = RET_HEADS * RET_V_DIM
RET_CHUNK = 128
RET_IN = 2 * D_MODEL + 2 * RET_D_V
ROPE_BASE = 10000.0
GRID_W = 64

N_EXPERTS = 16
EC_CAPACITY = 2
D_EXPERT = D_MODEL

V7X_LANES = 128
V7X_SUBLANES = 8
V7X_VMEM_LIMIT_BYTES = 56 * 1024 * 1024
MOD_ROWS = 16
ROUTER_LANES = V7X_LANES


def _cparams(sem, vmem=V7X_VMEM_LIMIT_BYTES):
    return pltpu.CompilerParams(dimension_semantics=sem, vmem_limit_bytes=vmem)


def _silu(x):
    return x * jax.nn.sigmoid(x)


def _pick_tile(n, target):
    t = min(n, target)
    while n % t:
        t //= 2
    return t


def _ada_kernel(c_ref, w_ref, b_ref, o_ref):
    a = _silu(c_ref[...]).astype(BF16)
    o_ref[...] = jnp.dot(a, w_ref[...].astype(BF16), preferred_element_type=F32) + b_ref[...]


def ada_table(cc, ada_w, ada_b):
    depth, d, n = ada_w.shape
    tn = 1024
    return pl.pallas_call(
        _ada_kernel,
        grid=(depth, n // tn),
        in_specs=[pl.BlockSpec((MOD_ROWS, d), lambda i, j: (0, 0)),
                  pl.BlockSpec((None, d, tn), lambda i, j: (i, 0, j)),
                  pl.BlockSpec((None, 1, tn), lambda i, j: (i, 0, j))],
        out_specs=pl.BlockSpec((None, MOD_ROWS, tn), lambda i, j: (i, 0, j)),
        out_shape=jax.ShapeDtypeStruct((depth, MOD_ROWS, n), F32),
        compiler_params=_cparams(("parallel", "parallel")),
        name="ada_table",
    )(cc, ada_w, ada_b.reshape(depth, 1, n))


class Geom:
    def __init__(self, batch, l_lat, l_ctx):
        self.batch, self.l_lat, self.l_ctx = batch, l_lat, l_ctx
        self.r_lat = batch * l_lat
        self.r_ctx = batch * l_ctx
        self.rows = self.r_lat + self.r_ctx

    def mod_row(self, i, tile):
        return jnp.where(i * tile < self.r_lat, (i * tile) // self.l_lat, self.batch)


def _norm_mod_kernel(x_ref, g_ref, sc_ref, sh_ref, o_ref):
    x = x_ref[...]
    y = x * lax.rsqrt(jnp.mean(x * x, axis=-1, keepdims=True) + EPS)
    o_ref[...] = ((y * g_ref[...]) * (1.0 + sc_ref[...]) + sh_ref[...]).astype(o_ref.dtype)


def norm_mod(x2d, g, mod3, geom, k_shift, k_scale, rows, out_dtype):
    d = x2d.shape[1]
    tl = _pick_tile(geom.l_ctx, 256)
    return pl.pallas_call(
        _norm_mod_kernel,
        grid=(rows // tl,),
        in_specs=[pl.BlockSpec((tl, d), lambda i: (i, 0)),
                  pl.BlockSpec((1, d), lambda i: (0, 0)),
                  pl.BlockSpec((None, 1, d), lambda i: (geom.mod_row(i, tl), 0, k_scale)),
                  pl.BlockSpec((None, 1, d), lambda i: (geom.mod_row(i, tl), 0, k_shift))],
        out_specs=pl.BlockSpec((tl, d), lambda i: (i, 0)),
        out_shape=jax.ShapeDtypeStruct((rows, d), out_dtype),
        compiler_params=_cparams(("parallel",)),
        name="norm_mod",
    )(x2d, g.reshape(1, d), mod3, mod3)


def _mm_kernel(a_ref, w_ref, o_ref):
    o_ref[...] = jnp.dot(a_ref[...], w_ref[...].astype(BF16), preferred_element_type=F32)


def matmul(a, w, tm, tn, col_start=0, n=None):
    m, k = a.shape
    n = w.shape[1] - col_start if n is None else n
    assert n % tn == 0 and col_start % tn == 0 and m % tm == 0
    j0 = col_start // tn
    return pl.pallas_call(
        _mm_kernel,
        grid=(m // tm, n // tn),
        in_specs=[pl.BlockSpec((tm, k), lambda i, j: (i, 0)),
                  pl.BlockSpec((k, tn), lambda i, j: (0, j + j0))],
        out_specs=pl.BlockSpec((tm, tn), lambda i, j: (i, j)),
        out_shape=jax.ShapeDtypeStruct((m, n), F32),
        compiler_params=_cparams(("parallel", "arbitrary")),
        name="matmul",
    )(a, w)


def _mm_resid_kernel(a_ref, w_ref, r_ref, g_ref, o_ref):
    acc = jnp.dot(a_ref[...], w_ref[...].astype(BF16), preferred_element_type=F32)
    o_ref[...] = r_ref[...] + g_ref[...] * acc


def matmul_resid(a, w, res, mod3, geom, k_gate, tm, tn):
    m, k = a.shape
    n = w.shape[1]
    nj = n // tn
    return pl.pallas_call(
        _mm_resid_kernel,
        grid=(m // tm, nj),
        in_specs=[pl.BlockSpec((tm, k), lambda i, j: (i, 0)),
                  pl.BlockSpec((k, tn), lambda i, j: (0, j)),
                  pl.BlockSpec((tm, tn), lambda i, j: (i, j)),
                  pl.BlockSpec((None, 1, tn), lambda i, j: (geom.mod_row(i, tm), 0, k_gate * nj + j))],
        out_specs=pl.BlockSpec((tm, tn), lambda i, j: (i, j)),
        out_shape=jax.ShapeDtypeStruct((m, n), F32),
        compiler_params=_cparams(("parallel", "arbitrary")),
        name="matmul_resid",
    )(a, w, res, mod3)


def _conv_kernel(u_ref, w_ref, b_ref, o_ref):
    u = u_ref[...]
    l = u.shape[0]
    pad = SSD_CONV_W // 2
    t = lax.broadcasted_iota(jnp.int32, u.shape, 0)
    acc = u * w_ref[pad:pad + 1, :] + b_ref[...]
    for k in range(SSD_CONV_W):
        off = k - pad
        if off == 0:
            continue
        shifted = pltpu.roll(u, (-off) % l, axis=0)
        valid = (t + off >= 0) & (t + off < l)
        acc = acc + jnp.where(valid, shifted, 0.0) * w_ref[k:k + 1, :]
    o_ref[...] = _silu(acc)


def ssd_conv(p, conv_w, conv_b, seq_len, n_seq, row_block_off, out_prev):
    rows = p.shape[0]
    ct = 512
    col_off = SSD_D_INNER // ct
    in_specs = [pl.BlockSpec((seq_len, ct), lambda s, c: (s + row_block_off, c + col_off)),
                pl.BlockSpec((SSD_CONV_W, ct), lambda s, c: (0, c)),
                pl.BlockSpec((1, ct), lambda s, c: (0, c))]
    args = [p, conv_w, conv_b.reshape(1, SSD_XBC)]
    aliases = {}
    if out_prev is not None:
        in_specs.append(pl.BlockSpec(memory_space=pl.ANY))
        args.append(out_prev)
        aliases = {3: 0}

    def kern(u_ref, w_ref, b_ref, *rest):
        _conv_kernel(u_ref, w_ref, b_ref, rest[-1])

    return pl.pallas_call(
        kern,
        grid=(n_seq, SSD_XBC // ct),
        in_specs=in_specs,
        out_specs=pl.BlockSpec((seq_len, ct), lambda s, c: (s + row_block_off, c)),
        out_shape=jax.ShapeDtypeStruct((rows, SSD_XBC), F32),
        input_output_aliases=aliases,
        compiler_params=_cparams(("parallel", "parallel")),
        name="ssd_conv",
    )(*args)


def _dt_kernel(p_ref, b_ref, o_ref):
    o_ref[...] = jax.nn.softplus(p_ref[...] + b_ref[...])


def ssd_dt(p, dt_bias, tl):
    rows, w = p.shape
    return pl.pallas_call(
        _dt_kernel,
        grid=(rows // tl,),
        in_specs=[pl.BlockSpec((tl, w), lambda i: (i, 0)),
                  pl.BlockSpec((1, w), lambda i: (0, 0))],
        out_specs=pl.BlockSpec((tl, w), lambda i: (i, 0)),
        out_shape=jax.ShapeDtypeStruct((rows, w), F32),
        compiler_params=_cparams(("parallel",)),
        name="ssd_dt",
    )(p, dt_bias.reshape(1, w))


def _ssd_scan_kernel(x_ref, b_ref, c_ref, dtc_ref, dtr_ref, alc_ref, alr_ref, dsk_ref, s0_ref, *rest, nc):
    y_ref, sfin_ref, s_scr = rest[-3], rest[-2], rest[-1]
    q = SSD_CHUNK
    e_n = SSD_HEADS_PER_GROUP
    row = lax.broadcasted_iota(jnp.int32, (q, q), 0)
    col = lax.broadcasted_iota(jnp.int32, (q, q), 1)
    lane = lax.broadcasted_iota(jnp.int32, (q, q), 1)
    left = lane < SSD_HEAD_DIM

    for d in range(2):
        a_c = -jnp.exp(alc_ref[0][:, d * e_n:(d + 1) * e_n])
        a_r = -jnp.exp(alr_ref[0][d * e_n:(d + 1) * e_n, :])
        if d == 0:
            incl = col <= row
            edge = q - 1
        else:
            incl = col >= row
            edge = 0
        tri = incl.astype(F32)
        tri_t = (row <= col if d == 0 else row >= col).astype(F32)
        s_scr[...] = s0_ref[0, d]

        def body(ci, carry, d=d, a_c=a_c, a_r=a_r, incl=incl, tri=tri, tri_t=tri_t, edge=edge):
            c = ci if d == 0 else nc - 1 - ci
            t0 = pl.multiple_of(c * q, q)
            x = x_ref[pl.ds(t0, q), :]
            bm = b_ref[pl.ds(t0, q), :]
            cm = c_ref[pl.ds(t0, q), :]
            dtc = dtc_ref[0, pl.ds(t0, q), :][:, d * e_n:(d + 1) * e_n]
            dtr = dtr_ref[0, c][d * e_n:(d + 1) * e_n, :]
            cs = jnp.dot(tri, dtc * a_c, precision=HIGHEST, preferred_element_type=F32)
            cs_t = jnp.dot(dtr * a_r, tri_t, precision=HIGHEST, preferred_element_type=F32)
            tot_t = cs_t[:, edge:edge + 1]
            w_t = jnp.exp(tot_t - cs_t) * dtr
            et_b = jnp.broadcast_to(jnp.exp(tot_t), (e_n, q))
            ecs = jnp.exp(cs)
            cb = lax.dot_general(cm.astype(BF16), bm.astype(BF16), (((1,), (1,)), ((), ())),
                                 preferred_element_type=F32)
            bm_t = bm.T
            for pr in range(e_n // 2):
                sl = slice(pr * 2 * SSD_HEAD_DIM, (pr + 1) * 2 * SSD_HEAD_DIM)
                mix, bw, cw = [], [], []
                for e in (2 * pr, 2 * pr + 1):
                    seg = jnp.exp(jnp.where(incl, cs[:, e:e + 1] - cs_t[e:e + 1, :], -jnp.inf))
                    mix.append((cb * seg * dtr[e:e + 1, :]).astype(BF16))
                    bw.append((bm_t * w_t[e:e + 1, :]).astype(BF16))
                    cw.append((cm * ecs[:, e:e + 1]).astype(BF16))
                xs = x[:, sl]
                ss = s_scr[:, sl]
                lhs1 = jnp.concatenate([jnp.concatenate(mix, axis=1), jnp.concatenate(bw, axis=1)], axis=0)
                rhs1 = jnp.concatenate([jnp.where(left, xs, 0.0), jnp.where(left, 0.0, xs)], axis=0).astype(BF16)
                r1 = jnp.dot(lhs1, rhs1, preferred_element_type=F32)
                rhs2 = jnp.concatenate([jnp.where(left, ss, 0.0), jnp.where(left, 0.0, ss)], axis=0).astype(BF16)
                y = r1[:q] + jnp.dot(jnp.concatenate(cw, axis=1), rhs2, preferred_element_type=F32)
                dec = jnp.where(left[:1], et_b[2 * pr:2 * pr + 1, :], et_b[2 * pr + 1:2 * pr + 2, :])
                s_scr[:, sl] = ss * dec + r1[q:]
                if d == 0:
                    y_ref[pl.ds(t0, q), sl] = y + dsk_ref[:, sl] * xs
                else:
                    y_ref[pl.ds(t0, q), sl] = y_ref[pl.ds(t0, q), sl] + y
            return carry

        lax.fori_loop(0, nc, body, 0)
        sfin_ref[0, d] = s_scr[...]


def ssd_scan(xbc, dt_col, dt_row, a_log, d_skip, s0, seq_len, n_seq, row_block_off, y_prev):
    rows = xbc.shape[0]
    g_n, e_n, gw = SSD_GROUPS, SSD_HEADS_PER_GROUP, SSD_GROUP_WIDTH
    nc = seq_len // SSD_CHUNK
    b_off = SSD_D_INNER // SSD_D_STATE
    c_off = (SSD_D_INNER + SSD_BC) // SSD_D_STATE
    al = a_log.reshape(2, g_n, e_n).transpose(1, 0, 2).reshape(g_n, 1, 2 * e_n)
    al_r = jnp.broadcast_to(al.reshape(g_n, 2 * e_n, 1), (g_n, 2 * e_n, V7X_LANES))
    dsk = jnp.repeat(d_skip, SSD_HEAD_DIM).reshape(1, SSD_D_INNER)
    in_specs = [pl.BlockSpec((seq_len, gw), lambda s, g: (s + row_block_off, g)),
                pl.BlockSpec((seq_len, SSD_D_STATE), lambda s, g: (s + row_block_off, b_off + g)),
                pl.BlockSpec((seq_len, SSD_D_STATE), lambda s, g: (s + row_block_off, c_off + g)),
                pl.BlockSpec((1, seq_len, 2 * e_n), lambda s, g: (g, s + row_block_off, 0)),
                pl.BlockSpec((1, nc, 2 * e_n, SSD_CHUNK), lambda s, g: (g, s + row_block_off, 0, 0)),
                pl.BlockSpec((1, 1, 2 * e_n), lambda s, g: (g, 0, 0)),
                pl.BlockSpec((1, 2 * e_n, V7X_LANES), lambda s, g: (g, 0, 0)),
                pl.BlockSpec((1, gw), lambda s, g: (0, g)),
                pl.BlockSpec((1, 2, SSD_D_STATE, gw), lambda s, g: (s * g_n + g, 0, 0, 0))]
    args = [xbc, xbc, xbc, dt_col, dt_row, al, al_r, dsk, s0]
    aliases = {}
    if y_prev is not None:
        in_specs.append(pl.BlockSpec(memory_space=pl.ANY))
        args.append(y_prev)
        aliases = {9: 0}
    return pl.pallas_call(
        functools.partial(_ssd_scan_kernel, nc=nc),
        grid=(n_seq, g_n),
        in_specs=in_specs,
        out_specs=[pl.BlockSpec((seq_len, gw), lambda s, g: (s + row_block_off, g)),
                   pl.BlockSpec((1, 2, SSD_D_STATE, gw), lambda s, g: (s * g_n + g, 0, 0, 0))],
        out_shape=[jax.ShapeDtypeStruct((rows, SSD_D_INNER), F32),
                   jax.ShapeDtypeStruct((n_seq * g_n, 2, SSD_D_STATE, gw), F32)],
        scratch_shapes=[pltpu.VMEM((SSD_D_STATE, gw), F32)],
        input_output_aliases=aliases,
        compiler_params=_cparams(("parallel", "parallel")),
        name="ssd_scan",
    )(*args)


def _ssd_out_kernel(y_ref, z_ref, g_ref, o_ref):
    v = y_ref[...] * _silu(z_ref[...])
    n = v * lax.rsqrt(jnp.mean(v * v, axis=-1, keepdims=True) + EPS)
    o_ref[...] = (n * g_ref[...]).astype(o_ref.dtype)


def ssd_out(y, p, norm_g, tl):
    rows = y.shape[0]
    return pl.pallas_call(
        _ssd_out_kernel,
        grid=(rows // tl,),
        in_specs=[pl.BlockSpec((tl, SSD_D_INNER), lambda i: (i, 0)),
                  pl.BlockSpec((tl, SSD_D_INNER), lambda i: (i, 0)),
                  pl.BlockSpec((1, SSD_D_INNER), lambda i: (0, 0))],
        out_specs=pl.BlockSpec((tl, SSD_D_INNER), lambda i: (i, 0)),
        out_shape=jax.ShapeDtypeStruct((rows, SSD_D_INNER), BF16),
        compiler_params=_cparams(("parallel",)),
        name="ssd_out",
    )(y, p, norm_g.reshape(1, SSD_D_INNER))


def _ret_state_kernel(k_ref, v_ref, ld_ref, s_ref):
    l = k_ref.shape[0]
    k = k_ref[...] * (RET_QK_DIM ** -0.5)
    vb = v_ref[...].astype(BF16)
    pos = lax.broadcasted_iota(jnp.int32, (l, V7X_LANES), 0).astype(F32)
    for d in range(2):
        ld = -jnp.exp(ld_ref[0, d:d + 1, :])
        steps = (l - 1.0 - pos) if d == 0 else pos
        w = jnp.exp(steps * ld)
        kd = (k * jnp.concatenate([w] * (RET_QK_DIM // V7X_LANES), axis=1)).astype(BF16)
        s_ref[0, d] = lax.dot_general(kd, vb, (((0,), (0,)), ((), ())), preferred_element_type=F32)


def ret_ctx_state(p, decay_b, geom):
    l = geom.l_ctx
    row_off = geom.r_lat // l
    k_off = D_MODEL // RET_QK_DIM
    v_off = 2 * D_MODEL // RET_V_DIM
    return pl.pallas_call(
        _ret_state_kernel,
        grid=(geom.batch, RET_HEADS),
        in_specs=[pl.BlockSpec((l, RET_QK_DIM), lambda b, h: (b + row_off, k_off + h)),
                  pl.BlockSpec((l, RET_V_DIM), lambda b, h: (b + row_off, v_off + h)),
                  pl.BlockSpec((1, 2, V7X_LANES), lambda b, h: (h, 0, 0))],
        out_specs=pl.BlockSpec((1, 2, RET_QK_DIM, RET_V_DIM), lambda b, h: (b * RET_HEADS + h, 0, 0, 0)),
        out_shape=jax.ShapeDtypeStruct((geom.batch * RET_HEADS, 2, RET_QK_DIM, RET_V_DIM), F32),
        compiler_params=_cparams(("parallel", "parallel")),
        name="ret_ctx_state",
    )(p, p, decay_b)


def _rope(u, cos, sin):
    parts = []
    for j in range(u.shape[1] // V7X_LANES):
        s = u[:, j * V7X_LANES:(j + 1) * V7X_LANES]
        parts.append(pltpu.roll(s, V7X_LANES // 2, axis=1))
    return u * cos + jnp.concatenate(parts, axis=1) * sin


def _ret_scan_kernel(q_ref, k_ref, v_ref, g_ref, cos_ref, sin_ref, ld_ref, gn_ref, s0_ref, o_ref,
                     s_scr, qb_scr, k_scr, acc_scr, *, nc):
    q = RET_CHUNK
    nslab_k = RET_QK_DIM // V7X_LANES
    nslab_v = RET_V_DIM // V7X_LANES
    l_i = lax.broadcasted_iota(jnp.int32, (q, q), 0).astype(F32)
    s_i = lax.broadcasted_iota(jnp.int32, (q, q), 1).astype(F32)
    rel = l_i - s_i
    ld_f = -jnp.exp(ld_ref[0, 0:1, :])
    ld_b = -jnp.exp(ld_ref[0, 1:2, :])
    decay_in = (jnp.exp(jnp.where(rel >= 0, rel * ld_f, -jnp.inf))
                + jnp.exp(jnp.where(rel <= 0, -rel * ld_b, -jnp.inf)))
    from_state = (jnp.exp((l_i + 1.0) * ld_f), jnp.exp((q - l_i) * ld_b))
    to_end = (jnp.exp((q - 1.0 - l_i) * ld_f), jnp.exp(l_i * ld_b))
    chunk_decay = (jnp.exp(q * ld_f), jnp.exp(q * ld_b))

    def update_state(d, k, vb):
        kd = (k * jnp.concatenate([to_end[d]] * nslab_k, axis=1)).astype(BF16)
        s_scr[...] = (s_scr[...] * jnp.concatenate([chunk_decay[d]] * nslab_v, axis=1)
                      + lax.dot_general(kd, vb, (((0,), (0,)), ((), ())), preferred_element_type=F32))

    def inter(d, qb):
        return (jnp.dot(qb, s_scr[...].astype(BF16), preferred_element_type=F32)
                * jnp.concatenate([from_state[d]] * nslab_v, axis=1))

    s_scr[...] = s0_ref[0, 0]

    def fwd(c, carry):
        t0 = pl.multiple_of(c * q, q)
        cos = cos_ref[pl.ds(t0, q), :]
        sin = sin_ref[pl.ds(t0, q), :]
        qb = _rope(q_ref[pl.ds(t0, q), :], cos, sin).astype(BF16)
        k = _rope(k_ref[pl.ds(t0, q), :], cos, sin) * (RET_QK_DIM ** -0.5)
        vb = v_ref[pl.ds(t0, q), :].astype(BF16)
        qb_scr[pl.ds(t0, q), :] = qb
        k_scr[pl.ds(t0, q), :] = k
        scores = lax.dot_general(qb, k.astype(BF16), (((1,), (1,)), ((), ())),
                                 preferred_element_type=F32) * decay_in
        acc_scr[pl.ds(t0, q), :] = jnp.dot(scores.astype(BF16), vb, preferred_element_type=F32) + inter(0, qb)
        update_state(0, k, vb)
        return carry

    lax.fori_loop(0, nc, fwd, 0)
    s_scr[...] = s0_ref[0, 1]

    def bwd(ci, carry):
        c = nc - 1 - ci
        t0 = pl.multiple_of(c * q, q)
        qb = qb_scr[pl.ds(t0, q), :]
        k = k_scr[pl.ds(t0, q), :]
        vb = v_ref[pl.ds(t0, q), :].astype(BF16)
        o = acc_scr[pl.ds(t0, q), :] + inter(1, qb)
        update_state(1, k, vb)
        mu = jnp.mean(o, axis=-1, keepdims=True)
        var = jnp.mean(jnp.square(o - mu), axis=-1, keepdims=True)
        on = ((o - mu) * lax.rsqrt(var + EPS)) * gn_ref[...]
        o_ref[pl.ds(t0, q), :] = (on * _silu(g_ref[pl.ds(t0, q), :])).astype(o_ref.dtype)
        return carry

    lax.fori_loop(0, nc, bwd, 0)


def ret_scan(p, cos_t, sin_t, decay_b, gn_g, s0, geom):
    l = geom.l_lat
    nc = l // RET_CHUNK
    k_off = D_MODEL // RET_QK_DIM
    v_off = 2 * D_MODEL // RET_V_DIM
    g_off = (2 * D_MODEL + RET_D_V) // RET_V_DIM
    return pl.pallas_call(
        functools.partial(_ret_scan_kernel, nc=nc),
        grid=(geom.batch, RET_HEADS),
        in_specs=[pl.BlockSpec((l, RET_QK_DIM), lambda b, h: (b, h)),
                  pl.BlockSpec((l, RET_QK_DIM), lambda b, h: (b, k_off + h)),
                  pl.BlockSpec((l, RET_V_DIM), lambda b, h: (b, v_off + h)),
                  pl.BlockSpec((l, RET_V_DIM), lambda b, h: (b, g_off + h)),
                  pl.BlockSpec((l, RET_QK_DIM), lambda b, h: (0, 0)),
                  pl.BlockSpec((l, RET_QK_DIM), lambda b, h: (0, 0)),
                  pl.BlockSpec((1, 2, V7X_LANES), lambda b, h: (h, 0, 0)),
                  pl.BlockSpec((1, RET_V_DIM), lambda b, h: (0, h)),
                  pl.BlockSpec((1, 2, RET_QK_DIM, RET_V_DIM), lambda b, h: (b * RET_HEADS + h, 0, 0, 0))],
        out_specs=pl.BlockSpec((l, RET_V_DIM), lambda b, h: (b, h)),
        out_shape=jax.ShapeDtypeStruct((geom.r_lat, RET_D_V), BF16),
        scratch_shapes=[pltpu.VMEM((RET_QK_DIM, RET_V_DIM), F32),
                        pltpu.VMEM((l, RET_QK_DIM), BF16),
                        pltpu.VMEM((l, RET_QK_DIM), F32),
                        pltpu.VMEM((l, RET_V_DIM), F32)],
        compiler_params=_cparams(("parallel", "parallel")),
        name="ret_scan",
    )(p, p, p, p, cos_t, sin_t, decay_b, gn_g.reshape(1, RET_D_V), s0)


def rope_tables(l_lat):
    half = RET_QK_DIM // 4
    pos = jnp.arange(l_lat)
    freqs = ROPE_BASE ** (-jnp.arange(half, dtype=F32) / half)
    cs, sn = [], []
    for ids in (pos // GRID_W, pos % GRID_W):
        ang = ids.astype(F32)[:, None] * freqs[None, :]
        cs += [jnp.cos(ang), jnp.cos(ang)]
        sn += [-jnp.sin(ang), jnp.sin(ang)]
    return jnp.concatenate(cs, axis=1), jnp.concatenate(sn, axis=1)


def _router_kernel(x_ref, g_ref, sc_ref, sh_ref, wr_ref, f_ref, ac_ref, ar_ref):
    x = x_ref[...]
    y = x * lax.rsqrt(jnp.mean(x * x, axis=-1, keepdims=True) + EPS)
    f = (y * g_ref[...]) * (1.0 + sc_ref[...]) + sh_ref[...]
    f_ref[...] = f
    logits = jnp.dot(f, wr_ref[...], precision=HIGHEST, preferred_element_type=F32)
    lane = lax.broadcasted_iota(jnp.int32, logits.shape, 1)
    logits = jnp.where(lane < N_EXPERTS, logits, -jnp.inf)
    un = jnp.exp(logits - jnp.max(logits, axis=-1, keepdims=True))
    aff = un / jnp.sum(un, axis=-1, keepdims=True)
    ac_ref[...] = aff
    ar_ref[...] = aff.T[:N_EXPERTS, :]


def moe_router(x2d, g, mod3, w_router, geom, k_shift, k_scale, rows):
    d = x2d.shape[1]
    tl = _pick_tile(geom.l_ctx, 256)
    wr = jnp.pad(w_router, ((0, 0), (0, ROUTER_LANES - N_EXPERTS)))
    return pl.pallas_call(
        _router_kernel,
        grid=(rows // tl,),
        in_specs=[pl.BlockSpec((tl, d), lambda i: (i, 0)),
                  pl.BlockSpec((1, d), lambda i: (0, 0)),
                  pl.BlockSpec((None, 1, d), lambda i: (geom.mod_row(i, tl), 0, k_scale)),
                  pl.BlockSpec((None, 1, d), lambda i: (geom.mod_row(i, tl), 0, k_shift)),
                  pl.BlockSpec((d, ROUTER_LANES), lambda i: (0, 0))],
        out_specs=[pl.BlockSpec((tl, d), lambda i: (i, 0)),
                   pl.BlockSpec((tl, ROUTER_LANES), lambda i: (i, 0)),
                   pl.BlockSpec((N_EXPERTS, tl), lambda i: (0, i))],
        out_shape=[jax.ShapeDtypeStruct((rows, d), F32),
                   jax.ShapeDtypeStruct((rows, ROUTER_LANES), F32),
                   jax.ShapeDtypeStruct((N_EXPERTS, rows), F32)],
        compiler_params=_cparams(("parallel",)),
        name="moe_router",
    )(x2d, g.reshape(1, d), mod3, mod3, wr)


def _rank_kernel(ac_ref, ar_ref, o_ref, colb_scr, *, n, cap):
    e = pl.program_id(1)
    sub, lanes = V7X_SUBLANES, V7X_LANES
    lane_c = lax.broadcasted_iota(jnp.int32, (n, lanes), 1)
    a_col = jnp.sum(jnp.where(lane_c == e, ac_ref[...], 0.0), axis=1, keepdims=True)
    colb_scr[...] = jnp.broadcast_to(a_col, (n, lanes))
    a_row = ar_ref[pl.ds(e, 1), :]
    m_in = lax.broadcasted_iota(jnp.int32, (sub, lanes), 0)
    t_in = lax.broadcasted_iota(jnp.int32, (sub, lanes), 1)
    ranks = []
    for j in range(n // lanes):
        a_t = jnp.broadcast_to(a_row[:, j * lanes:(j + 1) * lanes], (sub, lanes))

        def body(i, acc, a_t=a_t, j=j):
            a_m = colb_scr[pl.ds(pl.multiple_of(i * sub, sub), sub), :]
            before = (i * sub + m_in) < (j * lanes + t_in)
            hit = (a_m > a_t) | ((a_m == a_t) & before)
            return acc + hit.astype(jnp.int32)

        acc = lax.fori_loop(0, n // sub, body, jnp.zeros((sub, lanes), jnp.int32))
        ranks.append(jnp.sum(acc, axis=0, keepdims=True))
    rank = jnp.concatenate(ranks, axis=1)
    r_col = lax.broadcasted_iota(jnp.int32, (cap, n), 0)
    t_row = lax.broadcasted_iota(jnp.int32, (cap, n), 1).astype(F32)
    onehot = rank == r_col
    idx = jnp.sum(jnp.where(onehot, t_row, 0.0), axis=1, keepdims=True)
    gate = jnp.sum(jnp.where(onehot, a_row, 0.0), axis=1, keepdims=True)
    lane_o = lax.broadcasted_iota(jnp.int32, (cap, lanes), 1)
    packed = jnp.where(lane_o == 0, idx, jnp.where(lane_o == 1, gate, 0.0))
    cap_pad = o_ref.shape[-1]
    if cap_pad > cap:
        packed = jnp.concatenate([packed, jnp.zeros((cap_pad - cap, lanes), F32)], axis=0)
    o_ref[0] = packed.T[:sub, :]


def moe_rank(aff_col, aff_row, n, n_sets, row_block_off):
    cap = EC_CAPACITY * n // N_EXPERTS
    cap_pad = max(cap, V7X_LANES)
    packed = pl.pallas_call(
        functools.partial(_rank_kernel, n=n, cap=cap),
        grid=(n_sets, N_EXPERTS),
        in_specs=[pl.BlockSpec((n, ROUTER_LANES), lambda b, e: (b + row_block_off, 0)),
                  pl.BlockSpec((N_EXPERTS, n), lambda b, e: (0, b + row_block_off))],
        out_specs=pl.BlockSpec((1, V7X_SUBLANES, cap_pad), lambda b, e: (b * N_EXPERTS + e, 0, 0)),
        out_shape=jax.ShapeDtypeStruct((n_sets * N_EXPERTS, V7X_SUBLANES, cap_pad), F32),
        scratch_shapes=[pltpu.VMEM((n, ROUTER_LANES), F32)],
        compiler_params=_cparams(("parallel", "arbitrary")),
        name="moe_rank",
    )(aff_col, aff_row)
    return packed[:, :, :cap]


def _gather_kernel(idx_ref, f_ref, *rest, cap):
    o_ref, buf = rest[-2], rest[-1]

    def body(r, carry):
        buf[pl.ds(r, 1), :] = f_ref[pl.ds(idx_ref[0, 0, r], 1), :]
        return carry

    lax.fori_loop(0, cap, body, 0, unroll=8)
    o_ref[...] = buf[...].astype(o_ref.dtype)


def moe_gather(idx, f2d, n, n_sets, row_block_off, slot_block_off, m_total, xe_prev):
    d = f2d.shape[1]
    cap = idx.shape[-1]
    dt = 1024
    in_specs = [pl.BlockSpec((1, 1, cap), lambda b, c, e: (b * N_EXPERTS + e, 0, 0), memory_space=pltpu.SMEM),
                pl.BlockSpec((n, dt), lambda b, c, e: (b + row_block_off, c))]
    args = [idx, f2d]
    aliases = {}
    if xe_prev is not None:
        in_specs.append(pl.BlockSpec(memory_space=pl.ANY))
        args.append(xe_prev)
        aliases = {2: 0}
    return pl.pallas_call(
        functools.partial(_gather_kernel, cap=cap),
        grid=(n_sets, d // dt, N_EXPERTS),
        in_specs=in_specs,
        out_specs=pl.BlockSpec((None, cap, dt), lambda b, c, e: (e, b + slot_block_off, c)),
        out_shape=jax.ShapeDtypeStruct((N_EXPERTS, m_total, d), BF16),
        scratch_shapes=[pltpu.VMEM((cap, dt), F32)],
        input_output_aliases=aliases,
        compiler_params=_cparams(("parallel", "parallel", "arbitrary")),
        name="moe_gather",
    )(*args)


def _glu_kernel(x_ref, wg_ref, wu_ref, o_ref):
    x = x_ref[...]
    hg = jnp.dot(x, wg_ref[...].astype(BF16), preferred_element_type=F32)
    hu = jnp.dot(x, wu_ref[...].astype(BF16), preferred_element_type=F32)
    o_ref[...] = (_silu(hg) * hu).astype(o_ref.dtype)


def moe_glu(xe, w_gate, w_up, layer):
    e_n, m, d = xe.shape
    f = w_gate.shape[-1]
    tn = 256
    return pl.pallas_call(
        _glu_kernel,
        grid=(e_n, f // tn),
        in_specs=[pl.BlockSpec((None, m, d), lambda e, j: (e, 0, 0)),
                  pl.BlockSpec((None, None, d, tn), lambda e, j: (layer, e, 0, j)),
                  pl.BlockSpec((None, None, d, tn), lambda e, j: (layer, e, 0, j))],
        out_specs=pl.BlockSpec((None, m, tn), lambda e, j: (e, 0, j)),
        out_shape=jax.ShapeDtypeStruct((e_n, m, f), BF16),
        compiler_params=_cparams(("parallel", "arbitrary")),
        name="moe_glu",
    )(xe, w_gate, w_up)


def _down_kernel(h_ref, wd_ref, gate_ref, o_ref):
    acc = jnp.dot(h_ref[...], wd_ref[...].astype(BF16), preferred_element_type=F32)
    o_ref[...] = acc * gate_ref[...]


def moe_down(hid, w_down, gate_col, layer):
    e_n, m, f = hid.shape
    d = w_down.shape[-1]
    tn = 256
    return pl.pallas_call(
        _down_kernel,
        grid=(e_n, d // tn),
        in_specs=[pl.BlockSpec((None, m, f), lambda e, j: (e, 0, 0)),
                  pl.BlockSpec((None, None, f, tn), lambda e, j: (layer, e, 0, j)),
                  pl.BlockSpec((None, m, 1), lambda e, j: (e, 0, 0))],
        out_specs=pl.BlockSpec((None, m, tn), lambda e, j: (e, 0, j)),
        out_shape=jax.ShapeDtypeStruct((e_n, m, d), F32),
        compiler_params=_cparams(("parallel", "arbitrary")),
        name="moe_down",
    )(hid, w_down, gate_col)


def _combine_kernel(idx_ref, ye_ref, h_ref, g_ref, *rest, cap):
    o_ref, acc = rest[-2], rest[-1]
    e = pl.program_id(2)

    @pl.when(e == 0)
    def _():
        acc[...] = jnp.zeros_like(acc)

    def body(r, carry):
        t = idx_ref[0, 0, r]
        acc[pl.ds(t, 1), :] = acc[pl.ds(t, 1), :] + ye_ref[pl.ds(r, 1), :]
        return carry

    lax.fori_loop(0, cap, body, 0, unroll=4)

    @pl.when(e == N_EXPERTS - 1)
    def _():
        o_ref[...] = h_ref[...] + g_ref[...] * acc[...]


def moe_combine(idx, ye, x2d, mod3, geom, k_gate, n, n_sets, row_block_off, slot_block_off, out_rows, out_prev):
    d = x2d.shape[1]
    cap = idx.shape[-1]
    dt = 512
    ndt = d // dt
    in_specs = [pl.BlockSpec((1, 1, cap), lambda b, c, e: (b * N_EXPERTS + e, 0, 0), memory_space=pltpu.SMEM),
                pl.BlockSpec((None, cap, dt), lambda b, c, e: (e, b + slot_block_off, c)),
                pl.BlockSpec((n, dt), lambda b, c, e: (b + row_block_off, c)),
                pl.BlockSpec((None, 1, dt),
                             lambda b, c, e: (geom.mod_row(b + row_block_off, n), 0, k_gate * ndt + c))]
    args = [idx, ye, x2d, mod3]
    aliases = {}
    if out_prev is not None:
        in_specs.append(pl.BlockSpec(memory_space=pl.ANY))
        args.append(out_prev)
        aliases = {4: 0}
    return pl.pallas_call(
        functools.partial(_combine_kernel, cap=cap),
        grid=(n_sets, ndt, N_EXPERTS),
        in_specs=in_specs,
        out_specs=pl.BlockSpec((n, dt), lambda b, c, e: (b + row_block_off, c)),
        out_shape=jax.ShapeDtypeStruct((out_rows, d), F32),
        scratch_shapes=[pltpu.VMEM((n, dt), F32)],
        input_output_aliases=aliases,
        compiler_params=_cparams(("parallel", "parallel", "arbitrary")),
        name="moe_combine",
    )(*args)


def _split_rank(packed, n_sets):
    cap = packed.shape[-1]
    idx = packed[:, 0, :].astype(jnp.int32).reshape(n_sets * N_EXPERTS, 1, cap)
    gate = packed[:, 1, :].reshape(n_sets, N_EXPERTS, cap).transpose(1, 0, 2).reshape(N_EXPERTS, n_sets * cap)
    return idx, gate


def ec_moe(x2d, norm_g, mod3, w_router, w_gate, w_up, w_down, layer, geom, with_ctx):
    rows = geom.rows if with_ctx else geom.r_lat
    f2d, aff_col, aff_row = moe_router(x2d, norm_g, mod3, w_router, geom, 3, 4, rows)
    sets = [(geom.l_lat, 0)]
    if with_ctx:
        sets.append((geom.l_ctx, geom.r_lat // geom.l_ctx))
    caps = [EC_CAPACITY * n // N_EXPERTS for n, _ in sets]
    m_total = sum(geom.batch * c for c in caps)
    routed, xe, slot_off = [], None, 0
    for (n, row_off), cap in zip(sets, caps):
        idx, gate = _split_rank(moe_rank(aff_col, aff_row, n, geom.batch, row_off), geom.batch)
        xe = moe_gather(idx, f2d, n, geom.batch, row_off, slot_off // cap, m_total, xe)
        routed.append((idx, gate, n, row_off, slot_off // cap))
        slot_off += geom.batch * cap
    gate_col = jnp.concatenate([r[1] for r in routed], axis=1).reshape(N_EXPERTS, m_total, 1)
    ye = moe_down(moe_glu(xe, w_gate, w_up, layer), w_down, gate_col, layer)
    out = None
    for idx, _, n, row_off, slot_blk in routed:
        out = moe_combine(idx, ye, x2d, mod3, geom, 5, n, geom.batch, row_off, slot_blk, rows, out)
    return out


def _final_norm_kernel(x_ref, g_ref, o_ref):
    x = x_ref[...]
    o_ref[...] = (x * lax.rsqrt(jnp.mean(x * x, axis=-1, keepdims=True) + EPS)) * g_ref[...]


def final_norm(x2d, g, rows, tl):
    d = x2d.shape[1]
    return pl.pallas_call(
        _final_norm_kernel,
        grid=(rows // tl,),
        in_specs=[pl.BlockSpec((tl, d), lambda i: (i, 0)), pl.BlockSpec((1, d), lambda i: (0, 0))],
        out_specs=pl.BlockSpec((tl, d), lambda i: (i, 0)),
        out_shape=jax.ShapeDtypeStruct((rows, d), F32),
        compiler_params=_cparams(("parallel",)),
        name="final_norm",
    )(x2d, g.reshape(1, d))


def _group_dt(dt, rows):
    g_n, e_n = SSD_GROUPS, SSD_HEADS_PER_GROUP
    dt_col = dt.reshape(rows, 2, g_n, e_n).transpose(2, 0, 1, 3).reshape(g_n, rows, 2 * e_n)
    dt_row = dt_col.reshape(g_n, rows // SSD_CHUNK, SSD_CHUNK, 2 * e_n).transpose(0, 1, 3, 2)
    return dt_col, dt_row


def ssd_layer(x2d, mod3, norm_mix_g, w_in, conv_w, conv_b, dt_bias, a_log, d_skip, norm_g, w_out, geom, tm):
    a = norm_mod(x2d, norm_mix_g, mod3, geom, 0, 1, geom.rows, BF16)
    n_main = SSD_D_INNER + SSD_XBC
    p = matmul(a, w_in, tm, 512, 0, n_main)
    p_dt = matmul(a, w_in, tm, 2 * SSD_HEADS, n_main, 2 * SSD_HEADS)
    ctx_blk = geom.r_lat // geom.l_ctx
    xbc = ssd_conv(p, conv_w, conv_b, geom.l_ctx, geom.batch, ctx_blk, None)
    xbc = ssd_conv(p, conv_w, conv_b, geom.l_lat, geom.batch, 0, xbc)
    dt = ssd_dt(p_dt, dt_bias, tm)
    dt_col, dt_row = _group_dt(dt, geom.rows)
    zero = jnp.zeros((geom.batch * SSD_GROUPS, 2, SSD_D_STATE, SSD_GROUP_WIDTH), F32)
    y, s_ctx = ssd_scan(xbc, dt_col, dt_row, a_log, d_skip, zero, geom.l_ctx, geom.batch, ctx_blk, None)
    y, _ = ssd_scan(xbc, dt_col, dt_row, a_log, d_skip, s_ctx, geom.l_lat, geom.batch, 0, y)
    yb = ssd_out(y, p, norm_g, _pick_tile(geom.l_ctx, 256))
    return matmul_resid(yb, w_out, x2d, mod3, geom, 2, tm, 512)


def ret_layer(x2d, mod3, norm_mix_g, w_in, decay, gn_g, w_out, geom, tm):
    a = norm_mod(x2d, norm_mix_g, mod3, geom, 0, 1, geom.rows, BF16)
    p = matmul(a, w_in, tm, 512)
    decay_b = jnp.broadcast_to(decay.T.reshape(RET_HEADS, 2, 1), (RET_HEADS, 2, V7X_LANES))
    cos_t, sin_t = rope_tables(geom.l_lat)
    s_ctx = ret_ctx_state(p, decay_b, geom)
    yb = ret_scan(p, cos_t, sin_t, decay_b, gn_g, s_ctx, geom)
    return matmul_resid(yb, w_out, x2d, mod3, geom, 2, tm, 512)


def kernel(x, c, ctx, c_ctx, ada_w, ada_b, norm_mix_g, norm_ffn_g, ssd_w_in, ssd_conv_w, ssd_conv_b,
           ssd_dt_bias, ssd_a_log, ssd_d, ssd_norm_g, ssd_w_out, ret_w_in, ret_decay, ret_gn_g, ret_w_out,
           moe_w_router, moe_w_gate, moe_w_up, moe_w_down, final_norm_g):
    batch, l_lat, d = x.shape
    l_ctx = ctx.shape[1]
    depth = ada_w.shape[0]
    n_mixers = 2
    assert batch + 1 <= MOD_ROWS and d == D_MODEL
    assert depth == n_mixers, "the retention layer must be the last one: its context branch is state-only"
    geom = Geom(batch, l_lat, l_ctx)
    tm = _pick_tile(geom.r_ctx, 1024)
    x2d = jnp.concatenate([x.reshape(geom.r_lat, d), ctx.reshape(geom.r_ctx, d)], axis=0)
    cc = jnp.zeros((MOD_ROWS, d), F32).at[:batch].set(c).at[batch].set(c_ctx)
    mod = ada_table(cc, ada_w, ada_b)
    for i in range(depth):
        need_ctx = i < depth - 1
        j = i // n_mixers
        mod3 = mod[i].reshape(MOD_ROWS, 1, N_MOD * d)
        if i % n_mixers == 0:
            x2d = ssd_layer(x2d, mod3, norm_mix_g[i], ssd_w_in[j], ssd_conv_w[j], ssd_conv_b[j], ssd_dt_bias[j],
                            ssd_a_log[j], ssd_d[j], ssd_norm_g[j], ssd_w_out[j], geom, tm)
        else:
            x2d = ret_layer(x2d, mod3, norm_mix_g[i], ret_w_in[j], ret_decay[j], ret_gn_g[j], ret_w_out[j],
                            geom, tm)
        x2d = ec_moe(x2d, norm_ffn_g[i], mod3, moe_w_router[i], moe_w_gate, moe_w_up, moe_w_down, i, geom,
                     need_ctx)
    out = final_norm(x2d, final_norm_g, geom.r_lat, _pick_tile(geom.l_ctx, 256))
    return out.reshape(batch, l_lat, d)
```

```python
import functools

import jax
import jax.numpy as jnp
from jax import lax
from jax.experimental import pallas as pl
from jax.experimental.pallas import tpu as pltpu

F32 = jnp.float32
BF16 = jnp.bfloat16
HIGHEST = lax.Precision.HIGHEST

D_MODEL = 2048
N_MOD = 6
EPS = 1e-6

SSD_D_INNER = 2 * D_MODEL
SSD_HEAD_DIM = 64
SSD_HEADS = SSD_D_INNER // SSD_HEAD_DIM
SSD_GROUPS = 8
SSD_HEADS_PER_GROUP = SSD_HEADS // SSD_GROUPS
SSD_D_STATE = 128
SSD_CONV_W = 5
SSD_CHUNK = 128
SSD_BC = SSD_GROUPS * SSD_D_STATE
SSD_XBC = SSD_D_INNER + 2 * SSD_BC
SSD_IN = SSD_D_INNER + SSD_XBC + 2 * SSD_HEADS
SSD_GROUP_WIDTH = SSD_HEADS_PER_GROUP * SSD_HEAD_DIM

RET_HEADS = 8
RET_QK_DIM = D_MODEL // RET_HEADS
RET_V_DIM = 2 * RET_QK_DIM
RET_D_V = RET_HEADS * RET_V_DIM
RET_CHUNK = 128
RET_IN = 2 * D_MODEL + 2 * RET_D_V
ROPE_BASE = 10000.0
GRID_W = 64

N_EXPERTS = 16
EC_CAPACITY = 2
D_EXPERT = D_MODEL

V7X_LANES = 128
V7X_SUBLANES = 8
V7X_VMEM_LIMIT_BYTES = 56 * 1024 * 1024
MOD_ROWS = 16
ROUTER_LANES = V7X_LANES


def _cparams(sem, vmem=V7X_VMEM_LIMIT_BYTES):
    return pltpu.CompilerParams(dimension_semantics=sem, vmem_limit_bytes=vmem)


def _silu(x):
    return x * jax.nn.sigmoid(x)


def _pick_tile(n, target):
    t = min(n, target)
    while n % t:
        t //= 2
    return t


def _ada_kernel(c_ref, w_ref, b_ref, o_ref):
    a = _silu(c_ref[...]).astype(BF16)
    o_ref[...] = jnp.dot(a, w_ref[...].astype(BF16), preferred_element_type=F32) + b_ref[...]


def ada_table(cc, ada_w, ada_b):
    depth, d, n = ada_w.shape
    tn = 1024
    return pl.pallas_call(
        _ada_kernel,
        grid=(depth, n // tn),
        in_specs=[pl.BlockSpec((MOD_ROWS, d), lambda i, j: (0, 0)),
                  pl.BlockSpec((None, d, tn), lambda i, j: (i, 0, j)),
                  pl.BlockSpec((None, 1, tn), lambda i, j: (i, 0, j))],
        out_specs=pl.BlockSpec((None, MOD_ROWS, tn), lambda i, j: (i, 0, j)),
        out_shape=jax.ShapeDtypeStruct((depth, MOD_ROWS, n), F32),
        compiler_params=_cparams(("parallel", "parallel")),
        name="ada_table",
    )(cc, ada_w, ada_b.reshape(depth, 1, n))


class Geom:
    def __init__(self, batch, l_lat, l_ctx):
        self.batch, self.l_lat, self.l_ctx = batch, l_lat, l_ctx
        self.r_lat = batch * l_lat
        self.r_ctx = batch * l_ctx
        self.rows = self.r_lat + self.r_ctx

    def mod_row(self, i, tile):
        return jnp.where(i * tile < self.r_lat, (i * tile) // self.l_lat, self.batch)


def _norm_mod_kernel(x_ref, g_ref, sc_ref, sh_ref, o_ref):
    x = x_ref[...]
    y = x * lax.rsqrt(jnp.mean(x * x, axis=-1, keepdims=True) + EPS)
    o_ref[...] = ((y * g_ref[...]) * (1.0 + sc_ref[...]) + sh_ref[...]).astype(o_ref.dtype)


def norm_mod(x2d, g, mod3, geom, k_shift, k_scale, rows, out_dtype):
    d = x2d.shape[1]
    tl = _pick_tile(geom.l_ctx, 256)
    return pl.pallas_call(
        _norm_mod_kernel,
        grid=(rows // tl,),
        in_specs=[pl.BlockSpec((tl, d), lambda i: (i, 0)),
                  pl.BlockSpec((1, d), lambda i: (0, 0)),
                  pl.BlockSpec((None, 1, d), lambda i: (geom.mod_row(i, tl), 0, k_scale)),
                  pl.BlockSpec((None, 1, d), lambda i: (geom.mod_row(i, tl), 0, k_shift))],
        out_specs=pl.BlockSpec((tl, d), lambda i: (i, 0)),
        out_shape=jax.ShapeDtypeStruct((rows, d), out_dtype),
        compiler_params=_cparams(("parallel",)),
        name="norm_mod",
    )(x2d, g.reshape(1, d), mod3, mod3)


def _mm_kernel(a_ref, w_ref, o_ref):
    o_ref[...] = jnp.dot(a_ref[...], w_ref[...].astype(BF16), preferred_element_type=F32)


def matmul(a, w, tm, tn, col_start=0, n=None):
    m, k = a.shape
    n = w.shape[1] - col_start if n is None else n
    assert n % tn == 0 and col_start % tn == 0 and m % tm == 0
    j0 = col_start // tn
    return pl.pallas_call(
        _mm_kernel,
        grid=(m // tm, n // tn),
        in_specs=[pl.BlockSpec((tm, k), lambda i, j: (i, 0)),
                  pl.BlockSpec((k, tn), lambda i, j: (0, j + j0))],
        out_specs=pl.BlockSpec((tm, tn), lambda i, j: (i, j)),
        out_shape=jax.ShapeDtypeStruct((m, n), F32),
        compiler_params=_cparams(("parallel", "arbitrary")),
        name="matmul",
    )(a, w)


def _mm_resid_kernel(a_ref, w_ref, r_ref, g_ref, o_ref):
    acc = jnp.dot(a_ref[...], w_ref[...].astype(BF16), preferred_element_type=F32)
    o_ref[...] = r_ref[...] + g_ref[...] * acc


def matmul_resid(a, w, res, mod3, geom, k_gate, tm, tn):
    m, k = a.shape
    n = w.shape[1]
    nj = n // tn
    return pl.pallas_call(
        _mm_resid_kernel,
        grid=(m // tm, nj),
        in_specs=[pl.BlockSpec((tm, k), lambda i, j: (i, 0)),
                  pl.BlockSpec((k, tn), lambda i, j: (0, j)),
                  pl.BlockSpec((tm, tn), lambda i, j: (i, j)),
                  pl.BlockSpec((None, 1, tn), lambda i, j: (geom.mod_row(i, tm), 0, k_gate * nj + j))],
        out_specs=pl.BlockSpec((tm, tn), lambda i, j: (i, j)),
        out_shape=jax.ShapeDtypeStruct((m, n), F32),
        compiler_params=_cparams(("parallel", "arbitrary")),
        name="matmul_resid",
    )(a, w, res, mod3)


def _conv_kernel(u_ref, w_ref, b_ref, o_ref):
    u = u_ref[...]
    l = u.shape[0]
    pad = SSD_CONV_W // 2
    t = lax.broadcasted_iota(jnp.int32, u.shape, 0)
    acc = u * w_ref[pad:pad + 1, :] + b_ref[...]
    for k in range(SSD_CONV_W):
        off = k - pad
        if off == 0:
            continue
        shifted = pltpu.roll(u, (-off) % l, axis=0)
        valid = (t + off >= 0) & (t + off < l)
        acc = acc + jnp.where(valid, shifted, 0.0) * w_ref[k:k + 1, :]
    o_ref[...] = _silu(acc)


def ssd_conv(p, conv_w, conv_b, seq_len, n_seq, row_block_off, out_prev):
    rows = p.shape[0]
    ct = 512
    col_off = SSD_D_INNER // ct
    in_specs = [pl.BlockSpec((seq_len, ct), lambda s, c: (s + row_block_off, c + col_off)),
                pl.BlockSpec((SSD_CONV_W, ct), lambda s, c: (0, c)),
                pl.BlockSpec((1, ct), lambda s, c: (0, c))]
    args = [p, conv_w, conv_b.reshape(1, SSD_XBC)]
    aliases = {}
    if out_prev is not None:
        in_specs.append(pl.BlockSpec(memory_space=pl.ANY))
        args.append(out_prev)
        aliases = {3: 0}

    def kern(u_ref, w_ref, b_ref, *rest):
        _conv_kernel(u_ref, w_ref, b_ref, rest[-1])

    return pl.pallas_call(
        kern,
        grid=(n_seq, SSD_XBC // ct),
        in_specs=in_specs,
        out_specs=pl.BlockSpec((seq_len, ct), lambda s, c: (s + row_block_off, c)),
        out_shape=jax.ShapeDtypeStruct((rows, SSD_XBC), F32),
        input_output_aliases=aliases,
        compiler_params=_cparams(("parallel", "parallel")),
        name="ssd_conv",
    )(*args)


def _dt_kernel(p_ref, b_ref, o_ref):
    o_ref[...] = jax.nn.softplus(p_ref[...] + b_ref[...])


def ssd_dt(p, dt_bias, tl):
    rows, w = p.shape
    return pl.pallas_call(
        _dt_kernel,
        grid=(rows // tl,),
        in_specs=[pl.BlockSpec((tl, w), lambda i: (i, 0)),
                  pl.BlockSpec((1, w), lambda i: (0, 0))],
        out_specs=pl.BlockSpec((tl, w), lambda i: (i, 0)),
        out_shape=jax.ShapeDtypeStruct((rows, w), F32),
        compiler_params=_cparams(("parallel",)),
        name="ssd_dt",
    )(p, dt_bias.reshape(1, w))


def _ssd_scan_kernel(x_ref, b_ref, c_ref, dtc_ref, dtr_ref, alc_ref, alr_ref, dsk_ref, s0_ref, *rest, nc):
    y_ref, sfin_ref, s_scr = rest[-3], rest[-2], rest[-1]
    q = SSD_CHUNK
    e_n = SSD_HEADS_PER_GROUP
    row = lax.broadcasted_iota(jnp.int32, (q, q), 0)
    col = lax.broadcasted_iota(jnp.int32, (q, q), 1)
    lane = lax.broadcasted_iota(jnp.int32, (q, q), 1)
    left = lane < SSD_HEAD_DIM

    for d in range(2):
        a_c = -jnp.exp(alc_ref[0][:, d * e_n:(d + 1) * e_n])
        a_r = -jnp.exp(alr_ref[0][d * e_n:(d + 1) * e_n, :])
        if d == 0:
            incl = col <= row
            edge = q - 1
        else:
            incl = col >= row
            edge = 0
        tri = incl.astype(F32)
        tri_t = (row <= col if d == 0 else row >= col).astype(F32)
        s_scr[...] = s0_ref[0, d]

        def body(ci, carry, d=d, a_c=a_c, a_r=a_r, incl=incl, tri=tri, tri_t=tri_t, edge=edge):
            c = ci if d == 0 else nc - 1 - ci
            t0 = pl.multiple_of(c * q, q)
            x = x_ref[pl.ds(t0, q), :]
            bm = b_ref[pl.ds(t0, q), :]
            cm = c_ref[pl.ds(t0, q), :]
            dtc = dtc_ref[0, pl.ds(t0, q), :][:, d * e_n:(d + 1) * e_n]
            dtr = dtr_ref[0, c][d * e_n:(d + 1) * e_n, :]
            cs = jnp.dot(tri, dtc * a_c, precision=HIGHEST, preferred_element_type=F32)
            cs_t = jnp.dot(dtr * a_r, tri_t, precision=HIGHEST, preferred_element_type=F32)
            tot_t = cs_t[:, edge:edge + 1]
            w_t = jnp.exp(tot_t - cs_t) * dtr
            et_b = jnp.broadcast_to(jnp.exp(tot_t), (e_n, q))
            ecs = jnp.exp(cs)
            cb = lax.dot_general(cm.astype(BF16), bm.astype(BF16), (((1,), (1,)), ((), ())),
                                 preferred_element_type=F32)
            bm_t = bm.T
            for pr in range(e_n // 2):
                sl = slice(pr * 2 * SSD_HEAD_DIM, (pr + 1) * 2 * SSD_HEAD_DIM)
                mix, bw, cw = [], [], []
                for e in (2 * pr, 2 * pr + 1):
                    seg = jnp.exp(jnp.where(incl, cs[:, e:e + 1] - cs_t[e:e + 1, :], -jnp.inf))
                    mix.append((cb * seg * dtr[e:e + 1, :]).astype(BF16))
                    bw.append((bm_t * w_t[e:e + 1, :]).astype(BF16))
                    cw.append((cm * ecs[:, e:e + 1]).astype(BF16))
                xs = x[:, sl]
                ss = s_scr[:, sl]
                lhs1 = jnp.concatenate([jnp.concatenate(mix, axis=1), jnp.concatenate(bw, axis=1)], axis=0)
                rhs1 = jnp.concatenate([jnp.where(left, xs, 0.0), jnp.where(left, 0.0, xs)], axis=0).astype(BF16)
                r1 = jnp.dot(lhs1, rhs1, preferred_element_type=F32)
                rhs2 = jnp.concatenate([jnp.where(left, ss, 0.0), jnp.where(left, 0.0, ss)], axis=0).astype(BF16)
                y = r1[:q] + jnp.dot(jnp.concatenate(cw, axis=1), rhs2, preferred_element_type=F32)
                dec = jnp.where(left[:1], et_b[2 * pr:2 * pr + 1, :], et_b[2 * pr + 1:2 * pr + 2, :])
                s_scr[:, sl] = ss * dec + r1[q:]
                if d == 0:
                    y_ref[pl.ds(t0, q), sl] = y + dsk_ref[:, sl] * xs
                else:
                    y_ref[pl.ds(t0, q), sl] = y_ref[pl.ds(t0, q), sl] + y
            return carry

        lax.fori_loop(0, nc, body, 0)
        sfin_ref[0, d] = s_scr[...]


def ssd_scan(xbc, dt_col, dt_row, a_log, d_skip, s0, seq_len, n_seq, row_block_off, y_prev):
    rows = xbc.shape[0]
    g_n, e_n, gw = SSD_GROUPS, SSD_HEADS_PER_GROUP, SSD_GROUP_WIDTH
    nc = seq_len // SSD_CHUNK
    b_off = SSD_D_INNER // SSD_D_STATE
    c_off = (SSD_D_INNER + SSD_BC) // SSD_D_STATE
    al = a_log.reshape(2, g_n, e_n).transpose(1, 0, 2).reshape(g_n, 1, 2 * e_n)
    al_r = jnp.broadcast_to(al.reshape(g_n, 2 * e_n, 1), (g_n, 2 * e_n, V7X_LANES))
    dsk = jnp.repeat(d_skip, SSD_HEAD_DIM).reshape(1, SSD_D_INNER)
    in_specs = [pl.BlockSpec((seq_len, gw), lambda s, g: (s + row_block_off, g)),
                pl.BlockSpec((seq_len, SSD_D_STATE), lambda s, g: (s + row_block_off, b_off + g)),
                pl.BlockSpec((seq_len, SSD_D_STATE), lambda s, g: (s + row_block_off, c_off + g)),
                pl.BlockSpec((1, seq_len, 2 * e_n), lambda s, g: (g, s + row_block_off, 0)),
                pl.BlockSpec((1, nc, 2 * e_n, SSD_CHUNK), lambda s, g: (g, s + row_block_off, 0, 0)),
                pl.BlockSpec((1, 1, 2 * e_n), lambda s, g: (g, 0, 0)),
                pl.BlockSpec((1, 2 * e_n, V7X_LANES), lambda s, g: (g, 0, 0)),
                pl.BlockSpec((1, gw), lambda s, g: (0, g)),
                pl.BlockSpec((1, 2, SSD_D_STATE, gw), lambda s, g: (s * g_n + g, 0, 0, 0))]
    args = [xbc, xbc, xbc, dt_col, dt_row, al, al_r, dsk, s0]
    aliases = {}
    if y_prev is not None:
        in_specs.append(pl.BlockSpec(memory_space=pl.ANY))
        args.append(y_prev)
        aliases = {9: 0}
    return pl.pallas_call(
        functools.partial(_ssd_scan_kernel, nc=nc),
        grid=(n_seq, g_n),
        in_specs=in_specs,
        out_specs=[pl.BlockSpec((seq_len, gw), lambda s, g: (s + row_block_off, g)),
                   pl.BlockSpec((1, 2, SSD_D_STATE, gw), lambda s, g: (s * g_n + g, 0, 0, 0))],
        out_shape=[jax.ShapeDtypeStruct((rows, SSD_D_INNER), F32),
                   jax.ShapeDtypeStruct((n_seq * g_n, 2, SSD_D_STATE, gw), F32)],
        scratch_shapes=[pltpu.VMEM((SSD_D_STATE, gw), F32)],
        input_output_aliases=aliases,
        compiler_params=_cparams(("parallel", "parallel")),
        name="ssd_scan",
    )(*args)


def _ssd_out_kernel(y_ref, z_ref, g_ref, o_ref):
    v = y_ref[...] * _silu(z_ref[...])
    n = v * lax.rsqrt(jnp.mean(v * v, axis=-1, keepdims=True) + EPS)
    o_ref[...] = (n * g_ref[...]).astype(o_ref.dtype)


def ssd_out(y, p, norm_g, tl):
    rows = y.shape[0]
    return pl.pallas_call(
        _ssd_out_kernel,
        grid=(rows // tl,),
        in_specs=[pl.BlockSpec((tl, SSD_D_INNER), lambda i: (i, 0)),
                  pl.BlockSpec((tl, SSD_D_INNER), lambda i: (i, 0)),
                  pl.BlockSpec((1, SSD_D_INNER), lambda i: (0, 0))],
        out_specs=pl.BlockSpec((tl, SSD_D_INNER), lambda i: (i, 0)),
        out_shape=jax.ShapeDtypeStruct((rows, SSD_D_INNER), BF16),
        compiler_params=_cparams(("parallel",)),
        name="ssd_out",
    )(y, p, norm_g.reshape(1, SSD_D_INNER))


def _ret_state_kernel(k_ref, v_ref, ld_ref, s_ref):
    l = k_ref.shape[0]
    k = k_ref[...] * (RET_QK_DIM ** -0.5)
    vb = v_ref[...].astype(BF16)
    pos = lax.broadcasted_iota(jnp.int32, (l, V7X_LANES), 0).astype(F32)
    for d in range(2):
        ld = -jnp.exp(ld_ref[0, d:d + 1, :])
        steps = (l - 1.0 - pos) if d == 0 else pos
        w = jnp.exp(steps * ld)
        kd = (k * jnp.concatenate([w] * (RET_QK_DIM // V7X_LANES), axis=1)).astype(BF16)
        s_ref[0, d] = lax.dot_general(kd, vb, (((0,), (0,)), ((), ())), preferred_element_type=F32)


def ret_ctx_state(p, decay_b, geom):
    l = geom.l_ctx
    row_off = geom.r_lat // l
    k_off = D_MODEL // RET_QK_DIM
    v_off = 2 * D_MODEL // RET_V_DIM
    return pl.pallas_call(
        _ret_state_kernel,
        grid=(geom.batch, RET_HEADS),
        in_specs=[pl.BlockSpec((l, RET_QK_DIM), lambda b, h: (b + row_off, k_off + h)),
                  pl.BlockSpec((l, RET_V_DIM), lambda b, h: (b + row_off, v_off + h)),
                  pl.BlockSpec((1, 2, V7X_LANES), lambda b, h: (h, 0, 0))],
        out_specs=pl.BlockSpec((1, 2, RET_QK_DIM, RET_V_DIM), lambda b, h: (b * RET_HEADS + h, 0, 0, 0)),
        out_shape=jax.ShapeDtypeStruct((geom.batch * RET_HEADS, 2, RET_QK_DIM, RET_V_DIM), F32),
        compiler_params=_cparams(("parallel", "parallel")),
        name="ret_ctx_state",
    )(p, p, decay_b)


def _rope(u, cos, sin):
    parts = []
    for j in range(u.shape[1] // V7X_LANES):
        s = u[:, j * V7X_LANES:(j + 1) * V7X_LANES]
        parts.append(pltpu.roll(s, V7X_LANES // 2, axis=1))
    return u * cos + jnp.concatenate(parts, axis=1) * sin


def _ret_scan_kernel(q_ref, k_ref, v_ref, g_ref, cos_ref, sin_ref, ld_ref, gn_ref, s0_ref, o_ref,
                     s_scr, qb_scr, k_scr, acc_scr, *, nc):
    q = RET_CHUNK
    nslab_k = RET_QK_DIM // V7X_LANES
    nslab_v = RET_V_DIM // V7X_LANES
    l_i = lax.broadcasted_iota(jnp.int32, (q, q), 0).astype(F32)
    s_i = lax.broadcasted_iota(jnp.int32, (q, q), 1).astype(F32)
    rel = l_i - s_i
    ld_f = -jnp.exp(ld_ref[0, 0:1, :])
    ld_b = -jnp.exp(ld_ref[0, 1:2, :])
    decay_in = (jnp.exp(jnp.where(rel >= 0, rel * ld_f, -jnp.inf))
                + jnp.exp(jnp.where(rel <= 0, -rel * ld_b, -jnp.inf)))
    from_state = (jnp.exp((l_i + 1.0) * ld_f), jnp.exp((q - l_i) * ld_b))
    to_end = (jnp.exp((q - 1.0 - l_i) * ld_f), jnp.exp(l_i * ld_b))
    chunk_decay = (jnp.exp(q * ld_f), jnp.exp(q * ld_b))

    def update_state(d, k, vb):
        kd = (k * jnp.concatenate([to_end[d]] * nslab_k, axis=1)).astype(BF16)
        s_scr[...] = (s_scr[...] * jnp.concatenate([chunk_decay[d]] * nslab_v, axis=1)
                      + lax.dot_general(kd, vb, (((0,), (0,)), ((), ())), preferred_element_type=F32))

    def inter(d, qb):
        return (jnp.dot(qb, s_scr[...].astype(BF16), preferred_element_type=F32)
                * jnp.concatenate([from_state[d]] * nslab_v, axis=1))

    s_scr[...] = s0_ref[0, 0]

    def fwd(c, carry):
        t0 = pl.multiple_of(c * q, q)
        cos = cos_ref[pl.ds(t0, q), :]
        sin = sin_ref[pl.ds(t0, q), :]
        qb = _rope(q_ref[pl.ds(t0, q), :], cos, sin).astype(BF16)
        k = _rope(k_ref[pl.ds(t0, q), :], cos, sin) * (RET_QK_DIM ** -0.5)
        vb = v_ref[pl.ds(t0, q), :].astype(BF16)
        qb_scr[pl.ds(t0, q), :] = qb
        k_scr[pl.ds(t0, q), :] = k
        scores = lax.dot_general(qb, k.astype(BF16), (((1,), (1,)), ((), ())),
                                 preferred_element_type=F32) * decay_in
        acc_scr[pl.ds(t0, q), :] = jnp.dot(scores.astype(BF16), vb, preferred_element_type=F32) + inter(0, qb)
        update_state(0, k, vb)
        return carry

    lax.fori_loop(0, nc, fwd, 0)
    s_scr[...] = s0_ref[0, 1]

    def bwd(ci, carry):
        c = nc - 1 - ci
        t0 = pl.multiple_of(c * q, q)
        qb = qb_scr[pl.ds(t0, q), :]
        k = k_scr[pl.ds(t0, q), :]
        vb = v_ref[pl.ds(t0, q), :].astype(BF16)
        o = acc_scr[pl.ds(t0, q), :] + inter(1, qb)
        update_state(1, k, vb)
        mu = jnp.mean(o, axis=-1, keepdims=True)
        var = jnp.mean(jnp.square(o - mu), axis=-1, keepdims=True)
        on = ((o - mu) * lax.rsqrt(var + EPS)) * gn_ref[...]
        o_ref[pl.ds(t0, q), :] = (on * _silu(g_ref[pl.ds(t0, q), :])).astype(o_ref.dtype)
        return carry

    lax.fori_loop(0, nc, bwd, 0)


def ret_scan(p, cos_t, sin_t, decay_b, gn_g, s0, geom):
    l = geom.l_lat
    nc = l // RET_CHUNK
    k_off = D_MODEL // RET_QK_DIM
    v_off = 2 * D_MODEL // RET_V_DIM
    g_off = (2 * D_MODEL + RET_D_V) // RET_V_DIM
    return pl.pallas_call(
        functools.partial(_ret_scan_kernel, nc=nc),
        grid=(geom.batch, RET_HEADS),
        in_specs=[pl.BlockSpec((l, RET_QK_DIM), lambda b, h: (b, h)),
                  pl.BlockSpec((l, RET_QK_DIM), lambda b, h: (b, k_off + h)),
                  pl.BlockSpec((l, RET_V_DIM), lambda b, h: (b, v_off + h)),
                  pl.BlockSpec((l, RET_V_DIM), lambda b, h: (b, g_off + h)),
                  pl.BlockSpec((l, RET_QK_DIM), lambda b, h: (0, 0)),
                  pl.BlockSpec((l, RET_QK_DIM), lambda b, h: (0, 0)),
                  pl.BlockSpec((1, 2, V7X_LANES), lambda b, h: (h, 0, 0)),
                  pl.BlockSpec((1, RET_V_DIM), lambda b, h: (0, h)),
                  pl.BlockSpec((1, 2, RET_QK_DIM, RET_V_DIM), lambda b, h: (b * RET_HEADS + h, 0, 0, 0))],
        out_specs=pl.BlockSpec((l, RET_V_DIM), lambda b, h: (b, h)),
        out_shape=jax.ShapeDtypeStruct((geom.r_lat, RET_D_V), BF16),
        scratch_shapes=[pltpu.VMEM((RET_QK_DIM, RET_V_DIM), F32),
                        pltpu.VMEM((l, RET_QK_DIM), BF16),
                        pltpu.VMEM((l, RET_QK_DIM), F32),
                        pltpu.VMEM((l, RET_V_DIM), F32)],
        compiler_params=_cparams(("parallel", "parallel")),
        name="ret_scan",
    )(p, p, p, p, cos_t, sin_t, decay_b, gn_g.reshape(1, RET_D_V), s0)


def rope_tables(l_lat):
    half = RET_QK_DIM // 4
    pos = jnp.arange(l_lat)
    freqs = ROPE_BASE ** (-jnp.arange(half, dtype=F32) / half)
    cs, sn = [], []
    for ids in (pos // GRID_W, pos % GRID_W):
        ang = ids.astype(F32)[:, None] * freqs[None, :]
        cs += [jnp.cos(ang), jnp.cos(ang)]
        sn += [-jnp.sin(ang), jnp.sin(ang)]
    return jnp.concatenate(cs, axis=1), jnp.concatenate(sn, axis=1)


def _router_kernel(x_ref, g_ref, sc_ref, sh_ref, wr_ref, f_ref, ac_ref, ar_ref):
    x = x_ref[...]
    y = x * lax.rsqrt(jnp.mean(x * x, axis=-1, keepdims=True) + EPS)
    f = (y * g_ref[...]) * (1.0 + sc_ref[...]) + sh_ref[...]
    f_ref[...] = f
    logits = jnp.dot(f, wr_ref[...], precision=HIGHEST, preferred_element_type=F32)
    lane = lax.broadcasted_iota(jnp.int32, logits.shape, 1)
    logits = jnp.where(lane < N_EXPERTS, logits, -jnp.inf)
    un = jnp.exp(logits - jnp.max(logits, axis=-1, keepdims=True))
    aff = un / jnp.sum(un, axis=-1, keepdims=True)
    ac_ref[...] = aff
    ar_ref[...] = aff.T[:N_EXPERTS, :]


def moe_router(x2d, g, mod3, w_router, geom, k_shift, k_scale, rows):
    d = x2d.shape[1]
    tl = _pick_tile(geom.l_ctx, 256)
    wr = jnp.pad(w_router, ((0, 0), (0, ROUTER_LANES - N_EXPERTS)))
    return pl.pallas_call(
        _router_kernel,
        grid=(rows // tl,),
        in_specs=[pl.BlockSpec((tl, d), lambda i: (i, 0)),
                  pl.BlockSpec((1, d), lambda i: (0, 0)),
                  pl.BlockSpec((None, 1, d), lambda i: (geom.mod_row(i, tl), 0, k_scale)),
                  pl.BlockSpec((None, 1, d), lambda i: (geom.mod_row(i, tl), 0, k_shift)),
                  pl.BlockSpec((d, ROUTER_LANES), lambda i: (0, 0))],
        out_specs=[pl.BlockSpec((tl, d), lambda i: (i, 0)),
                   pl.BlockSpec((tl, ROUTER_LANES), lambda i: (i, 0)),
                   pl.BlockSpec((N_EXPERTS, tl), lambda i: (0, i))],
        out_shape=[jax.ShapeDtypeStruct((rows, d), F32),
                   jax.ShapeDtypeStruct((rows, ROUTER_LANES), F32),
                   jax.ShapeDtypeStruct((N_EXPERTS, rows), F32)],
        compiler_params=_cparams(("parallel",)),
        name="moe_router",
    )(x2d, g.reshape(1, d), mod3, mod3, wr)


def _rank_kernel(ac_ref, ar_ref, o_ref, colb_scr, *, n, cap):
    e = pl.program_id(1)
    sub, lanes = V7X_SUBLANES, V7X_LANES
    lane_c = lax.broadcasted_iota(jnp.int32, (n, lanes), 1)
    a_col = jnp.sum(jnp.where(lane_c == e, ac_ref[...], 0.0), axis=1, keepdims=True)
    colb_scr[...] = jnp.broadcast_to(a_col, (n, lanes))
    a_row = ar_ref[pl.ds(e, 1), :]
    m_in = lax.broadcasted_iota(jnp.int32, (sub, lanes), 0)
    t_in = lax.broadcasted_iota(jnp.int32, (sub, lanes), 1)
    nt = n // lanes
    a_t = [jnp.broadcast_to(a_row[:, j * lanes:(j + 1) * lanes], (sub, lanes)) for j in range(nt)]
    accs = tuple(jnp.zeros((sub, lanes), jnp.int32) for _ in range(nt))
    for js in range(nt):

        def body(i, accs, js=js):
            a_m = colb_scr[pl.ds(pl.multiple_of(js * lanes + i * sub, sub), sub), :]
            out = []
            for j in range(nt):
                gt = (a_m > a_t[j]).astype(jnp.int32)
                ge = (a_m >= a_t[j]).astype(jnp.int32)
                if j < js:
                    hit = gt
                elif j > js:
                    hit = ge
                else:
                    hit = jnp.where((i * sub + m_in) < t_in, ge, gt)
                out.append(accs[j] + hit)
            return tuple(out)

        accs = lax.fori_loop(0, lanes // sub, body, accs)
    rank = jnp.concatenate([jnp.sum(a, axis=0, keepdims=True) for a in accs], axis=1)
    r_col = lax.broadcasted_iota(jnp.int32, (cap, n), 0)
    t_row = lax.broadcasted_iota(jnp.int32, (cap, n), 1).astype(F32)
    onehot = rank == r_col
    idx = jnp.sum(jnp.where(onehot, t_row, 0.0), axis=1, keepdims=True)
    gate = jnp.sum(jnp.where(onehot, a_row, 0.0), axis=1, keepdims=True)
    lane_o = lax.broadcasted_iota(jnp.int32, (cap, lanes), 1)
    packed = jnp.where(lane_o == 0, idx, jnp.where(lane_o == 1, gate, 0.0))
    cap_pad = o_ref.shape[-1]
    if cap_pad > cap:
        packed = jnp.concatenate([packed, jnp.zeros((cap_pad - cap, lanes), F32)], axis=0)
    o_ref[0] = packed.T[:sub, :]


def moe_rank(aff_col, aff_row, n, n_sets, row_block_off):
    cap = EC_CAPACITY * n // N_EXPERTS
    cap_pad = max(cap, V7X_LANES)
    packed = pl.pallas_call(
        functools.partial(_rank_kernel, n=n, cap=cap),
        grid=(n_sets, N_EXPERTS),
        in_specs=[pl.BlockSpec((n, ROUTER_LANES), lambda b, e: (b + row_block_off, 0)),
                  pl.BlockSpec((N_EXPERTS, n), lambda b, e: (0, b + row_block_off))],
        out_specs=pl.BlockSpec((1, V7X_SUBLANES, cap_pad), lambda b, e: (b * N_EXPERTS + e, 0, 0)),
        out_shape=jax.ShapeDtypeStruct((n_sets * N_EXPERTS, V7X_SUBLANES, cap_pad), F32),
        scratch_shapes=[pltpu.VMEM((n, ROUTER_LANES), F32)],
        compiler_params=_cparams(("parallel", "arbitrary")),
        name="moe_rank",
    )(aff_col, aff_row)
    return packed[:, :, :cap]


def _gather_kernel(idx_ref, f_ref, *rest, cap):
    o_ref, buf = rest[-2], rest[-1]

    def body(r, carry):
        buf[pl.ds(r, 1), :] = f_ref[pl.ds(idx_ref[0, 0, r], 1), :]
        return carry

    lax.fori_loop(0, cap, body, 0, unroll=8)
    o_ref[...] = buf[...].astype(o_ref.dtype)


def moe_gather(idx, f2d, n, n_sets, row_block_off, slot_block_off, m_total, xe_prev):
    d = f2d.shape[1]
    cap = idx.shape[-1]
    dt = 1024
    in_specs = [pl.BlockSpec((1, 1, cap), lambda b, c, e: (b * N_EXPERTS + e, 0, 0), memory_space=pltpu.SMEM),
                pl.BlockSpec((n, dt), lambda b, c, e: (b + row_block_off, c))]
    args = [idx, f2d]
    aliases = {}
    if xe_prev is not None:
        in_specs.append(pl.BlockSpec(memory_space=pl.ANY))
        args.append(xe_prev)
        aliases = {2: 0}
    return pl.pallas_call(
        functools.partial(_gather_kernel, cap=cap),
        grid=(n_sets, d // dt, N_EXPERTS),
        in_specs=in_specs,
        out_specs=pl.BlockSpec((None, cap, dt), lambda b, c, e: (e, b + slot_block_off, c)),
        out_shape=jax.ShapeDtypeStruct((N_EXPERTS, m_total, d), BF16),
        scratch_shapes=[pltpu.VMEM((cap, dt), F32)],
        input_output_aliases=aliases,
        compiler_params=_cparams(("parallel", "parallel", "arbitrary")),
        name="moe_gather",
    )(*args)


def _glu_kernel(x_ref, wg_ref, wu_ref, o_ref):
    x = x_ref[...]
    hg = jnp.dot(x, wg_ref[...].astype(BF16), preferred_element_type=F32)
    hu = jnp.dot(x, wu_ref[...].astype(BF16), preferred_element_type=F32)
    o_ref[...] = (_silu(hg) * hu).astype(o_ref.dtype)


def moe_glu(xe, w_gate, w_up, layer):
    e_n, m, d = xe.shape
    f = w_gate.shape[-1]
    tn = 256
    return pl.pallas_call(
        _glu_kernel,
        grid=(e_n, f // tn),
        in_specs=[pl.BlockSpec((None, m, d), lambda e, j: (e, 0, 0)),
                  pl.BlockSpec((None, None, d, tn), lambda e, j: (layer, e, 0, j)),
                  pl.BlockSpec((None, None, d, tn), lambda e, j: (layer, e, 0, j))],
        out_specs=pl.BlockSpec((None, m, tn), lambda e, j: (e, 0, j)),
        out_shape=jax.ShapeDtypeStruct((e_n, m, f), BF16),
        compiler_params=_cparams(("parallel", "arbitrary")),
        name="moe_glu",
    )(xe, w_gate, w_up)


def _down_kernel(h_ref, wd_ref, gate_ref, g_ref, o_ref, *, row_groups):
    acc = jnp.dot(h_ref[...], wd_ref[...].astype(BF16), preferred_element_type=F32)
    for start, size, mrow in row_groups:
        o_ref[start:start + size, :] = (acc[start:start + size, :] * gate_ref[start:start + size, :]) * g_ref[mrow]


def moe_down(hid, w_down, gate_col, mod3, k_gate, row_groups, layer):
    e_n, m, f = hid.shape
    d = w_down.shape[-1]
    tn = 256
    nj = d // tn
    return pl.pallas_call(
        functools.partial(_down_kernel, row_groups=row_groups),
        grid=(e_n, nj),
        in_specs=[pl.BlockSpec((None, m, f), lambda e, j: (e, 0, 0)),
                  pl.BlockSpec((None, None, f, tn), lambda e, j: (layer, e, 0, j)),
                  pl.BlockSpec((None, m, 1), lambda e, j: (e, 0, 0)),
                  pl.BlockSpec((MOD_ROWS, 1, tn), lambda e, j: (0, 0, k_gate * nj + j))],
        out_specs=pl.BlockSpec((None, m, tn), lambda e, j: (e, 0, j)),
        out_shape=jax.ShapeDtypeStruct((e_n, m, d), F32),
        compiler_params=_cparams(("parallel", "arbitrary")),
        name="moe_down",
    )(hid, w_down, gate_col, mod3)


COMBINE_GROUP = 8


def _combine_kernel(idx_ref, ye_ref, h_ref, *rest, cap):
    o_ref = rest[-1]
    e = pl.program_id(2)

    @pl.when(e == 0)
    def _():
        o_ref[...] = h_ref[...]

    def body(g, carry):
        base = g * COMBINE_GROUP
        toks = [idx_ref[0, 0, base + u] for u in range(COMBINE_GROUP)]
        rows = [o_ref[pl.ds(t, 1), :] + ye_ref[pl.ds(base + u, 1), :] for u, t in enumerate(toks)]
        for t, row in zip(toks, rows):
            o_ref[pl.ds(t, 1), :] = row
        return carry

    lax.fori_loop(0, cap // COMBINE_GROUP, body, 0)


def moe_combine(idx, ye, x2d, n, n_sets, row_block_off, slot_block_off, out_rows, out_prev):
    d = x2d.shape[1]
    cap = idx.shape[-1]
    dt = 1024
    in_specs = [pl.BlockSpec((1, 1, cap), lambda b, c, e: (b * N_EXPERTS + e, 0, 0), memory_space=pltpu.SMEM),
                pl.BlockSpec((None, cap, dt), lambda b, c, e: (e, b + slot_block_off, c)),
                pl.BlockSpec((n, dt), lambda b, c, e: (b + row_block_off, c))]
    args = [idx, ye, x2d]
    aliases = {}
    if out_prev is not None:
        in_specs.append(pl.BlockSpec(memory_space=pl.ANY))
        args.append(out_prev)
        aliases = {3: 0}
    return pl.pallas_call(
        functools.partial(_combine_kernel, cap=cap),
        grid=(n_sets, d // dt, N_EXPERTS),
        in_specs=in_specs,
        out_specs=pl.BlockSpec((n, dt), lambda b, c, e: (b + row_block_off, c)),
        out_shape=jax.ShapeDtypeStruct((out_rows, d), F32),
        input_output_aliases=aliases,
        compiler_params=_cparams(("parallel", "parallel", "arbitrary")),
        name="moe_combine",
    )(*args)


def _split_rank(packed, n_sets):
    cap = packed.shape[-1]
    idx = packed[:, 0, :].astype(jnp.int32).reshape(n_sets * N_EXPERTS, 1, cap)
    gate = packed[:, 1, :].reshape(n_sets, N_EXPERTS, cap).transpose(1, 0, 2).reshape(N_EXPERTS, n_sets * cap)
    return idx, gate


def ec_moe(x2d, norm_g, mod3, w_router, w_gate, w_up, w_down, layer, geom, with_ctx):
    rows = geom.rows if with_ctx else geom.r_lat
    f2d, aff_col, aff_row = moe_router(x2d, norm_g, mod3, w_router, geom, 3, 4, rows)
    sets = [(geom.l_lat, 0)]
    if with_ctx:
        sets.append((geom.l_ctx, geom.r_lat // geom.l_ctx))
    caps = [EC_CAPACITY * n // N_EXPERTS for n, _ in sets]
    m_total = sum(geom.batch * c for c in caps)
    routed, row_groups, xe, slot_off = [], [], None, 0
    for (n, row_off), cap in zip(sets, caps):
        idx, gate = _split_rank(moe_rank(aff_col, aff_row, n, geom.batch, row_off), geom.batch)
        xe = moe_gather(idx, f2d, n, geom.batch, row_off, slot_off // cap, m_total, xe)
        routed.append((idx, gate, n, row_off, slot_off // cap))
        for b in range(geom.batch):
            row_groups.append((slot_off + b * cap, cap, b if row_off == 0 else geom.batch))
        slot_off += geom.batch * cap
    gate_col = jnp.concatenate([r[1] for r in routed], axis=1).reshape(N_EXPERTS, m_total, 1)
    ye = moe_down(moe_glu(xe, w_gate, w_up, layer), w_down, gate_col, mod3, 5, tuple(row_groups), layer)
    out = None
    for idx, _, n, row_off, slot_blk in routed:
        out = moe_combine(idx, ye, x2d, n, geom.batch, row_off, slot_blk, rows, out)
    return out


def _final_norm_kernel(x_ref, g_ref, o_ref):
    x = x_ref[...]
    o_ref[...] = (x * lax.rsqrt(jnp.mean(x * x, axis=-1, keepdims=True) + EPS)) * g_ref[...]


def final_norm(x2d, g, rows, tl):
    d = x2d.shape[1]
    return pl.pallas_call(
        _final_norm_kernel,
        grid=(rows // tl,),
        in_specs=[pl.BlockSpec((tl, d), lambda i: (i, 0)), pl.BlockSpec((1, d), lambda i: (0, 0))],
        out_specs=pl.BlockSpec((tl, d), lambda i: (i, 0)),
        out_shape=jax.ShapeDtypeStruct((rows, d), F32),
        compiler_params=_cparams(("parallel",)),
        name="final_norm",
    )(x2d, g.reshape(1, d))


def _group_dt(dt, rows):
    g_n, e_n = SSD_GROUPS, SSD_HEADS_PER_GROUP
    dt_col = dt.reshape(rows, 2, g_n, e_n).transpose(2, 0, 1, 3).reshape(g_n, rows, 2 * e_n)
    dt_row = dt_col.reshape(g_n, rows // SSD_CHUNK, SSD_CHUNK, 2 * e_n).transpose(0, 1, 3, 2)
    return dt_col, dt_row


def ssd_layer(x2d, mod3, norm_mix_g, w_in, conv_w, conv_b, dt_bias, a_log, d_skip, norm_g, w_out, geom, tm):
    a = norm_mod(x2d, norm_mix_g, mod3, geom, 0, 1, geom.rows, BF16)
    n_main = SSD_D_INNER + SSD_XBC
    p = matmul(a, w_in, tm, 512, 0, n_main)
    p_dt = matmul(a, w_in, tm, 2 * SSD_HEADS, n_main, 2 * SSD_HEADS)
    ctx_blk = geom.r_lat // geom.l_ctx
    xbc = ssd_conv(p, conv_w, conv_b, geom.l_ctx, geom.batch, ctx_blk, None)
    xbc = ssd_conv(p, conv_w, conv_b, geom.l_lat, geom.batch, 0, xbc)
    dt = ssd_dt(p_dt, dt_bias, tm)
    dt_col, dt_row = _group_dt(dt, geom.rows)
    zero = jnp.zeros((geom.batch * SSD_GROUPS, 2, SSD_D_STATE, SSD_GROUP_WIDTH), F32)
    y, s_ctx = ssd_scan(xbc, dt_col, dt_row, a_log, d_skip, zero, geom.l_ctx, geom.batch, ctx_blk, None)
    y, _ = ssd_scan(xbc, dt_col, dt_row, a_log, d_skip, s_ctx, geom.l_lat, geom.batch, 0, y)
    yb = ssd_out(y, p, norm_g, _pick_tile(geom.l_ctx, 256))
    return matmul_resid(yb, w_out, x2d, mod3, geom, 2, tm, 512)


def ret_layer(x2d, mod3, norm_mix_g, w_in, decay, gn_g, w_out, geom, tm):
    a = norm_mod(x2d, norm_mix_g, mod3, geom, 0, 1, geom.rows, BF16)
    p = matmul(a, w_in, tm, 512)
    decay_b = jnp.broadcast_to(decay.T.reshape(RET_HEADS, 2, 1), (RET_HEADS, 2, V7X_LANES))
    cos_t, sin_t = rope_tables(geom.l_lat)
    s_ctx = ret_ctx_state(p, decay_b, geom)
    yb = ret_scan(p, cos_t, sin_t, decay_b, gn_g, s_ctx, geom)
    return matmul_resid(yb, w_out, x2d, mod3, geom, 2, tm, 512)


def kernel(x, c, ctx, c_ctx, ada_w, ada_b, norm_mix_g, norm_ffn_g, ssd_w_in, ssd_conv_w, ssd_conv_b,
           ssd_dt_bias, ssd_a_log, ssd_d, ssd_norm_g, ssd_w_out, ret_w_in, ret_decay, ret_gn_g, ret_w_out,
           moe_w_router, moe_w_gate, moe_w_up, moe_w_down, final_norm_g):
    batch, l_lat, d = x.shape
    l_ctx = ctx.shape[1]
    depth = ada_w.shape[0]
    n_mixers = 2
    assert batch + 1 <= MOD_ROWS and d == D_MODEL
    assert depth == n_mixers, "the retention layer must be the last one: its context branch is state-only"
    geom = Geom(batch, l_lat, l_ctx)
    tm = _pick_tile(geom.r_ctx, 1024)
    x2d = jnp.concatenate([x.reshape(geom.r_lat, d), ctx.reshape(geom.r_ctx, d)], axis=0)
    cc = jnp.zeros((MOD_ROWS, d), F32).at[:batch].set(c).at[batch].set(c_ctx)
    mod = ada_table(cc, ada_w, ada_b)
    for i in range(depth):
        need_ctx = i < depth - 1
        j = i // n_mixers
        mod3 = mod[i].reshape(MOD_ROWS, 1, N_MOD * d)
        if i % n_mixers == 0:
            x2d = ssd_layer(x2d, mod3, norm_mix_g[i], ssd_w_in[j], ssd_conv_w[j], ssd_conv_b[j], ssd_dt_bias[j],
                            ssd_a_log[j], ssd_d[j], ssd_norm_g[j], ssd_w_out[j], geom, tm)
        else:
            x2d = ret_layer(x2d, mod3, norm_mix_g[i], ret_w_in[j], ret_decay[j], ret_gn_g[j], ret_w_out[j],
                            geom, tm)
        x2d = ec_moe(x2d, norm_ffn_g[i], mod3, moe_w_router[i], moe_w_gate, moe_w_up, moe_w_down, i, geom,
                     need_ctx)
    out = final_norm(x2d, final_norm_g, geom.r_lat, _pick_tile(geom.l_ctx, 256))
    return out.reshape(batch, l_lat, d)
```

```python
import functools

import jax
import jax.numpy as jnp
from jax import lax
from jax.experimental import pallas as pl
from jax.experimental.pallas import tpu as pltpu

F32 = jnp.float32
BF16 = jnp.bfloat16
HIGHEST = lax.Precision.HIGHEST

D_MODEL = 2048
N_MOD = 6
EPS = 1e-6

SSD_D_INNER = 2 * D_MODEL
SSD_HEAD_DIM = 64
SSD_HEADS = SSD_D_INNER // SSD_HEAD_DIM
SSD_GROUPS = 8
SSD_HEADS_PER_GROUP = SSD_HEADS // SSD_GROUPS
SSD_D_STATE = 128
SSD_CONV_W = 5
SSD_CHUNK = 128
SSD_BC = SSD_GROUPS * SSD_D_STATE
SSD_XBC = SSD_D_INNER + 2 * SSD_BC
SSD_IN = SSD_D_INNER + SSD_XBC + 2 * SSD_HEADS
SSD_GROUP_WIDTH = SSD_HEADS_PER_GROUP * SSD_HEAD_DIM

RET_HEADS = 8
RET_QK_DIM = D_MODEL // RET_HEADS
RET_V_DIM = 2 * RET_QK_DIM
RET_D_V = RET_HEADS * RET_V_DIM
RET_CHUNK = 128
RET_SCAN_CHUNK = 256
RET_IN = 2 * D_MODEL + 2 * RET_D_V
ROPE_BASE = 10000.0
GRID_W = 64

N_EXPERTS = 16
EC_CAPACITY = 2
D_EXPERT = D_MODEL

V7X_LANES = 128
V7X_SUBLANES = 8
V7X_VMEM_LIMIT_BYTES = 56 * 1024 * 1024
MOD_ROWS = 16
ROUTER_LANES = V7X_LANES
IN_PROJ_ROWS = 2048


def _cparams(sem, vmem=V7X_VMEM_LIMIT_BYTES):
    return pltpu.CompilerParams(dimension_semantics=sem, vmem_limit_bytes=vmem)


def _silu(x):
    return x * jax.nn.sigmoid(x)


def _pick_tile(n, target):
    t = min(n, target)
    while n % t:
        t //= 2
    return t


def _ada_kernel(c_ref, w_ref, b_ref, o_ref):
    a = _silu(c_ref[...]).astype(BF16)
    o_ref[...] = jnp.dot(a, w_ref[...].astype(BF16), preferred_element_type=F32) + b_ref[...]


def ada_table(cc, ada_w, ada_b):
    depth, d, n = ada_w.shape
    tn = 1024
    return pl.pallas_call(
        _ada_kernel,
        grid=(depth, n // tn),
        in_specs=[pl.BlockSpec((MOD_ROWS, d), lambda i, j: (0, 0)),
                  pl.BlockSpec((None, d, tn), lambda i, j: (i, 0, j)),
                  pl.BlockSpec((None, 1, tn), lambda i, j: (i, 0, j))],
        out_specs=pl.BlockSpec((None, MOD_ROWS, tn), lambda i, j: (i, 0, j)),
        out_shape=jax.ShapeDtypeStruct((depth, MOD_ROWS, n), F32),
        compiler_params=_cparams(("parallel", "parallel")),
        name="ada_table",
    )(cc, ada_w, ada_b.reshape(depth, 1, n))


class Geom:
    def __init__(self, batch, l_lat, l_ctx):
        self.batch, self.l_lat, self.l_ctx = batch, l_lat, l_ctx
        self.r_lat = batch * l_lat
        self.r_ctx = batch * l_ctx
        self.rows = self.r_lat + self.r_ctx

    def mod_row(self, i, tile):
        return jnp.where(i * tile < self.r_lat, (i * tile) // self.l_lat, self.batch)


def _norm_mod_kernel(x_ref, g_ref, sc_ref, sh_ref, o_ref):
    x = x_ref[...]
    y = x * lax.rsqrt(jnp.mean(x * x, axis=-1, keepdims=True) + EPS)
    o_ref[...] = ((y * g_ref[...]) * (1.0 + sc_ref[...]) + sh_ref[...]).astype(o_ref.dtype)


def norm_mod(x2d, g, mod3, geom, k_shift, k_scale, rows, out_dtype):
    d = x2d.shape[1]
    tl = _pick_tile(geom.l_ctx, 256)
    return pl.pallas_call(
        _norm_mod_kernel,
        grid=(rows // tl,),
        in_specs=[pl.BlockSpec((tl, d), lambda i: (i, 0)),
                  pl.BlockSpec((1, d), lambda i: (0, 0)),
                  pl.BlockSpec((None, 1, d), lambda i: (geom.mod_row(i, tl), 0, k_scale)),
                  pl.BlockSpec((None, 1, d), lambda i: (geom.mod_row(i, tl), 0, k_shift))],
        out_specs=pl.BlockSpec((tl, d), lambda i: (i, 0)),
        out_shape=jax.ShapeDtypeStruct((rows, d), out_dtype),
        compiler_params=_cparams(("parallel",)),
        name="norm_mod",
    )(x2d, g.reshape(1, d), mod3, mod3)


def _mm_kernel(a_ref, w_ref, o_ref):
    o_ref[...] = jnp.dot(a_ref[...], w_ref[...].astype(BF16), preferred_element_type=F32)


def matmul(a, w, tm, tn, col_start=0, n=None):
    m, k = a.shape
    n = w.shape[1] - col_start if n is None else n
    assert n % tn == 0 and col_start % tn == 0 and m % tm == 0
    j0 = col_start // tn
    return pl.pallas_call(
        _mm_kernel,
        grid=(m // tm, n // tn),
        in_specs=[pl.BlockSpec((tm, k), lambda i, j: (i, 0)),
                  pl.BlockSpec((k, tn), lambda i, j: (0, j + j0))],
        out_specs=pl.BlockSpec((tm, tn), lambda i, j: (i, j)),
        out_shape=jax.ShapeDtypeStruct((m, n), F32),
        compiler_params=_cparams(("parallel", "arbitrary")),
        name="matmul",
    )(a, w)


def _mm_resid_kernel(a_ref, w_ref, r_ref, g_ref, o_ref):
    acc = jnp.dot(a_ref[...], w_ref[...].astype(BF16), preferred_element_type=F32)
    o_ref[...] = r_ref[...] + g_ref[...] * acc


def matmul_resid(a, w, res, mod3, geom, k_gate, tm, tn):
    m, k = a.shape
    n = w.shape[1]
    nj = n // tn
    return pl.pallas_call(
        _mm_resid_kernel,
        grid=(m // tm, nj),
        in_specs=[pl.BlockSpec((tm, k), lambda i, j: (i, 0)),
                  pl.BlockSpec((k, tn), lambda i, j: (0, j)),
                  pl.BlockSpec((tm, tn), lambda i, j: (i, j)),
                  pl.BlockSpec((None, 1, tn), lambda i, j: (geom.mod_row(i, tm), 0, k_gate * nj + j))],
        out_specs=pl.BlockSpec((tm, tn), lambda i, j: (i, j)),
        out_shape=jax.ShapeDtypeStruct((m, n), F32),
        compiler_params=_cparams(("parallel", "arbitrary")),
        name="matmul_resid",
    )(a, w, res, mod3)


def _conv_kernel(u_ref, w_ref, b_ref, o_ref):
    u = u_ref[...]
    l = u.shape[0]
    pad = SSD_CONV_W // 2
    t = lax.broadcasted_iota(jnp.int32, u.shape, 0)
    acc = u * w_ref[pad:pad + 1, :] + b_ref[...]
    for k in range(SSD_CONV_W):
        off = k - pad
        if off == 0:
            continue
        shifted = pltpu.roll(u, (-off) % l, axis=0)
        valid = (t + off >= 0) & (t + off < l)
        acc = acc + jnp.where(valid, shifted, 0.0) * w_ref[k:k + 1, :]
    o_ref[...] = _silu(acc)


def ssd_conv(p, conv_w, conv_b, seq_len, n_seq, row_block_off, out_prev):
    rows = p.shape[0]
    ct = 512
    col_off = SSD_D_INNER // ct
    in_specs = [pl.BlockSpec((seq_len, ct), lambda s, c: (s + row_block_off, c + col_off)),
                pl.BlockSpec((SSD_CONV_W, ct), lambda s, c: (0, c)),
                pl.BlockSpec((1, ct), lambda s, c: (0, c))]
    args = [p, conv_w, conv_b.reshape(1, SSD_XBC)]
    aliases = {}
    if out_prev is not None:
        in_specs.append(pl.BlockSpec(memory_space=pl.ANY))
        args.append(out_prev)
        aliases = {3: 0}

    def kern(u_ref, w_ref, b_ref, *rest):
        _conv_kernel(u_ref, w_ref, b_ref, rest[-1])

    return pl.pallas_call(
        kern,
        grid=(n_seq, SSD_XBC // ct),
        in_specs=in_specs,
        out_specs=pl.BlockSpec((seq_len, ct), lambda s, c: (s + row_block_off, c)),
        out_shape=jax.ShapeDtypeStruct((rows, SSD_XBC), F32),
        input_output_aliases=aliases,
        compiler_params=_cparams(("parallel", "parallel")),
        name="ssd_conv",
    )(*args)


def _dt_kernel(p_ref, b_ref, o_ref):
    o_ref[...] = jax.nn.softplus(p_ref[...] + b_ref[...])


def ssd_dt(p, dt_bias, tl):
    rows, w = p.shape
    return pl.pallas_call(
        _dt_kernel,
        grid=(rows // tl,),
        in_specs=[pl.BlockSpec((tl, w), lambda i: (i, 0)),
                  pl.BlockSpec((1, w), lambda i: (0, 0))],
        out_specs=pl.BlockSpec((tl, w), lambda i: (i, 0)),
        out_shape=jax.ShapeDtypeStruct((rows, w), F32),
        compiler_params=_cparams(("parallel",)),
        name="ssd_dt",
    )(p, dt_bias.reshape(1, w))


def _ssd_scan_kernel(x_ref, b_ref, c_ref, dtc_ref, dtr_ref, alc_ref, alr_ref, dsk_ref, s0_ref, *rest, nc):
    y_ref, sfin_ref, s_scr = rest[-3], rest[-2], rest[-1]
    q = SSD_CHUNK
    e_n = SSD_HEADS_PER_GROUP
    row = lax.broadcasted_iota(jnp.int32, (q, q), 0)
    col = lax.broadcasted_iota(jnp.int32, (q, q), 1)
    left = col < SSD_HEAD_DIM

    def chunk_step(d, c):
        a_c = -jnp.exp(alc_ref[0][:, d * e_n:(d + 1) * e_n])
        a_r = -jnp.exp(alr_ref[0][d * e_n:(d + 1) * e_n, :])
        if d == 0:
            incl, incl_t, edge = col <= row, row <= col, q - 1
        else:
            incl, incl_t, edge = col >= row, row >= col, 0
        t0 = pl.multiple_of(c * q, q)
        x = x_ref[pl.ds(t0, q), :]
        bm = b_ref[pl.ds(t0, q), :]
        cm = c_ref[pl.ds(t0, q), :]
        dtc = dtc_ref[0, pl.ds(t0, q), :][:, d * e_n:(d + 1) * e_n]
        dtr = dtr_ref[0, c][d * e_n:(d + 1) * e_n, :]
        cs = jnp.dot(incl.astype(F32), dtc * a_c, precision=HIGHEST, preferred_element_type=F32)
        cs_t = jnp.dot(dtr * a_r, incl_t.astype(F32), precision=HIGHEST, preferred_element_type=F32)
        tot_t = cs_t[:, edge:edge + 1]
        w_t = jnp.exp(tot_t - cs_t) * dtr
        et_b = jnp.broadcast_to(jnp.exp(tot_t), (e_n, q))
        ecs = jnp.exp(cs)
        cb = lax.dot_general(cm.astype(BF16), bm.astype(BF16), (((1,), (1,)), ((), ())),
                             preferred_element_type=F32)
        bm_t = bm.T
        for pr in range(e_n // 2):
            sl = slice(pr * 2 * SSD_HEAD_DIM, (pr + 1) * 2 * SSD_HEAD_DIM)
            mix, bw, cw = [], [], []
            for e in (2 * pr, 2 * pr + 1):
                seg = jnp.exp(jnp.where(incl, cs[:, e:e + 1] - cs_t[e:e + 1, :], -jnp.inf))
                mix.append((cb * seg * dtr[e:e + 1, :]).astype(BF16))
                bw.append((bm_t * w_t[e:e + 1, :]).astype(BF16))
                cw.append((cm * ecs[:, e:e + 1]).astype(BF16))
            xs = x[:, sl]
            ss = s_scr[d, :, sl]
            lhs1 = jnp.concatenate([jnp.concatenate(mix, axis=1), jnp.concatenate(bw, axis=1)], axis=0)
            rhs1 = jnp.concatenate([jnp.where(left, xs, 0.0), jnp.where(left, 0.0, xs)], axis=0).astype(BF16)
            r1 = jnp.dot(lhs1, rhs1, preferred_element_type=F32)
            rhs2 = jnp.concatenate([jnp.where(left, ss, 0.0), jnp.where(left, 0.0, ss)], axis=0).astype(BF16)
            y = r1[:q] + jnp.dot(jnp.concatenate(cw, axis=1), rhs2, preferred_element_type=F32)
            dec = jnp.where(left[:1], et_b[2 * pr:2 * pr + 1, :], et_b[2 * pr + 1:2 * pr + 2, :])
            s_scr[d, :, sl] = ss * dec + r1[q:]
            if d == 0:
                y = y + dsk_ref[:, sl] * xs
            y_ref[pl.ds(t0, q), sl] = y_ref[pl.ds(t0, q), sl] + y

    s_scr[...] = s0_ref[0]
    y_ref[...] = jnp.zeros_like(y_ref)

    def body(ci, carry):
        chunk_step(0, ci)
        chunk_step(1, nc - 1 - ci)
        return carry

    lax.fori_loop(0, nc, body, 0)
    sfin_ref[0] = s_scr[...]


def ssd_scan(xbc, dt_col, dt_row, a_log, d_skip, s0, seq_len, n_seq, row_block_off, y_prev):
    rows = xbc.shape[0]
    g_n, e_n, gw = SSD_GROUPS, SSD_HEADS_PER_GROUP, SSD_GROUP_WIDTH
    nc = seq_len // SSD_CHUNK
    b_off = SSD_D_INNER // SSD_D_STATE
    c_off = (SSD_D_INNER + SSD_BC) // SSD_D_STATE
    al = a_log.reshape(2, g_n, e_n).transpose(1, 0, 2).reshape(g_n, 1, 2 * e_n)
    al_r = jnp.broadcast_to(al.reshape(g_n, 2 * e_n, 1), (g_n, 2 * e_n, V7X_LANES))
    dsk = jnp.repeat(d_skip, SSD_HEAD_DIM).reshape(1, SSD_D_INNER)
    in_specs = [pl.BlockSpec((seq_len, gw), lambda s, g: (s + row_block_off, g)),
                pl.BlockSpec((seq_len, SSD_D_STATE), lambda s, g: (s + row_block_off, b_off + g)),
                pl.BlockSpec((seq_len, SSD_D_STATE), lambda s, g: (s + row_block_off, c_off + g)),
                pl.BlockSpec((1, seq_len, 2 * e_n), lambda s, g: (g, s + row_block_off, 0)),
                pl.BlockSpec((1, nc, 2 * e_n, SSD_CHUNK), lambda s, g: (g, s + row_block_off, 0, 0)),
                pl.BlockSpec((1, 1, 2 * e_n), lambda s, g: (g, 0, 0)),
                pl.BlockSpec((1, 2 * e_n, V7X_LANES), lambda s, g: (g, 0, 0)),
                pl.BlockSpec((1, gw), lambda s, g: (0, g)),
                pl.BlockSpec((1, 2, SSD_D_STATE, gw), lambda s, g: (s * g_n + g, 0, 0, 0))]
    args = [xbc, xbc, xbc, dt_col, dt_row, al, al_r, dsk, s0]
    aliases = {}
    if y_prev is not None:
        in_specs.append(pl.BlockSpec(memory_space=pl.ANY))
        args.append(y_prev)
        aliases = {9: 0}
    return pl.pallas_call(
        functools.partial(_ssd_scan_kernel, nc=nc),
        grid=(n_seq, g_n),
        in_specs=in_specs,
        out_specs=[pl.BlockSpec((seq_len, gw), lambda s, g: (s + row_block_off, g)),
                   pl.BlockSpec((1, 2, SSD_D_STATE, gw), lambda s, g: (s * g_n + g, 0, 0, 0))],
        out_shape=[jax.ShapeDtypeStruct((rows, SSD_D_INNER), F32),
                   jax.ShapeDtypeStruct((n_seq * g_n, 2, SSD_D_STATE, gw), F32)],
        scratch_shapes=[pltpu.VMEM((2, SSD_D_STATE, gw), F32)],
        input_output_aliases=aliases,
        compiler_params=_cparams(("parallel", "parallel")),
        name="ssd_scan",
    )(*args)


def _ssd_out_kernel(y_ref, z_ref, g_ref, o_ref):
    v = y_ref[...] * _silu(z_ref[...])
    n = v * lax.rsqrt(jnp.mean(v * v, axis=-1, keepdims=True) + EPS)
    o_ref[...] = (n * g_ref[...]).astype(o_ref.dtype)


def ssd_out(y, p, norm_g, tl):
    rows = y.shape[0]
    return pl.pallas_call(
        _ssd_out_kernel,
        grid=(rows // tl,),
        in_specs=[pl.BlockSpec((tl, SSD_D_INNER), lambda i: (i, 0)),
                  pl.BlockSpec((tl, SSD_D_INNER), lambda i: (i, 0)),
                  pl.BlockSpec((1, SSD_D_INNER), lambda i: (0, 0))],
        out_specs=pl.BlockSpec((tl, SSD_D_INNER), lambda i: (i, 0)),
        out_shape=jax.ShapeDtypeStruct((rows, SSD_D_INNER), BF16),
        compiler_params=_cparams(("parallel",)),
        name="ssd_out",
    )(y, p, norm_g.reshape(1, SSD_D_INNER))


def _ret_state_kernel(k_ref, v_ref, ld_ref, s_ref):
    l = k_ref.shape[0]
    k = k_ref[...] * (RET_QK_DIM ** -0.5)
    vb = v_ref[...].astype(BF16)
    pos = lax.broadcasted_iota(jnp.int32, (l, V7X_LANES), 0).astype(F32)
    for d in range(2):
        ld = -jnp.exp(ld_ref[0, d:d + 1, :])
        steps = (l - 1.0 - pos) if d == 0 else pos
        w = jnp.exp(steps * ld)
        kd = (k * jnp.concatenate([w] * (RET_QK_DIM // V7X_LANES), axis=1)).astype(BF16)
        s_ref[0, d] = lax.dot_general(kd, vb, (((0,), (0,)), ((), ())), preferred_element_type=F32)


def ret_ctx_state(p, decay_b, geom):
    l = geom.l_ctx
    row_off = geom.r_lat // l
    k_off = D_MODEL // RET_QK_DIM
    v_off = 2 * D_MODEL // RET_V_DIM
    return pl.pallas_call(
        _ret_state_kernel,
        grid=(geom.batch, RET_HEADS),
        in_specs=[pl.BlockSpec((l, RET_QK_DIM), lambda b, h: (b + row_off, k_off + h)),
                  pl.BlockSpec((l, RET_V_DIM), lambda b, h: (b + row_off, v_off + h)),
                  pl.BlockSpec((1, 2, V7X_LANES), lambda b, h: (h, 0, 0))],
        out_specs=pl.BlockSpec((1, 2, RET_QK_DIM, RET_V_DIM), lambda b, h: (b * RET_HEADS + h, 0, 0, 0)),
        out_shape=jax.ShapeDtypeStruct((geom.batch * RET_HEADS, 2, RET_QK_DIM, RET_V_DIM), F32),
        compiler_params=_cparams(("parallel", "parallel")),
        name="ret_ctx_state",
    )(p, p, decay_b)


def _rope(u, cos, sin):
    parts = []
    for j in range(u.shape[1] // V7X_LANES):
        s = u[:, j * V7X_LANES:(j + 1) * V7X_LANES]
        parts.append(pltpu.roll(s, V7X_LANES // 2, axis=1))
    return u * cos + jnp.concatenate(parts, axis=1) * sin


def _ret_scan_kernel(q_ref, k_ref, v_ref, g_ref, cos_ref, sin_ref, ld_ref, gn_ref, s0_ref, o_ref,
                     s_scr, qb_scr, k_scr, acc_scr, *, nc):
    q = RET_SCAN_CHUNK
    nslab_k = RET_QK_DIM // V7X_LANES
    nslab_v = RET_V_DIM // V7X_LANES
    rel = (lax.broadcasted_iota(jnp.int32, (q, q), 0) - lax.broadcasted_iota(jnp.int32, (q, q), 1)).astype(F32)
    l_i = lax.broadcasted_iota(jnp.int32, (q, V7X_LANES), 0).astype(F32)
    ld_f = -jnp.exp(ld_ref[0, 0:1, :])
    ld_b = -jnp.exp(ld_ref[0, 1:2, :])
    decay_in = (jnp.exp(jnp.where(rel >= 0, rel * ld_f[:, :1], -jnp.inf))
                + jnp.exp(jnp.where(rel <= 0, -rel * ld_b[:, :1], -jnp.inf)))
    from_state = (jnp.exp((l_i + 1.0) * ld_f), jnp.exp((q - l_i) * ld_b))
    to_end = (jnp.exp((q - 1.0 - l_i) * ld_f), jnp.exp(l_i * ld_b))
    chunk_decay = (jnp.exp(q * ld_f), jnp.exp(q * ld_b))

    def update_state(d, k, vb):
        kd = (k * jnp.concatenate([to_end[d]] * nslab_k, axis=1)).astype(BF16)
        s_scr[...] = (s_scr[...] * jnp.concatenate([chunk_decay[d]] * nslab_v, axis=1)
                      + lax.dot_general(kd, vb, (((0,), (0,)), ((), ())), preferred_element_type=F32))

    def inter(d, qb):
        return (jnp.dot(qb, s_scr[...].astype(BF16), preferred_element_type=F32)
                * jnp.concatenate([from_state[d]] * nslab_v, axis=1))

    s_scr[...] = s0_ref[0, 0]

    def fwd(c, carry):
        t0 = pl.multiple_of(c * q, q)
        cos = cos_ref[pl.ds(t0, q), :]
        sin = sin_ref[pl.ds(t0, q), :]
        qb = _rope(q_ref[pl.ds(t0, q), :], cos, sin).astype(BF16)
        k = _rope(k_ref[pl.ds(t0, q), :], cos, sin) * (RET_QK_DIM ** -0.5)
        vb = v_ref[pl.ds(t0, q), :].astype(BF16)
        qb_scr[pl.ds(t0, q), :] = qb
        k_scr[pl.ds(t0, q), :] = k
        scores = lax.dot_general(qb, k.astype(BF16), (((1,), (1,)), ((), ())),
                                 preferred_element_type=F32) * decay_in
        acc_scr[pl.ds(t0, q), :] = jnp.dot(scores.astype(BF16), vb, preferred_element_type=F32) + inter(0, qb)
        update_state(0, k, vb)
        return carry

    lax.fori_loop(0, nc, fwd, 0)
    s_scr[...] = s0_ref[0, 1]

    def bwd(ci, carry):
        c = nc - 1 - ci
        t0 = pl.multiple_of(c * q, q)
        qb = qb_scr[pl.ds(t0, q), :]
        k = k_scr[pl.ds(t0, q), :]
        vb = v_ref[pl.ds(t0, q), :].astype(BF16)
        o = acc_scr[pl.ds(t0, q), :] + inter(1, qb)
        update_state(1, k, vb)
        mu = jnp.mean(o, axis=-1, keepdims=True)
        var = jnp.mean(jnp.square(o - mu), axis=-1, keepdims=True)
        on = ((o - mu) * lax.rsqrt(var + EPS)) * gn_ref[...]
        o_ref[pl.ds(t0, q), :] = (on * _silu(g_ref[pl.ds(t0, q), :])).astype(o_ref.dtype)
        return carry

    lax.fori_loop(0, nc, bwd, 0)


def ret_scan(p, cos_t, sin_t, decay_b, gn_g, s0, geom):
    l = geom.l_lat
    assert l % RET_SCAN_CHUNK == 0
    nc = l // RET_SCAN_CHUNK
    k_off = D_MODEL // RET_QK_DIM
    v_off = 2 * D_MODEL // RET_V_DIM
    g_off = (2 * D_MODEL + RET_D_V) // RET_V_DIM
    return pl.pallas_call(
        functools.partial(_ret_scan_kernel, nc=nc),
        grid=(geom.batch, RET_HEADS),
        in_specs=[pl.BlockSpec((l, RET_QK_DIM), lambda b, h: (b, h)),
                  pl.BlockSpec((l, RET_QK_DIM), lambda b, h: (b, k_off + h)),
                  pl.BlockSpec((l, RET_V_DIM), lambda b, h: (b, v_off + h)),
                  pl.BlockSpec((l, RET_V_DIM), lambda b, h: (b, g_off + h)),
                  pl.BlockSpec((l, RET_QK_DIM), lambda b, h: (0, 0)),
                  pl.BlockSpec((l, RET_QK_DIM), lambda b, h: (0, 0)),
                  pl.BlockSpec((1, 2, V7X_LANES), lambda b, h: (h, 0, 0)),
                  pl.BlockSpec((1, RET_V_DIM), lambda b, h: (0, h)),
                  pl.BlockSpec((1, 2, RET_QK_DIM, RET_V_DIM), lambda b, h: (b * RET_HEADS + h, 0, 0, 0))],
        out_specs=pl.BlockSpec((l, RET_V_DIM), lambda b, h: (b, h)),
        out_shape=jax.ShapeDtypeStruct((geom.r_lat, RET_D_V), BF16),
        scratch_shapes=[pltpu.VMEM((RET_QK_DIM, RET_V_DIM), F32),
                        pltpu.VMEM((l, RET_QK_DIM), BF16),
                        pltpu.VMEM((l, RET_QK_DIM), F32),
                        pltpu.VMEM((l, RET_V_DIM), F32)],
        compiler_params=_cparams(("parallel", "parallel")),
        name="ret_scan",
    )(p, p, p, p, cos_t, sin_t, decay_b, gn_g.reshape(1, RET_D_V), s0)


def rope_tables(l_lat):
    half = RET_QK_DIM // 4
    pos = jnp.arange(l_lat)
    freqs = ROPE_BASE ** (-jnp.arange(half, dtype=F32) / half)
    cs, sn = [], []
    for ids in (pos // GRID_W, pos % GRID_W):
        ang = ids.astype(F32)[:, None] * freqs[None, :]
        cs += [jnp.cos(ang), jnp.cos(ang)]
        sn += [-jnp.sin(ang), jnp.sin(ang)]
    return jnp.concatenate(cs, axis=1), jnp.concatenate(sn, axis=1)


def _router_kernel(x_ref, g_ref, sc_ref, sh_ref, wr_ref, f_ref, ar_ref):
    x = x_ref[...]
    y = x * lax.rsqrt(jnp.mean(x * x, axis=-1, keepdims=True) + EPS)
    f = (y * g_ref[...]) * (1.0 + sc_ref[...]) + sh_ref[...]
    f_ref[...] = f
    logits = jnp.dot(f, wr_ref[...], precision=HIGHEST, preferred_element_type=F32)
    lane = lax.broadcasted_iota(jnp.int32, logits.shape, 1)
    logits = jnp.where(lane < N_EXPERTS, logits, -jnp.inf)
    un = jnp.exp(logits - jnp.max(logits, axis=-1, keepdims=True))
    aff = un / jnp.sum(un, axis=-1, keepdims=True)
    ar_ref[...] = aff.T[:N_EXPERTS, :]


def moe_router(x2d, g, mod3, w_router, geom, k_shift, k_scale, rows):
    d = x2d.shape[1]
    tl = _pick_tile(geom.l_ctx, 256)
    wr = jnp.pad(w_router, ((0, 0), (0, ROUTER_LANES - N_EXPERTS)))
    return pl.pallas_call(
        _router_kernel,
        grid=(rows // tl,),
        in_specs=[pl.BlockSpec((tl, d), lambda i: (i, 0)),
                  pl.BlockSpec((1, d), lambda i: (0, 0)),
                  pl.BlockSpec((None, 1, d), lambda i: (geom.mod_row(i, tl), 0, k_scale)),
                  pl.BlockSpec((None, 1, d), lambda i: (geom.mod_row(i, tl), 0, k_shift)),
                  pl.BlockSpec((d, ROUTER_LANES), lambda i: (0, 0))],
        out_specs=[pl.BlockSpec((tl, d), lambda i: (i, 0)),
                   pl.BlockSpec((N_EXPERTS, tl), lambda i: (0, i))],
        out_shape=[jax.ShapeDtypeStruct((rows, d), F32),
                   jax.ShapeDtypeStruct((N_EXPERTS, rows), F32)],
        compiler_params=_cparams(("parallel",)),
        name="moe_router",
    )(x2d, g.reshape(1, d), mod3, mod3, wr)


def _select_kernel(ar_ref, o_ref, slot_scr, *, n, cap):
    e_n, lanes, sub = N_EXPERTS, V7X_LANES, V7X_SUBLANES
    a = ar_ref[...]
    bits = pltpu.bitcast(a, jnp.int32)

    def bisect(i, thr):
        cand = thr | lax.shift_left(jnp.int32(1), 30 - i)
        cnt = jnp.sum(jnp.where(bits >= cand, 1.0, 0.0), axis=1, keepdims=True)
        return jnp.where(cnt >= cap, cand, thr)

    thr = lax.fori_loop(0, 31, bisect, jnp.zeros((e_n, 1), jnp.int32))
    gt = jnp.where(bits > thr, 1.0, 0.0)
    eq = jnp.where(bits == thr, 1.0, 0.0)
    need = cap - jnp.sum(gt, axis=1, keepdims=True)
    upper = (lax.broadcasted_iota(jnp.int32, (lanes, lanes), 0)
             <= lax.broadcasted_iota(jnp.int32, (lanes, lanes), 1)).astype(BF16)
    ties_before = jnp.zeros((e_n, 1), F32)
    taken_before = jnp.zeros((e_n, 1), F32)
    for j in range(n // lanes):
        sl = slice(j * lanes, (j + 1) * lanes)
        eq_j = eq[:, sl]
        tie_incl = jnp.dot(eq_j.astype(BF16), upper, preferred_element_type=F32)
        tie_pos = ties_before + tie_incl - eq_j
        ties_before = ties_before + tie_incl[:, lanes - 1:lanes]
        sel_j = gt[:, sl] + eq_j * jnp.where(tie_pos < need, 1.0, 0.0)
        sel_incl = jnp.dot(sel_j.astype(BF16), upper, preferred_element_type=F32)
        slot_scr[:, sl] = jnp.where(sel_j > 0.0, taken_before + sel_incl - sel_j, -1.0)
        taken_before = taken_before + sel_incl[:, lanes - 1:lanes]

    r_col = lax.broadcasted_iota(jnp.int32, (cap, n), 0).astype(F32)
    t_row = lax.broadcasted_iota(jnp.int32, (cap, n), 1).astype(F32)
    lane_o = lax.broadcasted_iota(jnp.int32, (cap, lanes), 1)
    cap_pad = o_ref.shape[-1]

    def invert(e, carry):
        onehot = slot_scr[pl.ds(e, 1), :] == r_col
        idx = jnp.sum(jnp.where(onehot, t_row, 0.0), axis=1, keepdims=True)
        gate = jnp.sum(jnp.where(onehot, ar_ref[pl.ds(e, 1), :], 0.0), axis=1, keepdims=True)
        packed = jnp.where(lane_o == 0, idx, jnp.where(lane_o == 1, gate, 0.0))
        if cap_pad > cap:
            packed = jnp.concatenate([packed, jnp.zeros((cap_pad - cap, lanes), F32)], axis=0)
        o_ref[e] = packed.T[:sub, :]
        return carry

    lax.fori_loop(0, e_n, invert, 0)


def moe_select(aff_row, n, n_sets, row_block_off):
    cap = EC_CAPACITY * n // N_EXPERTS
    cap_pad = max(cap, V7X_LANES)
    packed = pl.pallas_call(
        functools.partial(_select_kernel, n=n, cap=cap),
        grid=(n_sets,),
        in_specs=[pl.BlockSpec((N_EXPERTS, n), lambda b: (0, b + row_block_off))],
        out_specs=pl.BlockSpec((N_EXPERTS, V7X_SUBLANES, cap_pad), lambda b: (b, 0, 0)),
        out_shape=jax.ShapeDtypeStruct((n_sets * N_EXPERTS, V7X_SUBLANES, cap_pad), F32),
        scratch_shapes=[pltpu.VMEM((N_EXPERTS, n), F32)],
        compiler_params=_cparams(("parallel",)),
        name="moe_select",
    )(aff_row)
    return packed[:, :, :cap]


def _gather_kernel(idx_ref, f_ref, *rest, cap):
    o_ref, buf = rest[-2], rest[-1]

    def body(r, carry):
        buf[pl.ds(r, 1), :] = f_ref[pl.ds(idx_ref[0, 0, r], 1), :]
        return carry

    lax.fori_loop(0, cap, body, 0, unroll=8)
    o_ref[...] = buf[...].astype(o_ref.dtype)


def moe_gather(idx, f2d, n, n_sets, row_block_off, slot_block_off, m_total, xe_prev):
    d = f2d.shape[1]
    cap = idx.shape[-1]
    dt = 1024
    in_specs = [pl.BlockSpec((1, 1, cap), lambda b, c, e: (b * N_EXPERTS + e, 0, 0), memory_space=pltpu.SMEM),
                pl.BlockSpec((n, dt), lambda b, c, e: (b + row_block_off, c))]
    args = [idx, f2d]
    aliases = {}
    if xe_prev is not None:
        in_specs.append(pl.BlockSpec(memory_space=pl.ANY))
        args.append(xe_prev)
        aliases = {2: 0}
    return pl.pallas_call(
        functools.partial(_gather_kernel, cap=cap),
        grid=(n_sets, d // dt, N_EXPERTS),
        in_specs=in_specs,
        out_specs=pl.BlockSpec((None, cap, dt), lambda b, c, e: (e, b + slot_block_off, c)),
        out_shape=jax.ShapeDtypeStruct((N_EXPERTS, m_total, d), BF16),
        scratch_shapes=[pltpu.VMEM((cap, dt), F32)],
        input_output_aliases=aliases,
        compiler_params=_cparams(("parallel", "parallel", "arbitrary")),
        name="moe_gather",
    )(*args)


def _glu_kernel(x_ref, wg_ref, wu_ref, o_ref):
    x = x_ref[...]
    hg = jnp.dot(x, wg_ref[...].astype(BF16), preferred_element_type=F32)
    hu = jnp.dot(x, wu_ref[...].astype(BF16), preferred_element_type=F32)
    o_ref[...] = (_silu(hg) * hu).astype(o_ref.dtype)


def moe_glu(xe, w_gate, w_up, layer):
    e_n, m, d = xe.shape
    f = w_gate.shape[-1]
    tn = 256
    return pl.pallas_call(
        _glu_kernel,
        grid=(e_n, f // tn),
        in_specs=[pl.BlockSpec((None, m, d), lambda e, j: (e, 0, 0)),
                  pl.BlockSpec((None, None, d, tn), lambda e, j: (layer, e, 0, j)),
                  pl.BlockSpec((None, None, d, tn), lambda e, j: (layer, e, 0, j))],
        out_specs=pl.BlockSpec((None, m, tn), lambda e, j: (e, 0, j)),
        out_shape=jax.ShapeDtypeStruct((e_n, m, f), BF16),
        compiler_params=_cparams(("parallel", "arbitrary")),
        name="moe_glu",
    )(xe, w_gate, w_up)


def _down_kernel(h_ref, wd_ref, gate_ref, g_ref, o_ref, *, row_groups):
    acc = jnp.dot(h_ref[...], wd_ref[...].astype(BF16), preferred_element_type=F32)
    for start, size, mrow in row_groups:
        o_ref[start:start + size, :] = (acc[start:start + size, :] * gate_ref[start:start + size, :]) * g_ref[mrow]


def moe_down(hid, w_down, gate_col, mod3, k_gate, row_groups, layer):
    e_n, m, f = hid.shape
    d = w_down.shape[-1]
    tn = 256
    nj = d // tn
    return pl.pallas_call(
        functools.partial(_down_kernel, row_groups=row_groups),
        grid=(e_n, nj),
        in_specs=[pl.BlockSpec((None, m, f), lambda e, j: (e, 0, 0)),
                  pl.BlockSpec((None, None, f, tn), lambda e, j: (layer, e, 0, j)),
                  pl.BlockSpec((None, m, 1), lambda e, j: (e, 0, 0)),
                  pl.BlockSpec((MOD_ROWS, 1, tn), lambda e, j: (0, 0, k_gate * nj + j))],
        out_specs=pl.BlockSpec((None, m, tn), lambda e, j: (e, 0, j)),
        out_shape=jax.ShapeDtypeStruct((e_n, m, d), F32),
        compiler_params=_cparams(("parallel", "arbitrary")),
        name="moe_down",
    )(hid, w_down, gate_col, mod3)


COMBINE_GROUP = 8


def _combine_kernel(idx_ref, ye_ref, h_ref, *rest, cap):
    o_ref = rest[-1]
    e = pl.program_id(2)

    @pl.when(e == 0)
    def _():
        o_ref[...] = h_ref[...]

    def body(g, carry):
        base = g * COMBINE_GROUP
        toks = [idx_ref[0, 0, base + u] for u in range(COMBINE_GROUP)]
        rows = [o_ref[pl.ds(t, 1), :] + ye_ref[pl.ds(base + u, 1), :] for u, t in enumerate(toks)]
        for t, row in zip(toks, rows):
            o_ref[pl.ds(t, 1), :] = row
        return carry

    lax.fori_loop(0, cap // COMBINE_GROUP, body, 0)


def moe_combine(idx, ye, x2d, n, n_sets, row_block_off, slot_block_off, out_rows, out_prev):
    d = x2d.shape[1]
    cap = idx.shape[-1]
    dt = 1024
    in_specs = [pl.BlockSpec((1, 1, cap), lambda b, c, e: (b * N_EXPERTS + e, 0, 0), memory_space=pltpu.SMEM),
                pl.BlockSpec((None, cap, dt), lambda b, c, e: (e, b + slot_block_off, c)),
                pl.BlockSpec((n, dt), lambda b, c, e: (b + row_block_off, c))]
    args = [idx, ye, x2d]
    aliases = {}
    if out_prev is not None:
        in_specs.append(pl.BlockSpec(memory_space=pl.ANY))
        args.append(out_prev)
        aliases = {3: 0}
    return pl.pallas_call(
        functools.partial(_combine_kernel, cap=cap),
        grid=(n_sets, d // dt, N_EXPERTS),
        in_specs=in_specs,
        out_specs=pl.BlockSpec((n, dt), lambda b, c, e: (b + row_block_off, c)),
        out_shape=jax.ShapeDtypeStruct((out_rows, d), F32),
        input_output_aliases=aliases,
        compiler_params=_cparams(("parallel", "parallel", "arbitrary")),
        name="moe_combine",
    )(*args)


def _split_rank(packed, n_sets):
    cap = packed.shape[-1]
    idx = packed[:, 0, :].astype(jnp.int32).reshape(n_sets * N_EXPERTS, 1, cap)
    gate = packed[:, 1, :].reshape(n_sets, N_EXPERTS, cap).transpose(1, 0, 2).reshape(N_EXPERTS, n_sets * cap)
    return idx, gate


def ec_moe(x2d, norm_g, mod3, w_router, w_gate, w_up, w_down, layer, geom, with_ctx):
    rows = geom.rows if with_ctx else geom.r_lat
    f2d, aff_row = moe_router(x2d, norm_g, mod3, w_router, geom, 3, 4, rows)
    sets = [(geom.l_lat, 0)]
    if with_ctx:
        sets.append((geom.l_ctx, geom.r_lat // geom.l_ctx))
    caps = [EC_CAPACITY * n // N_EXPERTS for n, _ in sets]
    m_total = sum(geom.batch * c for c in caps)
    routed, row_groups, xe, slot_off = [], [], None, 0
    for (n, row_off), cap in zip(sets, caps):
        idx, gate = _split_rank(moe_select(aff_row, n, geom.batch, row_off), geom.batch)
        xe = moe_gather(idx, f2d, n, geom.batch, row_off, slot_off // cap, m_total, xe)
        routed.append((idx, gate, n, row_off, slot_off // cap))
        for b in range(geom.batch):
            row_groups.append((slot_off + b * cap, cap, b if row_off == 0 else geom.batch))
        slot_off += geom.batch * cap
    gate_col = jnp.concatenate([r[1] for r in routed], axis=1).reshape(N_EXPERTS, m_total, 1)
    ye = moe_down(moe_glu(xe, w_gate, w_up, layer), w_down, gate_col, mod3, 5, tuple(row_groups), layer)
    out = None
    for idx, _, n, row_off, slot_blk in routed:
        out = moe_combine(idx, ye, x2d, n, geom.batch, row_off, slot_blk, rows, out)
    return out


def _final_norm_kernel(x_ref, g_ref, o_ref):
    x = x_ref[...]
    o_ref[...] = (x * lax.rsqrt(jnp.mean(x * x, axis=-1, keepdims=True) + EPS)) * g_ref[...]


def final_norm(x2d, g, rows, tl):
    d = x2d.shape[1]
    return pl.pallas_call(
        _final_norm_kernel,
        grid=(rows // tl,),
        in_specs=[pl.BlockSpec((tl, d), lambda i: (i, 0)), pl.BlockSpec((1, d), lambda i: (0, 0))],
        out_specs=pl.BlockSpec((tl, d), lambda i: (i, 0)),
        out_shape=jax.ShapeDtypeStruct((rows, d), F32),
        compiler_params=_cparams(("parallel",)),
        name="final_norm",
    )(x2d, g.reshape(1, d))


def _group_dt(dt, rows):
    g_n, e_n = SSD_GROUPS, SSD_HEADS_PER_GROUP
    dt_col = dt.reshape(rows, 2, g_n, e_n).transpose(2, 0, 1, 3).reshape(g_n, rows, 2 * e_n)
    dt_row = dt_col.reshape(g_n, rows // SSD_CHUNK, SSD_CHUNK, 2 * e_n).transpose(0, 1, 3, 2)
    return dt_col, dt_row


def ssd_layer(x2d, mod3, norm_mix_g, w_in, conv_w, conv_b, dt_bias, a_log, d_skip, norm_g, w_out, geom, tm):
    a = norm_mod(x2d, norm_mix_g, mod3, geom, 0, 1, geom.rows, BF16)
    n_main = SSD_D_INNER + SSD_XBC
    p = matmul(a, w_in, _pick_tile(geom.rows, IN_PROJ_ROWS), 512, 0, n_main)
    p_dt = matmul(a, w_in, tm, 2 * SSD_HEADS, n_main, 2 * SSD_HEADS)
    ctx_blk = geom.r_lat // geom.l_ctx
    xbc = ssd_conv(p, conv_w, conv_b, geom.l_ctx, geom.batch, ctx_blk, None)
    xbc = ssd_conv(p, conv_w, conv_b, geom.l_lat, geom.batch, 0, xbc)
    dt = ssd_dt(p_dt, dt_bias, tm)
    dt_col, dt_row = _group_dt(dt, geom.rows)
    zero = jnp.zeros((geom.batch * SSD_GROUPS, 2, SSD_D_STATE, SSD_GROUP_WIDTH), F32)
    y, s_ctx = ssd_scan(xbc, dt_col, dt_row, a_log, d_skip, zero, geom.l_ctx, geom.batch, ctx_blk, None)
    y, _ = ssd_scan(xbc, dt_col, dt_row, a_log, d_skip, s_ctx, geom.l_lat, geom.batch, 0, y)
    yb = ssd_out(y, p, norm_g, _pick_tile(geom.l_ctx, 256))
    return matmul_resid(yb, w_out, x2d, mod3, geom, 2, tm, 512)


def ret_layer(x2d, mod3, norm_mix_g, w_in, decay, gn_g, w_out, geom, tm):
    a = norm_mod(x2d, norm_mix_g, mod3, geom, 0, 1, geom.rows, BF16)
    p = matmul(a, w_in, _pick_tile(geom.rows, IN_PROJ_ROWS), 512)
    decay_b = jnp.broadcast_to(decay.T.reshape(RET_HEADS, 2, 1), (RET_HEADS, 2, V7X_LANES))
    cos_t, sin_t = rope_tables(geom.l_lat)
    s_ctx = ret_ctx_state(p, decay_b, geom)
    yb = ret_scan(p, cos_t, sin_t, decay_b, gn_g, s_ctx, geom)
    return matmul_resid(yb, w_out, x2d, mod3, geom, 2, tm, 512)


def kernel(x, c, ctx, c_ctx, ada_w, ada_b, norm_mix_g, norm_ffn_g, ssd_w_in, ssd_conv_w, ssd_conv_b,
           ssd_dt_bias, ssd_a_log, ssd_d, ssd_norm_g, ssd_w_out, ret_w_in, ret_decay, ret_gn_g, ret_w_out,
           moe_w_router, moe_w_gate, moe_w_up, moe_w_down, final_norm_g):
    batch, l_lat, d = x.shape
    l_ctx = ctx.shape[1]
    depth = ada_w.shape[0]
    n_mixers = 2
    assert batch + 1 <= MOD_ROWS and d == D_MODEL
    assert depth == n_mixers, "the retention layer must be the last one: its context branch is state-only"
    geom = Geom(batch, l_lat, l_ctx)
    tm = _pick_tile(geom.r_ctx, 1024)
    x2d = jnp.concatenate([x.reshape(geom.r_lat, d), ctx.reshape(geom.r_ctx, d)], axis=0)
    cc = jnp.zeros((MOD_ROWS, d), F32).at[:batch].set(c).at[batch].set(c_ctx)
    mod = ada_table(cc, ada_w, ada_b)
    for i in range(depth):
        need_ctx = i < depth - 1
        j = i // n_mixers
        mod3 = mod[i].reshape(MOD_ROWS, 1, N_MOD * d)
        if i % n_mixers == 0:
            x2d = ssd_layer(x2d, mod3, norm_mix_g[i], ssd_w_in[j], ssd_conv_w[j], ssd_conv_b[j], ssd_dt_bias[j],
                            ssd_a_log[j], ssd_d[j], ssd_norm_g[j], ssd_w_out[j], geom, tm)
        else:
            x2d = ret_layer(x2d, mod3, norm_mix_g[i], ret_w_in[j], ret_decay[j], ret_gn_g[j], ret_w_out[j],
                            geom, tm)
        x2d = ec_moe(x2d, norm_ffn_g[i], mod3, moe_w_router[i], moe_w_gate, moe_w_up, moe_w_down, i, geom,
                     need_ctx)
    out = final_norm(x2d, final_norm_g, geom.r_lat, _pick_tile(geom.l_ctx, 256))
    return out.reshape(batch, l_lat, d)
```

```python
import functools

import jax
import jax.numpy as jnp
from jax import lax
from jax.experimental import pallas as pl
from jax.experimental.pallas import tpu as pltpu

F32 = jnp.float32
BF16 = jnp.bfloat16
HIGHEST = lax.Precision.HIGHEST
LOG2_E = 1.4426950408889634

D_MODEL = 2048
N_MOD = 6
EPS = 1e-6

SSD_D_INNER = 2 * D_MODEL
SSD_HEAD_DIM = 64
SSD_HEADS = SSD_D_INNER // SSD_HEAD_DIM
SSD_GROUPS = 8
SSD_HEADS_PER_GROUP = SSD_HEADS // SSD_GROUPS
SSD_D_STATE = 128
SSD_CONV_W = 5
SSD_CHUNK = 128
SSD_BC = SSD_GROUPS * SSD_D_STATE
SSD_XBC = SSD_D_INNER + 2 * SSD_BC
SSD_IN = SSD_D_INNER + SSD_XBC + 2 * SSD_HEADS
SSD_GROUP_WIDTH = SSD_HEADS_PER_GROUP * SSD_HEAD_DIM

RET_HEADS = 8
RET_QK_DIM = D_MODEL // RET_HEADS
RET_V_DIM = 2 * RET_QK_DIM
RET_D_V = RET_HEADS * RET_V_DIM
RET_CHUNK = 128
RET_SCAN_CHUNK = 256
RET_IN = 2 * D_MODEL + 2 * RET_D_V
ROPE_BASE = 10000.0
GRID_W = 64

N_EXPERTS = 16
EC_CAPACITY = 2
D_EXPERT = D_MODEL

V7X_LANES = 128
V7X_SUBLANES = 8
V7X_VMEM_LIMIT_BYTES = 56 * 1024 * 1024
MOD_ROWS = 16
ROUTER_LANES = V7X_LANES
IN_PROJ_ROWS = 2048


def _cparams(sem, vmem=V7X_VMEM_LIMIT_BYTES):
    return pltpu.CompilerParams(dimension_semantics=sem, vmem_limit_bytes=vmem)


def _silu(x):
    return x * jax.nn.sigmoid(x)


def _pick_tile(n, target):
    t = min(n, target)
    while n % t:
        t //= 2
    return t


def _ada_kernel(c_ref, w_ref, b_ref, o_ref):
    a = _silu(c_ref[...]).astype(BF16)
    o_ref[...] = jnp.dot(a, w_ref[...].astype(BF16), preferred_element_type=F32) + b_ref[...]


def ada_table(cc, ada_w, ada_b):
    depth, d, n = ada_w.shape
    tn = 1024
    return pl.pallas_call(
        _ada_kernel,
        grid=(depth, n // tn),
        in_specs=[pl.BlockSpec((MOD_ROWS, d), lambda i, j: (0, 0)),
                  pl.BlockSpec((None, d, tn), lambda i, j: (i, 0, j)),
                  pl.BlockSpec((None, 1, tn), lambda i, j: (i, 0, j))],
        out_specs=pl.BlockSpec((None, MOD_ROWS, tn), lambda i, j: (i, 0, j)),
        out_shape=jax.ShapeDtypeStruct((depth, MOD_ROWS, n), F32),
        compiler_params=_cparams(("parallel", "parallel")),
        name="ada_table",
    )(cc, ada_w, ada_b.reshape(depth, 1, n))


class Geom:
    def __init__(self, batch, l_lat, l_ctx):
        self.batch, self.l_lat, self.l_ctx = batch, l_lat, l_ctx
        self.r_lat = batch * l_lat
        self.r_ctx = batch * l_ctx
        self.rows = self.r_lat + self.r_ctx

    def mod_row(self, i, tile):
        return jnp.where(i * tile < self.r_lat, (i * tile) // self.l_lat, self.batch)


def _norm_mod_kernel(x_ref, g_ref, sc_ref, sh_ref, o_ref):
    x = x_ref[...]
    y = x * lax.rsqrt(jnp.mean(x * x, axis=-1, keepdims=True) + EPS)
    o_ref[...] = ((y * g_ref[...]) * (1.0 + sc_ref[...]) + sh_ref[...]).astype(o_ref.dtype)


def norm_mod(x2d, g, mod3, geom, k_shift, k_scale, rows, out_dtype):
    d = x2d.shape[1]
    tl = _pick_tile(geom.l_ctx, 256)
    return pl.pallas_call(
        _norm_mod_kernel,
        grid=(rows // tl,),
        in_specs=[pl.BlockSpec((tl, d), lambda i: (i, 0)),
                  pl.BlockSpec((1, d), lambda i: (0, 0)),
                  pl.BlockSpec((None, 1, d), lambda i: (geom.mod_row(i, tl), 0, k_scale)),
                  pl.BlockSpec((None, 1, d), lambda i: (geom.mod_row(i, tl), 0, k_shift))],
        out_specs=pl.BlockSpec((tl, d), lambda i: (i, 0)),
        out_shape=jax.ShapeDtypeStruct((rows, d), out_dtype),
        compiler_params=_cparams(("parallel",)),
        name="norm_mod",
    )(x2d, g.reshape(1, d), mod3, mod3)


def _mm_kernel(a_ref, w_ref, o_ref):
    o_ref[...] = jnp.dot(a_ref[...], w_ref[...].astype(BF16), preferred_element_type=F32)


def matmul(a, w, tm, tn, col_start=0, n=None):
    m, k = a.shape
    n = w.shape[1] - col_start if n is None else n
    assert n % tn == 0 and col_start % tn == 0 and m % tm == 0
    j0 = col_start // tn
    return pl.pallas_call(
        _mm_kernel,
        grid=(m // tm, n // tn),
        in_specs=[pl.BlockSpec((tm, k), lambda i, j: (i, 0)),
                  pl.BlockSpec((k, tn), lambda i, j: (0, j + j0))],
        out_specs=pl.BlockSpec((tm, tn), lambda i, j: (i, j)),
        out_shape=jax.ShapeDtypeStruct((m, n), F32),
        compiler_params=_cparams(("parallel", "arbitrary")),
        name="matmul",
    )(a, w)


def _mm_resid_kernel(a_ref, w_ref, r_ref, g_ref, o_ref):
    acc = jnp.dot(a_ref[...], w_ref[...].astype(BF16), preferred_element_type=F32)
    o_ref[...] = r_ref[...] + g_ref[...] * acc


def matmul_resid(a, w, res, mod3, geom, k_gate, tm, tn):
    m, k = a.shape
    n = w.shape[1]
    nj = n // tn
    return pl.pallas_call(
        _mm_resid_kernel,
        grid=(m // tm, nj),
        in_specs=[pl.BlockSpec((tm, k), lambda i, j: (i, 0)),
                  pl.BlockSpec((k, tn), lambda i, j: (0, j)),
                  pl.BlockSpec((tm, tn), lambda i, j: (i, j)),
                  pl.BlockSpec((None, 1, tn), lambda i, j: (geom.mod_row(i, tm), 0, k_gate * nj + j))],
        out_specs=pl.BlockSpec((tm, tn), lambda i, j: (i, j)),
        out_shape=jax.ShapeDtypeStruct((m, n), F32),
        compiler_params=_cparams(("parallel", "arbitrary")),
        name="matmul_resid",
    )(a, w, res, mod3)


def _conv_kernel(u_ref, w_ref, b_ref, o_ref, *, n_lat_blocks, l_ctx):
    u = u_ref[...]
    l = u.shape[0]
    pad = SSD_CONV_W // 2
    t = lax.broadcasted_iota(jnp.int32, u.shape, 0)
    is_ctx = pl.program_id(0) >= n_lat_blocks
    pos = jnp.where(is_ctx, t & (l_ctx - 1), t)
    seq_len = jnp.where(is_ctx, l_ctx, l)
    acc = u * w_ref[pad:pad + 1, :] + b_ref[...]
    for k in range(SSD_CONV_W):
        off = k - pad
        if off == 0:
            continue
        shifted = pltpu.roll(u, (-off) % l, axis=0)
        valid = (pos + off >= 0) & (pos + off < seq_len)
        acc = acc + jnp.where(valid, shifted, 0.0) * w_ref[k:k + 1, :]
    o_ref[...] = _silu(acc)


def ssd_conv(p, conv_w, conv_b, geom):
    rows = p.shape[0]
    ct = 512
    col_off = SSD_D_INNER // ct
    assert geom.l_ctx & (geom.l_ctx - 1) == 0
    return pl.pallas_call(
        functools.partial(_conv_kernel, n_lat_blocks=geom.batch, l_ctx=geom.l_ctx),
        grid=(rows // geom.l_lat, SSD_XBC // ct),
        in_specs=[pl.BlockSpec((geom.l_lat, ct), lambda s, c: (s, c + col_off)),
                  pl.BlockSpec((SSD_CONV_W, ct), lambda s, c: (0, c)),
                  pl.BlockSpec((1, ct), lambda s, c: (0, c))],
        out_specs=pl.BlockSpec((geom.l_lat, ct), lambda s, c: (s, c)),
        out_shape=jax.ShapeDtypeStruct((rows, SSD_XBC), F32),
        compiler_params=_cparams(("parallel", "parallel")),
        name="ssd_conv",
    )(p, conv_w, conv_b.reshape(1, SSD_XBC))


def _dt_kernel(p_ref, b_ref, al_ref, cs_ref, cd_ref):
    q = SSD_CHUNK
    dt = jax.nn.softplus(p_ref[...] + b_ref[...])
    log2_dt = jnp.log2(dt)
    dta = dt * (-jnp.exp(al_ref[...]) * LOG2_E)
    row = lax.broadcasted_iota(jnp.int32, (q, q), 0)
    col = lax.broadcasted_iota(jnp.int32, (q, q), 1)
    prefix = (col <= row).astype(F32)
    suffix = (col >= row).astype(F32)
    fwd_cols = lax.broadcasted_iota(jnp.int32, (q, dt.shape[1]), 1) < SSD_HEADS
    for c in range(dt.shape[0] // q):
        blk = dta[c * q:(c + 1) * q, :]
        cs = jnp.where(fwd_cols,
                       jnp.dot(prefix, blk, precision=HIGHEST, preferred_element_type=F32),
                       jnp.dot(suffix, blk, precision=HIGHEST, preferred_element_type=F32))
        cs_ref[c * q:(c + 1) * q, :] = cs
        cd_ref[c * q:(c + 1) * q, :] = cs - log2_dt[c * q:(c + 1) * q, :]


def ssd_dt(p, dt_bias, a_log, tl):
    rows, w = p.shape
    assert tl % SSD_CHUNK == 0
    return pl.pallas_call(
        _dt_kernel,
        grid=(rows // tl,),
        in_specs=[pl.BlockSpec((tl, w), lambda i: (i, 0)),
                  pl.BlockSpec((1, w), lambda i: (0, 0)),
                  pl.BlockSpec((1, w), lambda i: (0, 0))],
        out_specs=[pl.BlockSpec((tl, w), lambda i: (i, 0)), pl.BlockSpec((tl, w), lambda i: (i, 0))],
        out_shape=[jax.ShapeDtypeStruct((rows, w), F32), jax.ShapeDtypeStruct((rows, w), F32)],
        compiler_params=_cparams(("parallel",)),
        name="ssd_dt",
    )(p, dt_bias.reshape(1, w), a_log.reshape(1, w))


def _ssd_scan_kernel(x_ref, b_ref, c_ref, csc_ref, csr_ref, cdr_ref, dsk_ref, s0_ref, *rest, nc):
    y_ref, sfin_ref, s_scr, yb_scr = rest[-4], rest[-3], rest[-2], rest[-1]
    q = SSD_CHUNK
    e_n = SSD_HEADS_PER_GROUP
    row = lax.broadcasted_iota(jnp.int32, (q, q), 0)
    col = lax.broadcasted_iota(jnp.int32, (q, q), 1)
    left = col < SSD_HEAD_DIM

    def chunk_step(d, c):
        if d == 0:
            incl, edge = col <= row, q - 1
        else:
            incl, edge = col >= row, 0
        t0 = pl.multiple_of(c * q, q)
        bm = b_ref[pl.ds(t0, q), :]
        cm = c_ref[pl.ds(t0, q), :]
        cs = csc_ref[0, pl.ds(t0, q), :][:, d * e_n:(d + 1) * e_n]
        tot_t = csr_ref[0, c][d * e_n:(d + 1) * e_n, :][:, edge:edge + 1]
        cd_t = cdr_ref[0, c][d * e_n:(d + 1) * e_n, :]
        w_t = jnp.exp2(tot_t - cd_t)
        et_b = jnp.broadcast_to(jnp.exp2(tot_t), (e_n, q))
        cmb = cm.astype(BF16)
        cb = lax.dot_general(cmb, bm.astype(BF16), (((1,), (1,)), ((), ())),
                             preferred_element_type=F32)
        bm_t = bm.T
        y_inter = jnp.dot(cmb, s_scr[d].astype(BF16), preferred_element_type=F32)
        for pr in range(e_n // 2):
            sl = slice(pr * 2 * SSD_HEAD_DIM, (pr + 1) * 2 * SSD_HEAD_DIM)
            mix, bw, ecs = [], [], []
            for e in (2 * pr, 2 * pr + 1):
                cs_b = jnp.broadcast_to(cs[:, e:e + 1], (q, q))
                seg = jnp.exp2(jnp.where(incl, cs_b - cd_t[e:e + 1, :], -jnp.inf))
                mix.append((cb * seg).astype(BF16))
                bw.append((bm_t * w_t[e:e + 1, :]).astype(BF16))
                ecs.append(jnp.exp2(cs_b))
            xs = x_ref[pl.ds(t0, q), sl]
            ss = s_scr[d, :, sl]
            lhs1 = jnp.concatenate([jnp.concatenate(mix, axis=1), jnp.concatenate(bw, axis=1)], axis=0)
            rhs1 = jnp.concatenate([jnp.where(left, xs, 0.0), jnp.where(left, 0.0, xs)], axis=0).astype(BF16)
            r1 = jnp.dot(lhs1, rhs1, preferred_element_type=F32)
            y = r1[:q] + y_inter[:, sl] * jnp.where(left, ecs[0], ecs[1])
            dec = jnp.where(left[:1], et_b[2 * pr:2 * pr + 1, :], et_b[2 * pr + 1:2 * pr + 2, :])
            s_scr[d, :, sl] = ss * dec + r1[q:]
            if d == 0:
                y_ref[pl.ds(t0, q), sl] = y + dsk_ref[:, sl] * xs
            else:
                yb_scr[pl.ds(t0, q), sl] = y

    s_scr[...] = s0_ref[0]

    def body(ci, carry):
        chunk_step(0, ci)
        chunk_step(1, nc - 1 - ci)
        return carry

    lax.fori_loop(0, nc, body, 0)
    sfin_ref[0] = s_scr[...]
    y_ref[...] = y_ref[...] + yb_scr[...]


def ssd_scan(xbc, cs_col, cs_row, cd_row, d_skip, s0, seq_len, n_seq, row_block_off):
    g_n, e_n, gw = SSD_GROUPS, SSD_HEADS_PER_GROUP, SSD_GROUP_WIDTH
    nc = seq_len // SSD_CHUNK
    b_off = SSD_D_INNER // SSD_D_STATE
    c_off = (SSD_D_INNER + SSD_BC) // SSD_D_STATE
    dsk = jnp.repeat(d_skip, SSD_HEAD_DIM).reshape(1, SSD_D_INNER)
    row_spec = pl.BlockSpec((1, nc, 2 * e_n, SSD_CHUNK), lambda s, g: (g, s + row_block_off, 0, 0))
    in_specs = [pl.BlockSpec((seq_len, gw), lambda s, g: (s + row_block_off, g)),
                pl.BlockSpec((seq_len, SSD_D_STATE), lambda s, g: (s + row_block_off, b_off + g)),
                pl.BlockSpec((seq_len, SSD_D_STATE), lambda s, g: (s + row_block_off, c_off + g)),
                pl.BlockSpec((1, seq_len, 2 * e_n), lambda s, g: (g, s + row_block_off, 0)),
                row_spec,
                row_spec,
                pl.BlockSpec((1, gw), lambda s, g: (0, g)),
                pl.BlockSpec((1, 2, SSD_D_STATE, gw), lambda s, g: (s * g_n + g, 0, 0, 0))]
    return pl.pallas_call(
        functools.partial(_ssd_scan_kernel, nc=nc),
        grid=(n_seq, g_n),
        in_specs=in_specs,
        out_specs=[pl.BlockSpec((seq_len, gw), lambda s, g: (s, g)),
                   pl.BlockSpec((1, 2, SSD_D_STATE, gw), lambda s, g: (s * g_n + g, 0, 0, 0))],
        out_shape=[jax.ShapeDtypeStruct((n_seq * seq_len, SSD_D_INNER), F32),
                   jax.ShapeDtypeStruct((n_seq * g_n, 2, SSD_D_STATE, gw), F32)],
        scratch_shapes=[pltpu.VMEM((2, SSD_D_STATE, gw), F32), pltpu.VMEM((seq_len, gw), F32)],
        compiler_params=_cparams(("parallel", "parallel")),
        name="ssd_scan",
    )(xbc, xbc, xbc, cs_col, cs_row, cd_row, dsk, s0)


def _ssd_out_kernel(yl_ref, yc_ref, z_ref, g_ref, o_ref, *, n_lat_tiles):
    is_ctx = pl.program_id(0) >= n_lat_tiles
    y = jnp.where(is_ctx, yc_ref[...], yl_ref[...])
    v = y * _silu(z_ref[...])
    n = v * lax.rsqrt(jnp.mean(v * v, axis=-1, keepdims=True) + EPS)
    o_ref[...] = (n * g_ref[...]).astype(o_ref.dtype)


def ssd_out(y_lat, y_ctx, p, norm_g, tl):
    n_lat, n_ctx = y_lat.shape[0] // tl, y_ctx.shape[0] // tl
    rows = y_lat.shape[0] + y_ctx.shape[0]
    return pl.pallas_call(
        functools.partial(_ssd_out_kernel, n_lat_tiles=n_lat),
        grid=(n_lat + n_ctx,),
        in_specs=[pl.BlockSpec((tl, SSD_D_INNER), lambda i: (jnp.minimum(i, n_lat - 1), 0)),
                  pl.BlockSpec((tl, SSD_D_INNER), lambda i: (jnp.maximum(i - n_lat, 0), 0)),
                  pl.BlockSpec((tl, SSD_D_INNER), lambda i: (i, 0)),
                  pl.BlockSpec((1, SSD_D_INNER), lambda i: (0, 0))],
        out_specs=pl.BlockSpec((tl, SSD_D_INNER), lambda i: (i, 0)),
        out_shape=jax.ShapeDtypeStruct((rows, SSD_D_INNER), BF16),
        compiler_params=_cparams(("parallel",)),
        name="ssd_out",
    )(y_lat, y_ctx, p, norm_g.reshape(1, SSD_D_INNER))


def _ret_state_kernel(k_ref, v_ref, ld_ref, s_ref):
    l = k_ref.shape[0]
    k = k_ref[...] * (RET_QK_DIM ** -0.5)
    vb = v_ref[...].astype(BF16)
    pos = lax.broadcasted_iota(jnp.int32, (l, V7X_LANES), 0).astype(F32)
    for d in range(2):
        ld = -jnp.exp(ld_ref[0, d:d + 1, :])
        steps = (l - 1.0 - pos) if d == 0 else pos
        w = jnp.exp(steps * ld)
        kd = (k * jnp.concatenate([w] * (RET_QK_DIM // V7X_LANES), axis=1)).astype(BF16)
        s_ref[0, d] = lax.dot_general(kd, vb, (((0,), (0,)), ((), ())), preferred_element_type=F32)


def ret_ctx_state(p, decay_b, geom):
    l = geom.l_ctx
    row_off = geom.r_lat // l
    k_off = D_MODEL // RET_QK_DIM
    v_off = 2 * D_MODEL // RET_V_DIM
    return pl.pallas_call(
        _ret_state_kernel,
        grid=(geom.batch, RET_HEADS),
        in_specs=[pl.BlockSpec((l, RET_QK_DIM), lambda b, h: (b + row_off, k_off + h)),
                  pl.BlockSpec((l, RET_V_DIM), lambda b, h: (b + row_off, v_off + h)),
                  pl.BlockSpec((1, 2, V7X_LANES), lambda b, h: (h, 0, 0))],
        out_specs=pl.BlockSpec((1, 2, RET_QK_DIM, RET_V_DIM), lambda b, h: (b * RET_HEADS + h, 0, 0, 0)),
        out_shape=jax.ShapeDtypeStruct((geom.batch * RET_HEADS, 2, RET_QK_DIM, RET_V_DIM), F32),
        compiler_params=_cparams(("parallel", "parallel")),
        name="ret_ctx_state",
    )(p, p, decay_b)


def _rope(u, cos, sin):
    parts = []
    for j in range(u.shape[1] // V7X_LANES):
        s = u[:, j * V7X_LANES:(j + 1) * V7X_LANES]
        parts.append(pltpu.roll(s, V7X_LANES // 2, axis=1))
    return u * cos + jnp.concatenate(parts, axis=1) * sin


def _ret_scan_kernel(q_ref, k_ref, v_ref, g_ref, cos_ref, sin_ref, ld_ref, gn_ref, s0_ref, o_ref,
                     s_scr, qb_scr, k_scr, acc_scr, *, nc):
    q = RET_SCAN_CHUNK
    nslab_k = RET_QK_DIM // V7X_LANES
    nslab_v = RET_V_DIM // V7X_LANES
    rel = (lax.broadcasted_iota(jnp.int32, (q, q), 0) - lax.broadcasted_iota(jnp.int32, (q, q), 1)).astype(F32)
    l_i = lax.broadcasted_iota(jnp.int32, (q, V7X_LANES), 0).astype(F32)
    ld_f = -jnp.exp(ld_ref[0, 0:1, :])
    ld_b = -jnp.exp(ld_ref[0, 1:2, :])
    decay_in = (jnp.exp(jnp.where(rel >= 0, rel * ld_f[:, :1], -jnp.inf))
                + jnp.exp(jnp.where(rel <= 0, -rel * ld_b[:, :1], -jnp.inf)))
    from_state = (jnp.exp((l_i + 1.0) * ld_f), jnp.exp((q - l_i) * ld_b))
    to_end = (jnp.exp((q - 1.0 - l_i) * ld_f), jnp.exp(l_i * ld_b))
    chunk_decay = (jnp.exp(q * ld_f), jnp.exp(q * ld_b))

    def update_state(d, k, vb):
        kd = (k * jnp.concatenate([to_end[d]] * nslab_k, axis=1)).astype(BF16)
        s_scr[...] = (s_scr[...] * jnp.concatenate([chunk_decay[d]] * nslab_v, axis=1)
                      + lax.dot_general(kd, vb, (((0,), (0,)), ((), ())), preferred_element_type=F32))

    def inter(d, qb):
        return (jnp.dot(qb, s_scr[...].astype(BF16), preferred_element_type=F32)
                * jnp.concatenate([from_state[d]] * nslab_v, axis=1))

    s_scr[...] = s0_ref[0, 0]

    def fwd(c, carry):
        t0 = pl.multiple_of(c * q, q)
        cos = cos_ref[pl.ds(t0, q), :]
        sin = sin_ref[pl.ds(t0, q), :]
        qb = _rope(q_ref[pl.ds(t0, q), :], cos, sin).astype(BF16)
        k = _rope(k_ref[pl.ds(t0, q), :], cos, sin) * (RET_QK_DIM ** -0.5)
        vb = v_ref[pl.ds(t0, q), :].astype(BF16)
        qb_scr[pl.ds(t0, q), :] = qb
        k_scr[pl.ds(t0, q), :] = k
        scores = lax.dot_general(qb, k.astype(BF16), (((1,), (1,)), ((), ())),
                                 preferred_element_type=F32) * decay_in
        acc_scr[pl.ds(t0, q), :] = jnp.dot(scores.astype(BF16), vb, preferred_element_type=F32) + inter(0, qb)
        update_state(0, k, vb)
        return carry

    lax.fori_loop(0, nc, fwd, 0)
    s_scr[...] = s0_ref[0, 1]

    def bwd(ci, carry):
        c = nc - 1 - ci
        t0 = pl.multiple_of(c * q, q)
        qb = qb_scr[pl.ds(t0, q), :]
        k = k_scr[pl.ds(t0, q), :]
        vb = v_ref[pl.ds(t0, q), :].astype(BF16)
        o = acc_scr[pl.ds(t0, q), :] + inter(1, qb)
        update_state(1, k, vb)
        mu = jnp.mean(o, axis=-1, keepdims=True)
        var = jnp.mean(jnp.square(o - mu), axis=-1, keepdims=True)
        on = ((o - mu) * lax.rsqrt(var + EPS)) * gn_ref[...]
        o_ref[pl.ds(t0, q), :] = (on * _silu(g_ref[pl.ds(t0, q), :])).astype(o_ref.dtype)
        return carry

    lax.fori_loop(0, nc, bwd, 0)


def ret_scan(p, cos_t, sin_t, decay_b, gn_g, s0, geom):
    l = geom.l_lat
    assert l % RET_SCAN_CHUNK == 0
    nc = l // RET_SCAN_CHUNK
    k_off = D_MODEL // RET_QK_DIM
    v_off = 2 * D_MODEL // RET_V_DIM
    g_off = (2 * D_MODEL + RET_D_V) // RET_V_DIM
    return pl.pallas_call(
        functools.partial(_ret_scan_kernel, nc=nc),
        grid=(geom.batch, RET_HEADS),
        in_specs=[pl.BlockSpec((l, RET_QK_DIM), lambda b, h: (b, h)),
                  pl.BlockSpec((l, RET_QK_DIM), lambda b, h: (b, k_off + h)),
                  pl.BlockSpec((l, RET_V_DIM), lambda b, h: (b, v_off + h)),
                  pl.BlockSpec((l, RET_V_DIM), lambda b, h: (b, g_off + h)),
                  pl.BlockSpec((l, RET_QK_DIM), lambda b, h: (0, 0)),
                  pl.BlockSpec((l, RET_QK_DIM), lambda b, h: (0, 0)),
                  pl.BlockSpec((1, 2, V7X_LANES), lambda b, h: (h, 0, 0)),
                  pl.BlockSpec((1, RET_V_DIM), lambda b, h: (0, h)),
                  pl.BlockSpec((1, 2, RET_QK_DIM, RET_V_DIM), lambda b, h: (b * RET_HEADS + h, 0, 0, 0))],
        out_specs=pl.BlockSpec((l, RET_V_DIM), lambda b, h: (b, h)),
        out_shape=jax.ShapeDtypeStruct((geom.r_lat, RET_D_V), BF16),
        scratch_shapes=[pltpu.VMEM((RET_QK_DIM, RET_V_DIM), F32),
                        pltpu.VMEM((l, RET_QK_DIM), BF16),
                        pltpu.VMEM((l, RET_QK_DIM), F32),
                        pltpu.VMEM((l, RET_V_DIM), F32)],
        compiler_params=_cparams(("parallel", "parallel")),
        name="ret_scan",
    )(p, p, p, p, cos_t, sin_t, decay_b, gn_g.reshape(1, RET_D_V), s0)


def rope_tables(l_lat):
    half = RET_QK_DIM // 4
    pos = jnp.arange(l_lat)
    freqs = ROPE_BASE ** (-jnp.arange(half, dtype=F32) / half)
    cs, sn = [], []
    for ids in (pos // GRID_W, pos % GRID_W):
        ang = ids.astype(F32)[:, None] * freqs[None, :]
        cs += [jnp.cos(ang), jnp.cos(ang)]
        sn += [-jnp.sin(ang), jnp.sin(ang)]
    return jnp.concatenate(cs, axis=1), jnp.concatenate(sn, axis=1)


def _router_kernel(x_ref, g_ref, sc_ref, sh_ref, wr_ref, f_ref, ar_ref):
    x = x_ref[...]
    y = x * lax.rsqrt(jnp.mean(x * x, axis=-1, keepdims=True) + EPS)
    f = (y * g_ref[...]) * (1.0 + sc_ref[...]) + sh_ref[...]
    f_ref[...] = f
    logits = jnp.dot(f, wr_ref[...], precision=HIGHEST, preferred_element_type=F32)
    lane = lax.broadcasted_iota(jnp.int32, logits.shape, 1)
    logits = jnp.where(lane < N_EXPERTS, logits, -jnp.inf)
    un = jnp.exp(logits - jnp.max(logits, axis=-1, keepdims=True))
    aff = un / jnp.sum(un, axis=-1, keepdims=True)
    ar_ref[...] = aff.T[:N_EXPERTS, :]


def moe_router(x2d, g, mod3, w_router, geom, k_shift, k_scale, rows):
    d = x2d.shape[1]
    tl = _pick_tile(geom.l_ctx, 256)
    wr = jnp.pad(w_router, ((0, 0), (0, ROUTER_LANES - N_EXPERTS)))
    return pl.pallas_call(
        _router_kernel,
        grid=(rows // tl,),
        in_specs=[pl.BlockSpec((tl, d), lambda i: (i, 0)),
                  pl.BlockSpec((1, d), lambda i: (0, 0)),
                  pl.BlockSpec((None, 1, d), lambda i: (geom.mod_row(i, tl), 0, k_scale)),
                  pl.BlockSpec((None, 1, d), lambda i: (geom.mod_row(i, tl), 0, k_shift)),
                  pl.BlockSpec((d, ROUTER_LANES), lambda i: (0, 0))],
        out_specs=[pl.BlockSpec((tl, d), lambda i: (i, 0)),
                   pl.BlockSpec((N_EXPERTS, tl), lambda i: (0, i))],
        out_shape=[jax.ShapeDtypeStruct((rows, d), F32),
                   jax.ShapeDtypeStruct((N_EXPERTS, rows), F32)],
        compiler_params=_cparams(("parallel",)),
        name="moe_router",
    )(x2d, g.reshape(1, d), mod3, mod3, wr)


def _select_kernel(ar_ref, o_ref, slot_scr, *, n, cap):
    e_n, lanes, sub = N_EXPERTS, V7X_LANES, V7X_SUBLANES
    a = ar_ref[...]
    bits = pltpu.bitcast(a, jnp.int32)

    def bisect(i, thr):
        cand = thr | lax.shift_left(jnp.int32(1), 30 - i)
        cnt = jnp.sum(jnp.where(bits >= cand, 1.0, 0.0), axis=1, keepdims=True)
        return jnp.where(cnt >= cap, cand, thr)

    thr = lax.fori_loop(0, 31, bisect, jnp.zeros((e_n, 1), jnp.int32))
    gt = jnp.where(bits > thr, 1.0, 0.0)
    eq = jnp.where(bits == thr, 1.0, 0.0)
    need = cap - jnp.sum(gt, axis=1, keepdims=True)
    upper = (lax.broadcasted_iota(jnp.int32, (lanes, lanes), 0)
             <= lax.broadcasted_iota(jnp.int32, (lanes, lanes), 1)).astype(BF16)
    ties_before = jnp.zeros((e_n, 1), F32)
    taken_before = jnp.zeros((e_n, 1), F32)
    for j in range(n // lanes):
        sl = slice(j * lanes, (j + 1) * lanes)
        eq_j = eq[:, sl]
        tie_incl = jnp.dot(eq_j.astype(BF16), upper, preferred_element_type=F32)
        tie_pos = ties_before + tie_incl - eq_j
        ties_before = ties_before + tie_incl[:, lanes - 1:lanes]
        sel_j = gt[:, sl] + eq_j * jnp.where(tie_pos < need, 1.0, 0.0)
        sel_incl = jnp.dot(sel_j.astype(BF16), upper, preferred_element_type=F32)
        slot_scr[:, sl] = jnp.where(sel_j > 0.0, taken_before + sel_incl - sel_j, -1.0)
        taken_before = taken_before + sel_incl[:, lanes - 1:lanes]

    r_col = lax.broadcasted_iota(jnp.int32, (cap, n), 0).astype(F32)
    t_row = lax.broadcasted_iota(jnp.int32, (cap, n), 1).astype(F32)
    lane_o = lax.broadcasted_iota(jnp.int32, (cap, lanes), 1)
    cap_pad = o_ref.shape[-1]

    def invert(e, carry):
        onehot = slot_scr[pl.ds(e, 1), :] == r_col
        idx = jnp.sum(jnp.where(onehot, t_row, 0.0), axis=1, keepdims=True)
        gate = jnp.sum(jnp.where(onehot, ar_ref[pl.ds(e, 1), :], 0.0), axis=1, keepdims=True)
        packed = jnp.where(lane_o == 0, idx, jnp.where(lane_o == 1, gate, 0.0))
        if cap_pad > cap:
            packed = jnp.concatenate([packed, jnp.zeros((cap_pad - cap, lanes), F32)], axis=0)
        o_ref[e] = packed.T[:sub, :]
        return carry

    lax.fori_loop(0, e_n, invert, 0)


def moe_select(aff_row, n, n_sets, row_block_off):
    cap = EC_CAPACITY * n // N_EXPERTS
    cap_pad = max(cap, V7X_LANES)
    packed = pl.pallas_call(
        functools.partial(_select_kernel, n=n, cap=cap),
        grid=(n_sets,),
        in_specs=[pl.BlockSpec((N_EXPERTS, n), lambda b: (0, b + row_block_off))],
        out_specs=pl.BlockSpec((N_EXPERTS, V7X_SUBLANES, cap_pad), lambda b: (b, 0, 0)),
        out_shape=jax.ShapeDtypeStruct((n_sets * N_EXPERTS, V7X_SUBLANES, cap_pad), F32),
        scratch_shapes=[pltpu.VMEM((N_EXPERTS, n), F32)],
        compiler_params=_cparams(("parallel",)),
        name="moe_select",
    )(aff_row)
    return packed[:, :, :cap]


def _gather_kernel(idx_ref, f_ref, o_ref, buf, *, cap):
    def body(r, carry):
        buf[pl.ds(r, 1), :] = f_ref[pl.ds(idx_ref[0, 0, r], 1), :]
        return carry

    lax.fori_loop(0, cap, body, 0, unroll=8)
    o_ref[...] = buf[...].astype(o_ref.dtype)


def moe_gather(idx, f2d, n, n_blocks):
    d = f2d.shape[1]
    cap = idx.shape[-1]
    dt = 1024
    return pl.pallas_call(
        functools.partial(_gather_kernel, cap=cap),
        grid=(n_blocks, d // dt, N_EXPERTS),
        in_specs=[pl.BlockSpec((1, 1, cap), lambda b, c, e: (b * N_EXPERTS + e, 0, 0), memory_space=pltpu.SMEM),
                  pl.BlockSpec((n, dt), lambda b, c, e: (b, c))],
        out_specs=pl.BlockSpec((None, cap, dt), lambda b, c, e: (e, b, c)),
        out_shape=jax.ShapeDtypeStruct((N_EXPERTS, n_blocks * cap, d), BF16),
        scratch_shapes=[pltpu.VMEM((cap, dt), F32)],
        compiler_params=_cparams(("parallel", "parallel", "arbitrary")),
        name="moe_gather",
    )(idx, f2d)


def _glu_kernel(x_ref, wg_ref, wu_ref, o_ref):
    x = x_ref[...]
    hg = jnp.dot(x, wg_ref[...].astype(BF16), preferred_element_type=F32)
    hu = jnp.dot(x, wu_ref[...].astype(BF16), preferred_element_type=F32)
    o_ref[...] = (_silu(hg) * hu).astype(o_ref.dtype)


def moe_glu(xe, w_gate, w_up, layer):
    e_n, m, d = xe.shape
    f = w_gate.shape[-1]
    tn = 256
    return pl.pallas_call(
        _glu_kernel,
        grid=(e_n, f // tn),
        in_specs=[pl.BlockSpec((None, m, d), lambda e, j: (e, 0, 0)),
                  pl.BlockSpec((None, None, d, tn), lambda e, j: (layer, e, 0, j)),
                  pl.BlockSpec((None, None, d, tn), lambda e, j: (layer, e, 0, j))],
        out_specs=pl.BlockSpec((None, m, tn), lambda e, j: (e, 0, j)),
        out_shape=jax.ShapeDtypeStruct((e_n, m, f), BF16),
        compiler_params=_cparams(("parallel", "arbitrary")),
        name="moe_glu",
    )(xe, w_gate, w_up)


def _down_kernel(h_ref, wd_ref, gate_ref, g_ref, o_ref, *, row_groups):
    acc = jnp.dot(h_ref[...], wd_ref[...].astype(BF16), preferred_element_type=F32)
    reps = acc.shape[1] // V7X_LANES
    for start, size, mrow in row_groups:
        gate = jnp.concatenate([gate_ref[start:start + size, :]] * reps, axis=1)
        o_ref[start:start + size, :] = (acc[start:start + size, :] * gate) * g_ref[mrow]


def moe_down(hid, w_down, gate_col, mod3, k_gate, row_groups, layer):
    e_n, m, f = hid.shape
    d = w_down.shape[-1]
    tn = 256
    nj = d // tn
    return pl.pallas_call(
        functools.partial(_down_kernel, row_groups=row_groups),
        grid=(e_n, nj),
        in_specs=[pl.BlockSpec((None, m, f), lambda e, j: (e, 0, 0)),
                  pl.BlockSpec((None, None, f, tn), lambda e, j: (layer, e, 0, j)),
                  pl.BlockSpec((None, m, V7X_LANES), lambda e, j: (e, 0, 0)),
                  pl.BlockSpec((MOD_ROWS, 1, tn), lambda e, j: (0, 0, k_gate * nj + j))],
        out_specs=pl.BlockSpec((None, m, tn), lambda e, j: (e, 0, j)),
        out_shape=jax.ShapeDtypeStruct((e_n, m, d), F32),
        compiler_params=_cparams(("parallel", "arbitrary")),
        name="moe_down",
    )(hid, w_down, gate_col, mod3)


COMBINE_GROUP = 8


def _combine_kernel(idx_ref, ye_ref, h_ref, *rest, cap):
    o_ref = rest[-1]
    e = pl.program_id(2)

    @pl.when(e == 0)
    def _():
        o_ref[...] = h_ref[...]

    def body(g, carry):
        base = g * COMBINE_GROUP
        toks = [idx_ref[0, 0, base + u] for u in range(COMBINE_GROUP)]
        rows = [o_ref[pl.ds(t, 1), :] + ye_ref[pl.ds(base + u, 1), :] for u, t in enumerate(toks)]
        for t, row in zip(toks, rows):
            o_ref[pl.ds(t, 1), :] = row
        return carry

    lax.fori_loop(0, cap // COMBINE_GROUP, body, 0)


def moe_combine(idx, ye, x2d, n, n_blocks):
    d = x2d.shape[1]
    cap = idx.shape[-1]
    dt = 1024
    return pl.pallas_call(
        functools.partial(_combine_kernel, cap=cap),
        grid=(n_blocks, d // dt, N_EXPERTS),
        in_specs=[pl.BlockSpec((1, 1, cap), lambda b, c, e: (b * N_EXPERTS + e, 0, 0), memory_space=pltpu.SMEM),
                  pl.BlockSpec((None, cap, dt), lambda b, c, e: (e, b, c)),
                  pl.BlockSpec((n, dt), lambda b, c, e: (b, c))],
        out_specs=pl.BlockSpec((n, dt), lambda b, c, e: (b, c)),
        out_shape=jax.ShapeDtypeStruct((n_blocks * n, d), F32),
        compiler_params=_cparams(("parallel", "parallel", "arbitrary")),
        name="moe_combine",
    )(idx, ye, x2d)


def _split_select(packed, n_sets):
    cap = packed.shape[-1]
    idx = packed[:, 0, :].astype(jnp.int32).reshape(n_sets, N_EXPERTS, cap)
    gate = packed[:, 1, :].reshape(n_sets, N_EXPERTS, cap)
    return idx, gate


def ec_moe(x2d, norm_g, mod3, w_router, w_gate, w_up, w_down, layer, geom, with_ctx):
    b_n, e_n = geom.batch, N_EXPERTS
    rows = geom.rows if with_ctx else geom.r_lat
    f2d, aff_row = moe_router(x2d, norm_g, mod3, w_router, geom, 3, 4, rows)
    cap = EC_CAPACITY * geom.l_lat // e_n
    idx, gate = _split_select(moe_select(aff_row, geom.l_lat, b_n, 0), b_n)
    row_groups = [(b * cap, cap, b) for b in range(b_n)]
    if with_ctx:
        cap_c = EC_CAPACITY * geom.l_ctx // e_n
        assert b_n * cap_c == cap and geom.r_ctx == geom.l_lat
        idx_c, gate_c = _split_select(moe_select(aff_row, geom.l_ctx, b_n, geom.r_lat // geom.l_ctx), b_n)
        idx_c = idx_c + (jnp.arange(b_n, dtype=jnp.int32) * geom.l_ctx)[:, None, None]
        idx = jnp.concatenate([idx, idx_c.transpose(1, 0, 2).reshape(1, e_n, cap)], axis=0)
        gate = jnp.concatenate([gate, gate_c.transpose(1, 0, 2).reshape(1, e_n, cap)], axis=0)
        row_groups += [(b_n * cap + b * cap_c, cap_c, b_n) for b in range(b_n)]
    n_blocks = idx.shape[0]
    idx = idx.reshape(n_blocks * e_n, 1, cap)
    gate_col = jnp.broadcast_to(gate.transpose(1, 0, 2).reshape(e_n, n_blocks * cap, 1),
                                (e_n, n_blocks * cap, V7X_LANES))
    xe = moe_gather(idx, f2d, geom.l_lat, n_blocks)
    ye = moe_down(moe_glu(xe, w_gate, w_up, layer), w_down, gate_col, mod3, 5, tuple(row_groups), layer)
    return moe_combine(idx, ye, x2d, geom.l_lat, n_blocks)


def _final_norm_kernel(x_ref, g_ref, o_ref):
    x = x_ref[...]
    o_ref[...] = (x * lax.rsqrt(jnp.mean(x * x, axis=-1, keepdims=True) + EPS)) * g_ref[...]


def final_norm(x2d, g, rows, tl):
    d = x2d.shape[1]
    return pl.pallas_call(
        _final_norm_kernel,
        grid=(rows // tl,),
        in_specs=[pl.BlockSpec((tl, d), lambda i: (i, 0)), pl.BlockSpec((1, d), lambda i: (0, 0))],
        out_specs=pl.BlockSpec((tl, d), lambda i: (i, 0)),
        out_shape=jax.ShapeDtypeStruct((rows, d), F32),
        compiler_params=_cparams(("parallel",)),
        name="final_norm",
    )(x2d, g.reshape(1, d))


def _group_dt(dt, rows):
    g_n, e_n = SSD_GROUPS, SSD_HEADS_PER_GROUP
    dt_col = dt.reshape(rows, 2, g_n, e_n).transpose(2, 0, 1, 3).reshape(g_n, rows, 2 * e_n)
    dt_row = dt_col.reshape(g_n, rows // SSD_CHUNK, SSD_CHUNK, 2 * e_n).transpose(0, 1, 3, 2)
    return dt_col, dt_row


def ssd_layer(x2d, mod3, norm_mix_g, w_in, conv_w, conv_b, dt_bias, a_log, d_skip, norm_g, w_out, geom, tm):
    a = norm_mod(x2d, norm_mix_g, mod3, geom, 0, 1, geom.rows, BF16)
    n_main = SSD_D_INNER + SSD_XBC
    p = matmul(a, w_in, _pick_tile(geom.rows, IN_PROJ_ROWS), 512, 0, n_main)
    p_dt = matmul(a, w_in, tm, 2 * SSD_HEADS, n_main, 2 * SSD_HEADS)
    ctx_blk = geom.r_lat // geom.l_ctx
    xbc = ssd_conv(p, conv_w, conv_b, geom)
    cs, cd = ssd_dt(p_dt, dt_bias, a_log, tm)
    cs_col, cs_row = _group_dt(cs, geom.rows)
    _, cd_row = _group_dt(cd, geom.rows)
    zero = jnp.zeros((geom.batch * SSD_GROUPS, 2, SSD_D_STATE, SSD_GROUP_WIDTH), F32)
    y_ctx, s_ctx = ssd_scan(xbc, cs_col, cs_row, cd_row, d_skip, zero, geom.l_ctx, geom.batch, ctx_blk)
    y_lat, _ = ssd_scan(xbc, cs_col, cs_row, cd_row, d_skip, s_ctx, geom.l_lat, geom.batch, 0)
    yb = ssd_out(y_lat, y_ctx, p, norm_g, _pick_tile(geom.l_ctx, 256))
    return matmul_resid(yb, w_out, x2d, mod3, geom, 2, tm, 512)


def ret_layer(x2d, mod3, norm_mix_g, w_in, decay, gn_g, w_out, geom, tm):
    a = norm_mod(x2d, norm_mix_g, mod3, geom, 0, 1, geom.rows, BF16)
    p = matmul(a, w_in, _pick_tile(geom.rows, IN_PROJ_ROWS), 512)
    decay_b = jnp.broadcast_to(decay.T.reshape(RET_HEADS, 2, 1), (RET_HEADS, 2, V7X_LANES))
    cos_t, sin_t = rope_tables(geom.l_lat)
    s_ctx = ret_ctx_state(p, decay_b, geom)
    yb = ret_scan(p, cos_t, sin_t, decay_b, gn_g, s_ctx, geom)
    return matmul_resid(yb, w_out, x2d, mod3, geom, 2, tm, 512)


def kernel(x, c, ctx, c_ctx, ada_w, ada_b, norm_mix_g, norm_ffn_g, ssd_w_in, ssd_conv_w, ssd_conv_b,
           ssd_dt_bias, ssd_a_log, ssd_d, ssd_norm_g, ssd_w_out, ret_w_in, ret_decay, ret_gn_g, ret_w_out,
           moe_w_router, moe_w_gate, moe_w_up, moe_w_down, final_norm_g):
    batch, l_lat, d = x.shape
    l_ctx = ctx.shape[1]
    depth = ada_w.shape[0]
    n_mixers = 2
    assert batch + 1 <= MOD_ROWS and d == D_MODEL
    assert depth == n_mixers, "the retention layer must be the last one: its context branch is state-only"
    geom = Geom(batch, l_lat, l_ctx)
    assert geom.r_ctx == l_lat, "all context sequences together must fill exactly one latent-sized row block"
    tm = _pick_tile(geom.r_ctx, 1024)
    x2d = jnp.concatenate([x.reshape(geom.r_lat, d), ctx.reshape(geom.r_ctx, d)], axis=0)
    cc = jnp.zeros((MOD_ROWS, d), F32).at[:batch].set(c).at[batch].set(c_ctx)
    mod = ada_table(cc, ada_w, ada_b)
    for i in range(depth):
        need_ctx = i < depth - 1
        j = i // n_mixers
        mod3 = mod[i].reshape(MOD_ROWS, 1, N_MOD * d)
        if i % n_mixers == 0:
            x2d = ssd_layer(x2d, mod3, norm_mix_g[i], ssd_w_in[j], ssd_conv_w[j], ssd_conv_b[j], ssd_dt_bias[j],
                            ssd_a_log[j], ssd_d[j], ssd_norm_g[j], ssd_w_out[j], geom, tm)
        else:
            x2d = ret_layer(x2d, mod3, norm_mix_g[i], ret_w_in[j], ret_decay[j], ret_gn_g[j], ret_w_out[j],
                            geom, tm)
        x2d = ec_moe(x2d, norm_ffn_g[i], mod3, moe_w_router[i], moe_w_gate, moe_w_up, moe_w_down, i, geom,
                     need_ctx)
    out = final_norm(x2d, final_norm_g, geom.r_lat, _pick_tile(geom.l_ctx, 256))
    return out.reshape(batch, l_lat, d)
```

```python
import functools

import jax
import jax.numpy as jnp
from jax import lax
from jax.experimental import pallas as pl
from jax.experimental.pallas import tpu as pltpu

F32 = jnp.float32
BF16 = jnp.bfloat16
HIGHEST = lax.Precision.HIGHEST
LOG2_E = 1.4426950408889634

D_MODEL = 2048
N_MOD = 6
EPS = 1e-6

SSD_D_INNER = 2 * D_MODEL
SSD_HEAD_DIM = 64
SSD_HEADS = SSD_D_INNER // SSD_HEAD_DIM
SSD_GROUPS = 8
SSD_HEADS_PER_GROUP = SSD_HEADS // SSD_GROUPS
SSD_D_STATE = 128
SSD_CONV_W = 5
SSD_CHUNK = 128
SSD_BC = SSD_GROUPS * SSD_D_STATE
SSD_XBC = SSD_D_INNER + 2 * SSD_BC
SSD_IN = SSD_D_INNER + SSD_XBC + 2 * SSD_HEADS
SSD_GROUP_WIDTH = SSD_HEADS_PER_GROUP * SSD_HEAD_DIM

RET_HEADS = 8
RET_QK_DIM = D_MODEL // RET_HEADS
RET_V_DIM = 2 * RET_QK_DIM
RET_D_V = RET_HEADS * RET_V_DIM
RET_CHUNK = 128
RET_SCAN_CHUNK = 256
RET_IN = 2 * D_MODEL + 2 * RET_D_V
ROPE_BASE = 10000.0
GRID_W = 64

N_EXPERTS = 16
EC_CAPACITY = 2
D_EXPERT = D_MODEL

V7X_LANES = 128
V7X_SUBLANES = 8
V7X_VMEM_LIMIT_BYTES = 56 * 1024 * 1024
MOD_ROWS = 16
ROUTER_LANES = V7X_LANES
IN_PROJ_ROWS = 2048
OUT_PROJ_ROWS = 1024
OUT_PROJ_COLS = 512


def _cparams(sem, vmem=V7X_VMEM_LIMIT_BYTES):
    return pltpu.CompilerParams(dimension_semantics=sem, vmem_limit_bytes=vmem)


def _silu(x):
    return x * jax.nn.sigmoid(x)


def _pick_tile(n, target):
    t = min(n, target)
    while n % t:
        t //= 2
    return t


def _ada_kernel(c_ref, w_ref, b_ref, o_ref):
    a = _silu(c_ref[...]).astype(BF16)
    o_ref[...] = jnp.dot(a, w_ref[...].astype(BF16), preferred_element_type=F32) + b_ref[...]


def ada_table(cc, ada_w, ada_b):
    depth, d, n = ada_w.shape
    tn = 1024
    return pl.pallas_call(
        _ada_kernel,
        grid=(depth, n // tn),
        in_specs=[pl.BlockSpec((MOD_ROWS, d), lambda i, j: (0, 0)),
                  pl.BlockSpec((None, d, tn), lambda i, j: (i, 0, j)),
                  pl.BlockSpec((None, 1, tn), lambda i, j: (i, 0, j))],
        out_specs=pl.BlockSpec((None, MOD_ROWS, tn), lambda i, j: (i, 0, j)),
        out_shape=jax.ShapeDtypeStruct((depth, MOD_ROWS, n), F32),
        compiler_params=_cparams(("parallel", "parallel")),
        name="ada_table",
    )(cc, ada_w, ada_b.reshape(depth, 1, n))


class Geom:
    def __init__(self, batch, l_lat, l_ctx):
        self.batch, self.l_lat, self.l_ctx = batch, l_lat, l_ctx
        self.r_lat = batch * l_lat
        self.r_ctx = batch * l_ctx
        self.rows = self.r_lat + self.r_ctx

    def mod_row(self, i, tile):
        return jnp.where(i * tile < self.r_lat, (i * tile) // self.l_lat, self.batch)


def _norm_mod_kernel(x_ref, g_ref, sc_ref, sh_ref, o_ref):
    x = x_ref[...]
    y = x * lax.rsqrt(jnp.mean(x * x, axis=-1, keepdims=True) + EPS)
    o_ref[...] = ((y * g_ref[...]) * (1.0 + sc_ref[...]) + sh_ref[...]).astype(o_ref.dtype)


def norm_mod(x2d, g, mod3, geom, k_shift, k_scale, rows, out_dtype):
    d = x2d.shape[1]
    tl = _pick_tile(geom.l_ctx, 256)
    return pl.pallas_call(
        _norm_mod_kernel,
        grid=(rows // tl,),
        in_specs=[pl.BlockSpec((tl, d), lambda i: (i, 0)),
                  pl.BlockSpec((1, d), lambda i: (0, 0)),
                  pl.BlockSpec((None, 1, d), lambda i: (geom.mod_row(i, tl), 0, k_scale)),
                  pl.BlockSpec((None, 1, d), lambda i: (geom.mod_row(i, tl), 0, k_shift))],
        out_specs=pl.BlockSpec((tl, d), lambda i: (i, 0)),
        out_shape=jax.ShapeDtypeStruct((rows, d), out_dtype),
        compiler_params=_cparams(("parallel",)),
        name="norm_mod",
    )(x2d, g.reshape(1, d), mod3, mod3)


def _mm_kernel(a_ref, w_ref, o_ref):
    o_ref[...] = jnp.dot(a_ref[...], w_ref[...].astype(BF16), preferred_element_type=F32)


def matmul(a, w, tm, tn, col_start=0, n=None, row_start=0, m=None):
    k = a.shape[1]
    m = a.shape[0] - row_start if m is None else m
    n = w.shape[1] - col_start if n is None else n
    assert n % tn == 0 and col_start % tn == 0 and m % tm == 0 and row_start % tm == 0
    i0, j0 = row_start // tm, col_start // tn
    return pl.pallas_call(
        _mm_kernel,
        grid=(m // tm, n // tn),
        in_specs=[pl.BlockSpec((tm, k), lambda i, j: (i + i0, 0)),
                  pl.BlockSpec((k, tn), lambda i, j: (0, j + j0))],
        out_specs=pl.BlockSpec((tm, tn), lambda i, j: (i, j)),
        out_shape=jax.ShapeDtypeStruct((m, n), F32),
        compiler_params=_cparams(("parallel", "arbitrary")),
        name="matmul",
    )(a, w)


def _mm_resid_kernel(a_ref, w_ref, r_ref, g_ref, o_ref):
    acc = jnp.dot(a_ref[...], w_ref[...].astype(BF16), preferred_element_type=F32)
    o_ref[...] = r_ref[...] + g_ref[...] * acc


def matmul_resid(a, w, res, mod3, geom, k_gate, tm, tn):
    m, k = a.shape
    n = w.shape[1]
    nj = n // tn
    return pl.pallas_call(
        _mm_resid_kernel,
        grid=(m // tm, nj),
        in_specs=[pl.BlockSpec((tm, k), lambda i, j: (i, 0)),
                  pl.BlockSpec((k, tn), lambda i, j: (0, j)),
                  pl.BlockSpec((tm, tn), lambda i, j: (i, j)),
                  pl.BlockSpec((None, 1, tn), lambda i, j: (geom.mod_row(i, tm), 0, k_gate * nj + j))],
        out_specs=pl.BlockSpec((tm, tn), lambda i, j: (i, j)),
        out_shape=jax.ShapeDtypeStruct((m, n), F32),
        compiler_params=_cparams(("parallel", "arbitrary")),
        name="matmul_resid",
    )(a, w, res, mod3)


def _conv_kernel(u_ref, w_ref, b_ref, o_ref, *, n_lat_blocks, l_ctx):
    u = u_ref[...]
    l, ct = u.shape
    pad = SSD_CONV_W // 2
    t = lax.broadcasted_iota(jnp.int32, (l, V7X_LANES), 0)
    is_ctx = pl.program_id(0) >= n_lat_blocks
    pos = jnp.where(is_ctx, t & (l_ctx - 1), t)
    seq_len = jnp.where(is_ctx, l_ctx, l)
    acc = u * w_ref[pad:pad + 1, :] + b_ref[...]
    for k in range(SSD_CONV_W):
        off = k - pad
        if off == 0:
            continue
        shifted = pltpu.roll(u, (-off) % l, axis=0)
        inside =jnp.where((pos + off >= 0) & (pos + off < seq_len), 1.0, 0.0)
        acc = acc + (shifted * jnp.concatenate([inside] * (ct // V7X_LANES), axis=1)) * w_ref[k:k + 1, :]
    o_ref[...] = _silu(acc)


def ssd_conv(p, conv_w, conv_b, geom):
    rows = p.shape[0]
    ct = 512
    col_off = SSD_D_INNER // ct
    assert geom.l_ctx & (geom.l_ctx - 1) == 0
    return pl.pallas_call(
        functools.partial(_conv_kernel, n_lat_blocks=geom.batch, l_ctx=geom.l_ctx),
        grid=(rows // geom.l_lat, SSD_XBC // ct),
        in_specs=[pl.BlockSpec((geom.l_lat, ct), lambda s, c: (s, c + col_off)),
                  pl.BlockSpec((SSD_CONV_W, ct), lambda s, c: (0, c)),
                  pl.BlockSpec((1, ct), lambda s, c: (0, c))],
        out_specs=pl.BlockSpec((geom.l_lat, ct), lambda s, c: (s, c)),
        out_shape=jax.ShapeDtypeStruct((rows, SSD_XBC), F32),
        compiler_params=_cparams(("parallel", "parallel")),
        name="ssd_conv",
    )(p, conv_w, conv_b.reshape(1, SSD_XBC))


def _dt_kernel(p_ref, b_ref, al_ref, cs_ref, cd_ref):
    q = SSD_CHUNK
    dt = jax.nn.softplus(p_ref[...] + b_ref[...])
    log2_dt = jnp.log2(dt)
    dta = dt * (-jnp.exp(al_ref[...]) * LOG2_E)
    row = lax.broadcasted_iota(jnp.int32, (q, q), 0)
    col = lax.broadcasted_iota(jnp.int32, (q, q), 1)
    prefix = (col <= row).astype(F32)
    suffix = (col >= row).astype(F32)
    fwd_cols = lax.broadcasted_iota(jnp.int32, (q, dt.shape[1]), 1) < SSD_HEADS
    for c in range(dt.shape[0] // q):
        blk = dta[c * q:(c + 1) * q, :]
        cs = jnp.where(fwd_cols,
                       jnp.dot(prefix, blk, precision=HIGHEST, preferred_element_type=F32),
                       jnp.dot(suffix, blk, precision=HIGHEST, preferred_element_type=F32))
        cs_ref[c * q:(c + 1) * q, :] = cs
        cd_ref[c * q:(c + 1) * q, :] = cs - log2_dt[c * q:(c + 1) * q, :]


def ssd_dt(p, dt_bias, a_log, tl):
    rows, w = p.shape
    assert tl % SSD_CHUNK == 0
    return pl.pallas_call(
        _dt_kernel,
        grid=(rows // tl,),
        in_specs=[pl.BlockSpec((tl, w), lambda i: (i, 0)),
                  pl.BlockSpec((1, w), lambda i: (0, 0)),
                  pl.BlockSpec((1, w), lambda i: (0, 0))],
        out_specs=[pl.BlockSpec((tl, w), lambda i: (i, 0)), pl.BlockSpec((tl, w), lambda i: (i, 0))],
        out_shape=[jax.ShapeDtypeStruct((rows, w), F32), jax.ShapeDtypeStruct((rows, w), F32)],
        compiler_params=_cparams(("parallel",)),
        name="ssd_dt",
    )(p, dt_bias.reshape(1, w), a_log.reshape(1, w))


def _ssd_scan_kernel(x_ref, b_ref, c_ref, csc_ref, csr_ref, cdr_ref, dsk_ref, s0_ref, *rest, nc):
    y_ref, sfin_ref, s_scr, yb_scr = rest[-4], rest[-3], rest[-2], rest[-1]
    q = SSD_CHUNK
    e_n = SSD_HEADS_PER_GROUP
    row = lax.broadcasted_iota(jnp.int32, (q, q), 0)
    col = lax.broadcasted_iota(jnp.int32, (q, q), 1)
    left = col < SSD_HEAD_DIM

    def chunk_step(d, c):
        if d == 0:
            incl, edge = col <= row, q - 1
        else:
            incl, edge = col >= row, 0
        t0 = pl.multiple_of(c * q, q)
        bm = b_ref[pl.ds(t0, q), :]
        cm = c_ref[pl.ds(t0, q), :]
        cs = csc_ref[0, pl.ds(t0, q), :][:, d * e_n:(d + 1) * e_n]
        tot_t = csr_ref[0, c][d * e_n:(d + 1) * e_n, :][:, edge:edge + 1]
        cd_t = cdr_ref[0, c][d * e_n:(d + 1) * e_n, :]
        w_t = jnp.exp2(tot_t - cd_t)
        et_b = jnp.broadcast_to(jnp.exp2(tot_t), (e_n, q))
        cmb = cm.astype(BF16)
        cb = lax.dot_general(cmb, bm.astype(BF16), (((1,), (1,)), ((), ())),
                             preferred_element_type=F32)
        bm_t = bm.T
        y_inter = jnp.dot(cmb, s_scr[d].astype(BF16), preferred_element_type=F32)
        for pr in range(e_n // 2):
            sl = slice(pr * 2 * SSD_HEAD_DIM, (pr + 1) * 2 * SSD_HEAD_DIM)
            mix, bw, ecs = [], [], []
            for e in (2 * pr, 2 * pr + 1):
                cs_b = jnp.broadcast_to(cs[:, e:e + 1], (q, q))
                seg = jnp.exp2(jnp.where(incl, cs_b - cd_t[e:e + 1, :], -jnp.inf))
                mix.append((cb * seg).astype(BF16))
                bw.append((bm_t * w_t[e:e + 1, :]).astype(BF16))
                ecs.append(jnp.exp2(cs_b))
            xs = x_ref[pl.ds(t0, q), sl]
            ss = s_scr[d, :, sl]
            lhs1 = jnp.concatenate([jnp.concatenate(mix, axis=1), jnp.concatenate(bw, axis=1)], axis=0)
            rhs1 = jnp.concatenate([jnp.where(left, xs, 0.0), jnp.where(left, 0.0, xs)], axis=0).astype(BF16)
            r1 = jnp.dot(lhs1, rhs1, preferred_element_type=F32)
            y = r1[:q] + y_inter[:, sl] * jnp.where(left, ecs[0], ecs[1])
            dec = jnp.where(left[:1], et_b[2 * pr:2 * pr + 1, :], et_b[2 * pr + 1:2 * pr + 2, :])
            s_scr[d, :, sl] = ss * dec + r1[q:]
            if d == 0:
                y_ref[pl.ds(t0, q), sl] = y + dsk_ref[:, sl] * xs
            else:
                yb_scr[pl.ds(t0, q), sl] = y

    s_scr[...] = s0_ref[0]

    def body(ci, carry):
        chunk_step(0, ci)
        chunk_step(1, nc - 1 - ci)
        return carry

    lax.fori_loop(0, nc, body, 0)
    sfin_ref[0] = s_scr[...]
    y_ref[...] = y_ref[...] + yb_scr[...]


def ssd_scan(xbc, cs_col, cs_row, cd_row, d_skip, s0, seq_len, n_seq, row_block_off):
    g_n, e_n, gw = SSD_GROUPS, SSD_HEADS_PER_GROUP, SSD_GROUP_WIDTH
    nc = seq_len // SSD_CHUNK
    b_off = SSD_D_INNER // SSD_D_STATE
    c_off = (SSD_D_INNER + SSD_BC) // SSD_D_STATE
    dsk = jnp.repeat(d_skip, SSD_HEAD_DIM).reshape(1, SSD_D_INNER)
    row_spec = pl.BlockSpec((1, nc, 2 * e_n, SSD_CHUNK), lambda s, g: (g, s + row_block_off, 0, 0))
    in_specs = [pl.BlockSpec((seq_len, gw), lambda s, g: (s + row_block_off, g)),
                pl.BlockSpec((seq_len, SSD_D_STATE), lambda s, g: (s + row_block_off, b_off + g)),
                pl.BlockSpec((seq_len, SSD_D_STATE), lambda s, g: (s + row_block_off, c_off + g)),
                pl.BlockSpec((1, seq_len, 2 * e_n), lambda s, g: (g, s + row_block_off, 0)),
                row_spec,
                row_spec,
                pl.BlockSpec((1, gw), lambda s, g: (0, g)),
                pl.BlockSpec((1, 2, SSD_D_STATE, gw), lambda s, g: (s * g_n + g, 0, 0, 0))]
    return pl.pallas_call(
        functools.partial(_ssd_scan_kernel, nc=nc),
        grid=(n_seq, g_n),
        in_specs=in_specs,
        out_specs=[pl.BlockSpec((seq_len, gw), lambda s, g: (s, g)),
                   pl.BlockSpec((1, 2, SSD_D_STATE, gw), lambda s, g: (s * g_n + g, 0, 0, 0))],
        out_shape=[jax.ShapeDtypeStruct((n_seq * seq_len, SSD_D_INNER), F32),
                   jax.ShapeDtypeStruct((n_seq * g_n, 2, SSD_D_STATE, gw), F32)],
        scratch_shapes=[pltpu.VMEM((2, SSD_D_STATE, gw), F32), pltpu.VMEM((seq_len, gw), F32)],
        compiler_params=_cparams(("parallel", "parallel")),
        name="ssd_scan",
    )(xbc, xbc, xbc, cs_col, cs_row, cd_row, dsk, s0)


def _ssd_out_kernel(yl_ref, yc_ref, z_ref, g_ref, o_ref, *, n_lat_tiles):
    is_ctx = pl.program_id(0) >= n_lat_tiles
    y = jnp.where(is_ctx, yc_ref[...], yl_ref[...])
    v = y * _silu(z_ref[...])
    n = v * lax.rsqrt(jnp.mean(v * v, axis=-1, keepdims=True) + EPS)
    o_ref[...] = (n * g_ref[...]).astype(o_ref.dtype)


def ssd_out(y_lat, y_ctx, p, norm_g, tl):
    n_lat, n_ctx = y_lat.shape[0] // tl, y_ctx.shape[0] // tl
    rows = y_lat.shape[0] + y_ctx.shape[0]
    return pl.pallas_call(
        functools.partial(_ssd_out_kernel, n_lat_tiles=n_lat),
        grid=(n_lat + n_ctx,),
        in_specs=[pl.BlockSpec((tl, SSD_D_INNER), lambda i: (jnp.minimum(i, n_lat - 1), 0)),
                  pl.BlockSpec((tl, SSD_D_INNER), lambda i: (jnp.maximum(i - n_lat, 0), 0)),
                  pl.BlockSpec((tl, SSD_D_INNER), lambda i: (i, 0)),
                  pl.BlockSpec((1, SSD_D_INNER), lambda i: (0, 0))],
        out_specs=pl.BlockSpec((tl, SSD_D_INNER), lambda i: (i, 0)),
        out_shape=jax.ShapeDtypeStruct((rows, SSD_D_INNER), BF16),
        compiler_params=_cparams(("parallel",)),
        name="ssd_out",
    )(y_lat, y_ctx, p, norm_g.reshape(1, SSD_D_INNER))


def _ret_state_kernel(k_ref, v_ref, ld_ref, s_ref):
    l = k_ref.shape[0]
    k = k_ref[...] * (RET_QK_DIM ** -0.5)
    vb = v_ref[...].astype(BF16)
    pos = lax.broadcasted_iota(jnp.int32, (l, V7X_LANES), 0).astype(F32)
    for d in range(2):
        ld = -jnp.exp(ld_ref[0, d:d + 1, :])
        steps = (l - 1.0 - pos) if d == 0 else pos
        w = jnp.exp(steps * ld)
        kd = (k * jnp.concatenate([w] * (RET_QK_DIM // V7X_LANES), axis=1)).astype(BF16)
        s_ref[0, d] = lax.dot_general(kd, vb, (((0,), (0,)), ((), ())), preferred_element_type=F32)


def ret_ctx_state(p, decay_b, geom):
    l = geom.l_ctx
    row_off = 0
    k_off = 0
    v_off = D_MODEL // RET_V_DIM
    return pl.pallas_call(
        _ret_state_kernel,
        grid=(geom.batch, RET_HEADS),
        in_specs=[pl.BlockSpec((l, RET_QK_DIM), lambda b, h: (b + row_off, k_off + h)),
                  pl.BlockSpec((l, RET_V_DIM), lambda b, h: (b + row_off, v_off + h)),
                  pl.BlockSpec((1, 2, V7X_LANES), lambda b, h: (h, 0, 0))],
        out_specs=pl.BlockSpec((1, 2, RET_QK_DIM, RET_V_DIM), lambda b, h: (b * RET_HEADS + h, 0, 0, 0)),
        out_shape=jax.ShapeDtypeStruct((geom.batch * RET_HEADS, 2, RET_QK_DIM, RET_V_DIM), F32),
        compiler_params=_cparams(("parallel", "parallel")),
        name="ret_ctx_state",
    )(p, p, decay_b)


def _rope(u, cos, sin):
    parts = []
    for j in range(u.shape[1] // V7X_LANES):
        s = u[:, j * V7X_LANES:(j + 1) * V7X_LANES]
        parts.append(pltpu.roll(s, V7X_LANES // 2, axis=1))
    return u * cos + jnp.concatenate(parts, axis=1) * sin


def _ret_scan_kernel(q_ref, k_ref, v_ref, g_ref, cos_ref, sin_ref, ld_ref, gn_ref, s0_ref, o_ref,
                     s_scr, qb_scr, sb_scr, acc_scr, *, nc):
    q = RET_SCAN_CHUNK
    nslab_k = RET_QK_DIM // V7X_LANES
    nslab_v = RET_V_DIM // V7X_LANES
    rel = (lax.broadcasted_iota(jnp.int32, (q, q), 0) - lax.broadcasted_iota(jnp.int32, (q, q), 1)).astype(F32)
    l_i = lax.broadcasted_iota(jnp.int32, (q, V7X_LANES), 0).astype(F32)
    ld_f = -jnp.exp(ld_ref[0, 0:1, :])
    ld_b = -jnp.exp(ld_ref[0, 1:2, :])
    decay_in = (jnp.exp(jnp.where(rel >= 0, rel * ld_f[:, :1], -jnp.inf))
                + jnp.exp(jnp.where(rel <= 0, -rel * ld_b[:, :1], -jnp.inf)))
    from_state = (jnp.exp((l_i + 1.0) * ld_f), jnp.exp((q - l_i) * ld_b))
    to_end = (jnp.exp((q - 1.0 - l_i) * ld_f), jnp.exp(l_i * ld_b))
    chunk_decay = (jnp.exp(q * ld_f), jnp.exp(q * ld_b))

    def roped_key(t0):
        return _rope(k_ref[pl.ds(t0, q), :], cos_ref[pl.ds(t0, q), :], sin_ref[pl.ds(t0, q), :]) * (RET_QK_DIM ** -0.5)

    def update_state(d, k, vb):
        kd = (k * jnp.concatenate([to_end[d]] * nslab_k, axis=1)).astype(BF16)
        s_scr[d] = (s_scr[d] * jnp.concatenate([chunk_decay[d]] * nslab_v, axis=1)
                    + lax.dot_general(kd, vb, (((0,), (0,)), ((), ())), preferred_element_type=F32))

    s_scr[...] = s0_ref[0]

    def scan(ci, carry):
        t0 = pl.multiple_of(ci * q, q)
        qb = _rope(q_ref[pl.ds(t0, q), :], cos_ref[pl.ds(t0, q), :], sin_ref[pl.ds(t0, q), :]).astype(BF16)
        k = roped_key(t0)
        vb = v_ref[pl.ds(t0, q), :].astype(BF16)
        qb_scr[pl.ds(t0, q), :] = qb
        scores = lax.dot_general(qb, k.astype(BF16), (((1,), (1,)), ((), ())),
                                 preferred_element_type=F32) * decay_in
        inter_f = (jnp.dot(qb, s_scr[0].astype(BF16), preferred_element_type=F32)
                   * jnp.concatenate([from_state[0]] * nslab_v, axis=1))
        acc_scr[pl.ds(t0, q), :] = jnp.dot(scores.astype(BF16), vb, preferred_element_type=F32) + inter_f
        update_state(0, k, vb)
        cb = nc - 1 - ci
        tb = pl.multiple_of(cb * q, q)
        sb_scr[cb] = s_scr[1].astype(BF16)
        update_state(1, roped_key(tb), v_ref[pl.ds(tb, q), :].astype(BF16))
        return carry

    lax.fori_loop(0, nc, scan, 0)

    def finish(c, carry):
        t0 = pl.multiple_of(c * q, q)
        o = acc_scr[pl.ds(t0, q), :] + (jnp.dot(qb_scr[pl.ds(t0, q), :], sb_scr[c], preferred_element_type=F32)
                                        * jnp.concatenate([from_state[1]] * nslab_v, axis=1))
        mu = jnp.mean(o, axis=-1, keepdims=True)
        var = jnp.mean(jnp.square(o - mu), axis=-1, keepdims=True)
        on = ((o - mu) * lax.rsqrt(var + EPS)) * gn_ref[...]
        o_ref[pl.ds(t0, q), :] = (on * _silu(g_ref[pl.ds(t0, q), :])).astype(o_ref.dtype)
        return carry

    lax.fori_loop(0, nc, finish, 0, unroll=2 if nc % 2 == 0 else 1)


def ret_scan(p, cos_t, sin_t, decay_b, gn_g, s0, geom):
    l = geom.l_lat
    assert l % RET_SCAN_CHUNK == 0
    nc = l // RET_SCAN_CHUNK
    k_off = D_MODEL // RET_QK_DIM
    v_off = 2 * D_MODEL // RET_V_DIM
    g_off = (2 * D_MODEL + RET_D_V) // RET_V_DIM
    return pl.pallas_call(
        functools.partial(_ret_scan_kernel, nc=nc),
        grid=(geom.batch, RET_HEADS),
        in_specs=[pl.BlockSpec((l, RET_QK_DIM), lambda b, h: (b, h)),
                  pl.BlockSpec((l, RET_QK_DIM), lambda b, h: (b, k_off + h)),
                  pl.BlockSpec((l, RET_V_DIM), lambda b, h: (b, v_off + h)),
                  pl.BlockSpec((l, RET_V_DIM), lambda b, h: (b, g_off + h)),
                  pl.BlockSpec((l, RET_QK_DIM), lambda b, h: (0, 0)),
                  pl.BlockSpec((l, RET_QK_DIM), lambda b, h: (0, 0)),
                  pl.BlockSpec((1, 2, V7X_LANES), lambda b, h: (h, 0, 0)),
                  pl.BlockSpec((1, RET_V_DIM), lambda b, h: (0, h)),
                  pl.BlockSpec((1, 2, RET_QK_DIM, RET_V_DIM), lambda b, h: (b * RET_HEADS + h, 0, 0, 0))],
        out_specs=pl.BlockSpec((l, RET_V_DIM), lambda b, h: (b, h)),
        out_shape=jax.ShapeDtypeStruct((geom.r_lat, RET_D_V), BF16),
        scratch_shapes=[pltpu.VMEM((2, RET_QK_DIM, RET_V_DIM), F32),
                        pltpu.VMEM((l, RET_QK_DIM), BF16),
                        pltpu.VMEM((nc, RET_QK_DIM, RET_V_DIM), BF16),
                        pltpu.VMEM((l, RET_V_DIM), F32)],
        compiler_params=_cparams(("parallel", "parallel")),
        name="ret_scan",
    )(p, p, p, p, cos_t, sin_t, decay_b, gn_g.reshape(1, RET_D_V), s0)


def rope_tables(l_lat):
    half = RET_QK_DIM // 4
    pos = jnp.arange(l_lat)
    freqs = ROPE_BASE ** (-jnp.arange(half, dtype=F32) / half)
    cs, sn = [], []
    for ids in (pos // GRID_W, pos % GRID_W):
        ang = ids.astype(F32)[:, None] * freqs[None, :]
        cs += [jnp.cos(ang), jnp.cos(ang)]
        sn += [-jnp.sin(ang), jnp.sin(ang)]
    return jnp.concatenate(cs, axis=1), jnp.concatenate(sn, axis=1)


def _router_kernel(x_ref, g_ref, sc_ref, sh_ref, wr_ref, f_ref, ar_ref):
    x = x_ref[...]
    y = x * lax.rsqrt(jnp.mean(x * x, axis=-1, keepdims=True) + EPS)
    f = (y * g_ref[...]) * (1.0 + sc_ref[...]) + sh_ref[...]
    f_ref[...] = f
    logits = jnp.dot(f, wr_ref[...], precision=HIGHEST, preferred_element_type=F32)
    lane = lax.broadcasted_iota(jnp.int32, logits.shape, 1)
    logits = jnp.where(lane < N_EXPERTS, logits, -jnp.inf)
    un = jnp.exp(logits - jnp.max(logits, axis=-1, keepdims=True))
    aff = un / jnp.sum(un, axis=-1, keepdims=True)
    ar_ref[...] = aff.T[:N_EXPERTS, :]


def moe_router(x2d, g, mod3, w_router, geom, k_shift, k_scale, rows):
    d = x2d.shape[1]
    tl = _pick_tile(geom.l_ctx, 256)
    wr = jnp.pad(w_router, ((0, 0), (0, ROUTER_LANES - N_EXPERTS)))
    return pl.pallas_call(
        _router_kernel,
        grid=(rows // tl,),
        in_specs=[pl.BlockSpec((tl, d), lambda i: (i, 0)),
                  pl.BlockSpec((1, d), lambda i: (0, 0)),
                  pl.BlockSpec((None, 1, d), lambda i: (geom.mod_row(i, tl), 0, k_scale)),
                  pl.BlockSpec((None, 1, d), lambda i: (geom.mod_row(i, tl), 0, k_shift)),
                  pl.BlockSpec((d, ROUTER_LANES), lambda i: (0, 0))],
        out_specs=[pl.BlockSpec((tl, d), lambda i: (i, 0)),
                   pl.BlockSpec((N_EXPERTS, tl), lambda i: (0, i))],
        out_shape=[jax.ShapeDtypeStruct((rows, d), F32),
                   jax.ShapeDtypeStruct((N_EXPERTS, rows), F32)],
        compiler_params=_cparams(("parallel",)),
        name="moe_router",
    )(x2d, g.reshape(1, d), mod3, mod3, wr)


def _select_kernel(ar_ref, o_ref, slot_scr, *, n, cap):
    e_n, lanes, sub = N_EXPERTS, V7X_LANES, V7X_SUBLANES
    a = ar_ref[...]
    bits = pltpu.bitcast(a, jnp.int32)

    def bisect(i, thr):
        cand = thr | lax.shift_left(jnp.int32(1), 30 - i)
        cnt = jnp.sum(jnp.where(bits >= cand, 1.0, 0.0), axis=1, keepdims=True)
        return jnp.where(cnt >= cap, cand, thr)

    thr = lax.fori_loop(0, 31, bisect, jnp.zeros((e_n, 1), jnp.int32))
    gt = jnp.where(bits > thr, 1.0, 0.0)
    eq = jnp.where(bits == thr, 1.0, 0.0)
    need = cap - jnp.sum(gt, axis=1, keepdims=True)
    upper = (lax.broadcasted_iota(jnp.int32, (lanes, lanes), 0)
             <= lax.broadcasted_iota(jnp.int32, (lanes, lanes), 1)).astype(BF16)
    ties_before = jnp.zeros((e_n, 1), F32)
    taken_before = jnp.zeros((e_n, 1), F32)
    for j in range(n // lanes):
        sl = slice(j * lanes, (j + 1) * lanes)
        eq_j = eq[:, sl]
        tie_incl = jnp.dot(eq_j.astype(BF16), upper, preferred_element_type=F32)
        tie_pos = ties_before + tie_incl - eq_j
        ties_before = ties_before + tie_incl[:, lanes - 1:lanes]
        sel_j = gt[:, sl] + eq_j * jnp.where(tie_pos < need, 1.0, 0.0)
        sel_incl = jnp.dot(sel_j.astype(BF16), upper, preferred_element_type=F32)
        slot_scr[:, sl] = jnp.where(sel_j > 0.0, taken_before + sel_incl - sel_j, -1.0)
        taken_before = taken_before + sel_incl[:, lanes - 1:lanes]

    r_col = lax.broadcasted_iota(jnp.int32, (cap, n), 0).astype(F32)
    t_row = lax.broadcasted_iota(jnp.int32, (cap, n), 1).astype(F32)
    lane_o = lax.broadcasted_iota(jnp.int32, (cap, lanes), 1)
    cap_pad = o_ref.shape[-1]

    def invert(e, carry):
        onehot = slot_scr[pl.ds(e, 1), :] == r_col
        idx = jnp.sum(jnp.where(onehot, t_row, 0.0), axis=1, keepdims=True)
        gate = jnp.sum(jnp.where(onehot, ar_ref[pl.ds(e, 1), :], 0.0), axis=1, keepdims=True)
        packed = jnp.where(lane_o == 0, idx, jnp.where(lane_o == 1, gate, 0.0))
        if cap_pad > cap:
            packed = jnp.concatenate([packed, jnp.zeros((cap_pad - cap, lanes), F32)], axis=0)
        o_ref[e] = packed.T[:sub, :]
        return carry

    lax.fori_loop(0, e_n, invert, 0)


def moe_select(aff_row, n, n_sets, row_block_off):
    cap = EC_CAPACITY * n // N_EXPERTS
    cap_pad = max(cap, V7X_LANES)
    packed = pl.pallas_call(
        functools.partial(_select_kernel, n=n, cap=cap),
        grid=(n_sets,),
        in_specs=[pl.BlockSpec((N_EXPERTS, n), lambda b: (0, b + row_block_off))],
        out_specs=pl.BlockSpec((N_EXPERTS, V7X_SUBLANES, cap_pad), lambda b: (b, 0, 0)),
        out_shape=jax.ShapeDtypeStruct((n_sets * N_EXPERTS, V7X_SUBLANES, cap_pad), F32),
        scratch_shapes=[pltpu.VMEM((N_EXPERTS, n), F32)],
        compiler_params=_cparams(("parallel",)),
        name="moe_select",
    )(aff_row)
    return packed[:, :, :cap]


def _gather_kernel(idx_ref, f_ref, o_ref, buf, *, cap):
    def body(r, carry):
        buf[pl.ds(r, 1), :] = f_ref[pl.ds(idx_ref[0, 0, r], 1), :]
        return carry

    lax.fori_loop(0, cap, body, 0, unroll=8)
    o_ref[...] = buf[...].astype(o_ref.dtype)


def moe_gather(idx, f2d, n, n_blocks):
    d = f2d.shape[1]
    cap = idx.shape[-1]
    dt = 1024
    return pl.pallas_call(
        functools.partial(_gather_kernel, cap=cap),
        grid=(n_blocks, d // dt, N_EXPERTS),
        in_specs=[pl.BlockSpec((1, 1, cap), lambda b, c, e: (b * N_EXPERTS + e, 0, 0), memory_space=pltpu.SMEM),
                  pl.BlockSpec((n, dt), lambda b, c, e: (b, c))],
        out_specs=pl.BlockSpec((None, cap, dt), lambda b, c, e: (e, b, c)),
        out_shape=jax.ShapeDtypeStruct((N_EXPERTS, n_blocks * cap, d), BF16),
        scratch_shapes=[pltpu.VMEM((cap, dt), F32)],
        compiler_params=_cparams(("parallel", "parallel", "arbitrary")),
        name="moe_gather",
    )(idx, f2d)


def _glu_kernel(x_ref, wg_ref, wu_ref, o_ref):
    x = x_ref[...]
    hg = jnp.dot(x, wg_ref[...].astype(BF16), preferred_element_type=F32)
    hu = jnp.dot(x, wu_ref[...].astype(BF16), preferred_element_type=F32)
    o_ref[...] = (_silu(hg) * hu).astype(o_ref.dtype)


def moe_glu(xe, w_gate, w_up, layer):
    e_n, m, d = xe.shape
    f = w_gate.shape[-1]
    tn = 512
    return pl.pallas_call(
        _glu_kernel,
        grid=(e_n, f // tn),
        in_specs=[pl.BlockSpec((None, m, d), lambda e, j: (e, 0, 0)),
                  pl.BlockSpec((None, None, d, tn), lambda e, j: (layer, e, 0, j)),
                  pl.BlockSpec((None, None, d, tn), lambda e, j: (layer, e, 0, j))],
        out_specs=pl.BlockSpec((None, m, tn), lambda e, j: (e, 0, j)),
        out_shape=jax.ShapeDtypeStruct((e_n, m, f), BF16),
        compiler_params=_cparams(("parallel", "arbitrary")),
        name="moe_glu",
    )(xe, w_gate, w_up)


def _down_kernel(h_ref, wd_ref, gate_ref, g_ref, o_ref, *, row_groups):
    acc = jnp.dot(h_ref[...], wd_ref[...].astype(BF16), preferred_element_type=F32)
    reps = acc.shape[1] // V7X_LANES
    for start, size, mrow in row_groups:
        gate = jnp.concatenate([gate_ref[start:start + size, :]] * reps, axis=1)
        o_ref[start:start + size, :] = (acc[start:start + size, :] * gate) * g_ref[mrow]


def moe_down(hid, w_down, gate_col, mod3, k_gate, row_groups, layer):
    e_n, m, f = hid.shape
    d = w_down.shape[-1]
    tn = 512
    nj = d // tn
    return pl.pallas_call(
        functools.partial(_down_kernel, row_groups=row_groups),
        grid=(e_n, nj),
        in_specs=[pl.BlockSpec((None, m, f), lambda e, j: (e, 0, 0)),
                  pl.BlockSpec((None, None, f, tn), lambda e, j: (layer, e, 0, j)),
                  pl.BlockSpec((None, m, V7X_LANES), lambda e, j: (e, 0, 0)),
                  pl.BlockSpec((MOD_ROWS, 1, tn), lambda e, j: (0, 0, k_gate * nj + j))],
        out_specs=pl.BlockSpec((None, m, tn), lambda e, j: (e, 0, j)),
        out_shape=jax.ShapeDtypeStruct((e_n, m, d), F32),
        compiler_params=_cparams(("parallel", "arbitrary")),
        name="moe_down",
    )(hid, w_down, gate_col, mod3)


COMBINE_GROUP = 8


def _combine_kernel(idx_ref, ye_ref, h_ref, *rest, cap):
    o_ref = rest[-1]
    e = pl.program_id(2)

    @pl.when(e == 0)
    def _():
        o_ref[...] = h_ref[...]

    def body(g, carry):
        base = g * COMBINE_GROUP
        toks = [idx_ref[0, 0, base + u] for u in range(COMBINE_GROUP)]
        rows = [o_ref[pl.ds(t, 1), :] + ye_ref[pl.ds(base + u, 1), :] for u, t in enumerate(toks)]
        for t, row in zip(toks, rows):
            o_ref[pl.ds(t, 1), :] = row
        return carry

    lax.fori_loop(0, cap // COMBINE_GROUP, body, 0)


def moe_combine(idx, ye, x2d, n, n_blocks):
    d = x2d.shape[1]
    cap = idx.shape[-1]
    dt = 1024
    return pl.pallas_call(
        functools.partial(_combine_kernel, cap=cap),
        grid=(n_blocks, d // dt, N_EXPERTS),
        in_specs=[pl.BlockSpec((1, 1, cap), lambda b, c, e: (b * N_EXPERTS + e, 0, 0), memory_space=pltpu.SMEM),
                  pl.BlockSpec((None, cap, dt), lambda b, c, e: (e, b, c)),
                  pl.BlockSpec((n, dt), lambda b, c, e: (b, c))],
        out_specs=pl.BlockSpec((n, dt), lambda b, c, e: (b, c)),
        out_shape=jax.ShapeDtypeStruct((n_blocks * n, d), F32),
        compiler_params=_cparams(("parallel", "parallel", "arbitrary")),
        name="moe_combine",
    )(idx, ye, x2d)


def _split_select(packed, n_sets):
    cap = packed.shape[-1]
    idx = packed[:, 0, :].astype(jnp.int32).reshape(n_sets, N_EXPERTS, cap)
    gate = packed[:, 1, :].reshape(n_sets, N_EXPERTS, cap)
    return idx, gate


def ec_moe(x2d, norm_g, mod3, w_router, w_gate, w_up, w_down, layer, geom, with_ctx):
    b_n, e_n = geom.batch, N_EXPERTS
    rows = geom.rows if with_ctx else geom.r_lat
    f2d, aff_row = moe_router(x2d, norm_g, mod3, w_router, geom, 3, 4, rows)
    cap = EC_CAPACITY * geom.l_lat // e_n
    idx, gate = _split_select(moe_select(aff_row, geom.l_lat, b_n, 0), b_n)
    row_groups = [(b * cap, cap, b) for b in range(b_n)]
    if with_ctx:
        cap_c = EC_CAPACITY * geom.l_ctx // e_n
        assert b_n * cap_c == cap and geom.r_ctx == geom.l_lat
        idx_c, gate_c = _split_select(moe_select(aff_row, geom.l_ctx, b_n, geom.r_lat // geom.l_ctx), b_n)
        idx_c = idx_c + (jnp.arange(b_n, dtype=jnp.int32) * geom.l_ctx)[:, None, None]
        idx = jnp.concatenate([idx, idx_c.transpose(1, 0, 2).reshape(1, e_n, cap)], axis=0)
        gate = jnp.concatenate([gate, gate_c.transpose(1, 0, 2).reshape(1, e_n, cap)], axis=0)
        row_groups += [(b_n * cap + b * cap_c, cap_c, b_n) for b in range(b_n)]
    n_blocks = idx.shape[0]
    idx = idx.reshape(n_blocks * e_n, 1, cap)
    gate_col = jnp.broadcast_to(gate.transpose(1, 0, 2).reshape(e_n, n_blocks * cap, 1),
                                (e_n, n_blocks * cap, V7X_LANES))
    xe = moe_gather(idx, f2d, geom.l_lat, n_blocks)
    ye = moe_down(moe_glu(xe, w_gate, w_up, layer), w_down, gate_col, mod3, 5, tuple(row_groups), layer)
    return moe_combine(idx, ye, x2d, geom.l_lat, n_blocks)


def _final_norm_kernel(x_ref, g_ref, o_ref):
    x = x_ref[...]
    o_ref[...] = (x * lax.rsqrt(jnp.mean(x * x, axis=-1, keepdims=True) + EPS)) * g_ref[...]


def final_norm(x2d, g, rows, tl):
    d = x2d.shape[1]
    return pl.pallas_call(
        _final_norm_kernel,
        grid=(rows // tl,),
        in_specs=[pl.BlockSpec((tl, d), lambda i: (i, 0)), pl.BlockSpec((1, d), lambda i: (0, 0))],
        out_specs=pl.BlockSpec((tl, d), lambda i: (i, 0)),
        out_shape=jax.ShapeDtypeStruct((rows, d), F32),
        compiler_params=_cparams(("parallel",)),
        name="final_norm",
    )(x2d, g.reshape(1, d))


def _group_dt(dt, rows):
    g_n, e_n = SSD_GROUPS, SSD_HEADS_PER_GROUP
    dt_col = dt.reshape(rows, 2, g_n, e_n).transpose(2, 0, 1, 3).reshape(g_n, rows, 2 * e_n)
    dt_row = dt_col.reshape(g_n, rows // SSD_CHUNK, SSD_CHUNK, 2 * e_n).transpose(0, 1, 3, 2)
    return dt_col, dt_row


def ssd_layer(x2d, mod3, norm_mix_g, w_in, conv_w, conv_b, dt_bias, a_log, d_skip, norm_g, w_out, geom, tm):
    a = norm_mod(x2d, norm_mix_g, mod3, geom, 0, 1, geom.rows, BF16)
    n_main = SSD_D_INNER + SSD_XBC
    p = matmul(a, w_in, _pick_tile(geom.rows, IN_PROJ_ROWS), 512, 0, n_main)
    p_dt = matmul(a, w_in, tm, 2 * SSD_HEADS, n_main, 2 * SSD_HEADS)
    ctx_blk = geom.r_lat // geom.l_ctx
    xbc = ssd_conv(p, conv_w, conv_b, geom)
    cs, cd = ssd_dt(p_dt, dt_bias, a_log, tm)
    cs_col, cs_row = _group_dt(cs, geom.rows)
    _, cd_row = _group_dt(cd, geom.rows)
    zero = jnp.zeros((geom.batch * SSD_GROUPS, 2, SSD_D_STATE, SSD_GROUP_WIDTH), F32)
    y_ctx, s_ctx = ssd_scan(xbc, cs_col, cs_row, cd_row, d_skip, zero, geom.l_ctx, geom.batch, ctx_blk)
    y_lat, _ = ssd_scan(xbc, cs_col, cs_row, cd_row, d_skip, s_ctx, geom.l_lat, geom.batch, 0)
    yb = ssd_out(y_lat, y_ctx, p, norm_g, _pick_tile(geom.l_ctx, 256))
    return matmul_resid(yb, w_out, x2d, mod3, geom, 2, _pick_tile(geom.r_ctx, OUT_PROJ_ROWS), OUT_PROJ_COLS)


def ret_layer(x2d, mod3, norm_mix_g, w_in, decay, gn_g, w_out, geom, tm):
    a = norm_mod(x2d, norm_mix_g, mod3, geom, 0, 1, geom.rows, BF16)
    p_lat = matmul(a, w_in, _pick_tile(geom.r_lat, IN_PROJ_ROWS), 512, 0, None, 0, geom.r_lat)
    p_ctx = matmul(a, w_in, _pick_tile(geom.r_ctx, IN_PROJ_ROWS), 512, D_MODEL, D_MODEL + RET_D_V,
                   geom.r_lat, geom.r_ctx)
    decay_b = jnp.broadcast_to(decay.T.reshape(RET_HEADS, 2, 1), (RET_HEADS, 2, V7X_LANES))
    cos_t, sin_t = rope_tables(geom.l_lat)
    s_ctx = ret_ctx_state(p_ctx, decay_b, geom)
    yb = ret_scan(p_lat, cos_t, sin_t, decay_b, gn_g, s_ctx, geom)
    return matmul_resid(yb, w_out, x2d, mod3, geom, 2, _pick_tile(geom.r_ctx, OUT_PROJ_ROWS), OUT_PROJ_COLS)


def kernel(x, c, ctx, c_ctx, ada_w, ada_b, norm_mix_g, norm_ffn_g, ssd_w_in, ssd_conv_w, ssd_conv_b,
           ssd_dt_bias, ssd_a_log, ssd_d, ssd_norm_g, ssd_w_out, ret_w_in, ret_decay, ret_gn_g, ret_w_out,
           moe_w_router, moe_w_gate, moe_w_up, moe_w_down, final_norm_g):
    batch, l_lat, d = x.shape
    l_ctx = ctx.shape[1]
    depth = ada_w.shape[0]
    n_mixers = 2
    assert batch + 1 <= MOD_ROWS and d == D_MODEL
    assert depth == n_mixers, "the retention layer must be the last one: its context branch is state-only"
    geom = Geom(batch, l_lat, l_ctx)
    assert geom.r_ctx == l_lat, "all context sequences together must fill exactly one latent-sized row block"
    tm = _pick_tile(geom.r_ctx, 1024)
    x2d = jnp.concatenate([x.reshape(geom.r_lat, d), ctx.reshape(geom.r_ctx, d)], axis=0)
    cc = jnp.zeros((MOD_ROWS, d), F32).at[:batch].set(c).at[batch].set(c_ctx)
    mod = ada_table(cc, ada_w, ada_b)
    for i in range(depth):
        need_ctx = i < depth - 1
        j = i // n_mixers
        mod3 = mod[i].reshape(MOD_ROWS, 1, N_MOD * d)
        if i % n_mixers == 0:
            x2d = ssd_layer(x2d, mod3, norm_mix_g[i], ssd_w_in[j], ssd_conv_w[j], ssd_conv_b[j], ssd_dt_bias[j],
                            ssd_a_log[j], ssd_d[j], ssd_norm_g[j], ssd_w_out[j], geom, tm)
        else:
            x2d = ret_layer(x2d, mod3, norm_mix_g[i], ret_w_in[j], ret_decay[j], ret_gn_g[j], ret_w_out[j],
                            geom, tm)
        x2d = ec_moe(x2d, norm_ffn_g[i], mod3, moe_w_router[i], moe_w_gate, moe_w_up, moe_w_down, i, geom,
                     need_ctx)
    out = final_norm(x2d, final_norm_g, geom.r_lat, _pick_tile(geom.l_ctx, 256))
    return out.reshape(batch, l_lat, d)
```

```python
import functools
import math

import jax
import jax.numpy as jnp
from jax import lax
from jax.experimental import pallas as pl
from jax.experimental.pallas import tpu as pltpu

F32 = jnp.float32
BF16 = jnp.bfloat16
HIGHEST = lax.Precision.HIGHEST
LOG2_E = 1.4426950408889634

D_MODEL = 2048
N_MOD = 6
EPS = 1e-6

SSD_D_INNER = 2 * D_MODEL
SSD_HEAD_DIM = 64
SSD_HEADS = SSD_D_INNER // SSD_HEAD_DIM
SSD_GROUPS = 8
SSD_HEADS_PER_GROUP = SSD_HEADS // SSD_GROUPS
SSD_D_STATE = 128
SSD_CONV_W = 5
SSD_CHUNK = 128
SSD_BC = SSD_GROUPS * SSD_D_STATE
SSD_XBC = SSD_D_INNER + 2 * SSD_BC
SSD_IN = SSD_D_INNER + SSD_XBC + 2 * SSD_HEADS
SSD_GROUP_WIDTH = SSD_HEADS_PER_GROUP * SSD_HEAD_DIM

RET_HEADS = 8
RET_QK_DIM = D_MODEL // RET_HEADS
RET_V_DIM = 2 * RET_QK_DIM
RET_D_V = RET_HEADS * RET_V_DIM
RET_CHUNK = 128
RET_SCAN_CHUNK = 256
RET_IN = 2 * D_MODEL + 2 * RET_D_V
ROPE_BASE = 10000.0
GRID_W = 64

N_EXPERTS = 16
EC_CAPACITY = 2
D_EXPERT = D_MODEL

V7X_LANES = 128
V7X_SUBLANES = 8
V7X_VMEM_LIMIT_BYTES = 56 * 1024 * 1024
MOD_ROWS = 16
ROUTER_LANES = V7X_LANES
IN_PROJ_ROWS = 2048
OUT_PROJ_ROWS = 1024
OUT_PROJ_COLS = 512
NORM_ROWS = 1024
ROUTER_ROWS = 512


def _cparams(sem, vmem=V7X_VMEM_LIMIT_BYTES):
    return pltpu.CompilerParams(dimension_semantics=sem, vmem_limit_bytes=vmem)


def _silu(x):
    return x * jax.nn.sigmoid(x)


def _pick_tile(n, target):
    t = min(n, target)
    while n % t:
        t //= 2
    return t


def _ada_kernel(c_ref, w_ref, b_ref, o_ref):
    a = _silu(c_ref[...]).astype(BF16)
    o_ref[...] = jnp.dot(a, w_ref[...].astype(BF16), preferred_element_type=F32) + b_ref[...]


def ada_table(cc, ada_w, ada_b):
    depth, d, n = ada_w.shape
    tn = 1024
    return pl.pallas_call(
        _ada_kernel,
        grid=(depth, n // tn),
        in_specs=[pl.BlockSpec((MOD_ROWS, d), lambda i, j: (0, 0)),
                  pl.BlockSpec((None, d, tn), lambda i, j: (i, 0, j)),
                  pl.BlockSpec((None, 1, tn), lambda i, j: (i, 0, j))],
        out_specs=pl.BlockSpec((None, MOD_ROWS, tn), lambda i, j: (i, 0, j)),
        out_shape=jax.ShapeDtypeStruct((depth, MOD_ROWS, n), F32),
        compiler_params=_cparams(("parallel", "parallel")),
        name="ada_table",
    )(cc, ada_w, ada_b.reshape(depth, 1, n))


class Geom:
    def __init__(self, batch, l_lat, l_ctx):
        self.batch, self.l_lat, self.l_ctx = batch, l_lat, l_ctx
        self.r_lat = batch * l_lat
        self.r_ctx = batch * l_ctx
        self.rows = self.r_lat + self.r_ctx

    def row_tile(self, target):
        return _pick_tile(math.gcd(self.l_lat, self.r_ctx), target)

    def mod_row(self, i, tile):
        return jnp.where(i * tile < self.r_lat, (i * tile) // self.l_lat, self.batch)


def _stream_specs(parts, tile, cols, col_of):
    if len(parts) == 1:
        return [pl.BlockSpec((tile, cols), lambda i, *j: (i, col_of(*j)))]
    n0 = parts[0].shape[0] // tile
    return [pl.BlockSpec((tile, cols), lambda i, *j: (jnp.minimum(i, n0 - 1), jnp.where(i < n0, col_of(*j), 0))),
            pl.BlockSpec((tile, cols), lambda i, *j: (jnp.maximum(i - n0, 0), jnp.where(i < n0, 0, col_of(*j))))]


def _stream_tile(x_refs, n_lat_tiles):
    if len(x_refs) == 1:
        return x_refs[0][...]
    return jnp.where(pl.program_id(0) < n_lat_tiles, x_refs[0][...], x_refs[1][...])


def _norm_mod_kernel(*refs, n_lat_tiles):
    x_refs, (g_ref, sc_ref, sh_ref, o_ref) = refs[:-4], refs[-4:]
    x = _stream_tile(x_refs, n_lat_tiles)
    y = x * lax.rsqrt(jnp.mean(x * x, axis=-1, keepdims=True) + EPS)
    o_ref[...] = ((y * g_ref[...]) * (1.0 + sc_ref[...]) + sh_ref[...]).astype(o_ref.dtype)


def norm_mod(parts, g, mod3, geom, k_shift, k_scale, rows, out_dtype):
    d = parts[0].shape[1]
    tl = geom.row_tile(NORM_ROWS)
    return pl.pallas_call(
        functools.partial(_norm_mod_kernel, n_lat_tiles=geom.r_lat // tl),
        grid=(rows // tl,),
        in_specs=_stream_specs(parts, tl, d, lambda: 0) + [
            pl.BlockSpec((1, d), lambda i: (0, 0)),
            pl.BlockSpec((None, 1, d), lambda i: (geom.mod_row(i, tl), 0, k_scale)),
            pl.BlockSpec((None, 1, d), lambda i: (geom.mod_row(i, tl), 0, k_shift))],
        out_specs=pl.BlockSpec((tl, d), lambda i: (i, 0)),
        out_shape=jax.ShapeDtypeStruct((rows, d), out_dtype),
        compiler_params=_cparams(("parallel",)),
        name="norm_mod",
    )(*parts, g.reshape(1, d), mod3, mod3)


def _mm_kernel(a_ref, w_ref, o_ref):
    o_ref[...] = jnp.dot(a_ref[...], w_ref[...].astype(BF16), preferred_element_type=F32)


def matmul(a, w, tm, tn, col_start=0, n=None, row_start=0, m=None):
    k = a.shape[1]
    m = a.shape[0] - row_start if m is None else m
    n = w.shape[1] - col_start if n is None else n
    assert n % tn == 0 and col_start % tn == 0 and m % tm == 0 and row_start % tm == 0
    i0, j0 = row_start // tm, col_start // tn
    return pl.pallas_call(
        _mm_kernel,
        grid=(m // tm, n // tn),
        in_specs=[pl.BlockSpec((tm, k), lambda i, j: (i + i0, 0)),
                  pl.BlockSpec((k, tn), lambda i, j: (0, j + j0))],
        out_specs=pl.BlockSpec((tm, tn), lambda i, j: (i, j)),
        out_shape=jax.ShapeDtypeStruct((m, n), F32),
        compiler_params=_cparams(("parallel", "arbitrary")),
        name="matmul",
    )(a, w)


def _mm_resid_kernel(a_ref, w_ref, *refs, n_lat_tiles):
    r_refs, (g_ref, o_ref) = refs[:-2], refs[-2:]
    acc = jnp.dot(a_ref[...], w_ref[...].astype(BF16), preferred_element_type=F32)
    o_ref[...] = _stream_tile(r_refs, n_lat_tiles) + g_ref[...] * acc


def matmul_resid(a, w, res_parts, mod3, geom, k_gate, tm, tn):
    m, k = a.shape
    n = w.shape[1]
    nj = n // tn
    return pl.pallas_call(
        functools.partial(_mm_resid_kernel, n_lat_tiles=geom.r_lat // tm),
        grid=(m // tm, nj),
        in_specs=[pl.BlockSpec((tm, k), lambda i, j: (i, 0)),
                  pl.BlockSpec((k, tn), lambda i, j: (0, j))]
        + _stream_specs(res_parts, tm, tn, lambda j: j)
        + [pl.BlockSpec((None, 1, tn), lambda i, j: (geom.mod_row(i, tm), 0, k_gate * nj + j))],
        out_specs=pl.BlockSpec((tm, tn), lambda i, j: (i, j)),
        out_shape=jax.ShapeDtypeStruct((m, n), F32),
        compiler_params=_cparams(("parallel", "arbitrary")),
        name="matmul_resid",
    )(a, w, *res_parts, mod3)


def _conv_kernel(u_ref, w_ref, b_ref, o_ref, *, n_lat_blocks, l_ctx):
    u = u_ref[...]
    l, ct = u.shape
    pad = SSD_CONV_W // 2
    t = lax.broadcasted_iota(jnp.int32, (l, V7X_LANES), 0)
    is_ctx = pl.program_id(0) >= n_lat_blocks
    pos = jnp.where(is_ctx, t & (l_ctx - 1), t)
    seq_len = jnp.where(is_ctx, l_ctx, l)
    acc = u * w_ref[pad:pad + 1, :] + b_ref[...]
    for k in range(SSD_CONV_W):
        off = k - pad
        if off == 0:
            continue
        shifted = pltpu.roll(u, (-off) % l, axis=0)
        inside =jnp.where((pos + off >= 0) & (pos + off < seq_len), 1.0, 0.0)
        acc = acc + (shifted * jnp.concatenate([inside] * (ct // V7X_LANES), axis=1)) * w_ref[k:k + 1, :]
    o_ref[...] = _silu(acc)


def ssd_conv(p, conv_w, conv_b, geom):
    rows = p.shape[0]
    ct = 512
    col_off = SSD_D_INNER // ct
    assert geom.l_ctx & (geom.l_ctx - 1) == 0
    return pl.pallas_call(
        functools.partial(_conv_kernel, n_lat_blocks=geom.batch, l_ctx=geom.l_ctx),
        grid=(rows // geom.l_lat, SSD_XBC // ct),
        in_specs=[pl.BlockSpec((geom.l_lat, ct), lambda s, c: (s, c + col_off)),
                  pl.BlockSpec((SSD_CONV_W, ct), lambda s, c: (0, c)),
                  pl.BlockSpec((1, ct), lambda s, c: (0, c))],
        out_specs=pl.BlockSpec((geom.l_lat, ct), lambda s, c: (s, c)),
        out_shape=jax.ShapeDtypeStruct((rows, SSD_XBC), F32),
        compiler_params=_cparams(("parallel", "parallel")),
        name="ssd_conv",
    )(p, conv_w, conv_b.reshape(1, SSD_XBC))


def _dt_kernel(p_ref, b_ref, al_ref, cs_ref, cd_ref):
    q = SSD_CHUNK
    dt = jax.nn.softplus(p_ref[...] + b_ref[...])
    log2_dt = jnp.log2(dt)
    dta = dt * (-jnp.exp(al_ref[...]) * LOG2_E)
    row = lax.broadcasted_iota(jnp.int32, (q, q), 0)
    col = lax.broadcasted_iota(jnp.int32, (q, q), 1)
    prefix = (col <= row).astype(F32)
    suffix = (col >= row).astype(F32)
    fwd_cols = lax.broadcasted_iota(jnp.int32, (q, dt.shape[1]), 1) < SSD_HEADS
    for c in range(dt.shape[0] // q):
        blk = dta[c * q:(c + 1) * q, :]
        cs = jnp.where(fwd_cols,
                       jnp.dot(prefix, blk, precision=HIGHEST, preferred_element_type=F32),
                       jnp.dot(suffix, blk, precision=HIGHEST, preferred_element_type=F32))
        cs_ref[c * q:(c + 1) * q, :] = cs
        cd_ref[c * q:(c + 1) * q, :] = cs - log2_dt[c * q:(c + 1) * q, :]


def ssd_dt(p, dt_bias, a_log, tl):
    rows, w = p.shape
    assert tl % SSD_CHUNK == 0
    return pl.pallas_call(
        _dt_kernel,
        grid=(rows // tl,),
        in_specs=[pl.BlockSpec((tl, w), lambda i: (i, 0)),
                  pl.BlockSpec((1, w), lambda i: (0, 0)),
                  pl.BlockSpec((1, w), lambda i: (0, 0))],
        out_specs=[pl.BlockSpec((tl, w), lambda i: (i, 0)), pl.BlockSpec((tl, w), lambda i: (i, 0))],
        out_shape=[jax.ShapeDtypeStruct((rows, w), F32), jax.ShapeDtypeStruct((rows, w), F32)],
        compiler_params=_cparams(("parallel",)),
        name="ssd_dt",
    )(p, dt_bias.reshape(1, w), a_log.reshape(1, w))


def _ssd_scan_kernel(x_ref, b_ref, c_ref, csc_ref, csr_ref, cdr_ref, dsk_ref, s0_ref, *rest, nc):
    y_ref, sfin_ref, s_scr, yb_scr = rest[-4], rest[-3], rest[-2], rest[-1]
    q = SSD_CHUNK
    e_n = SSD_HEADS_PER_GROUP
    row = lax.broadcasted_iota(jnp.int32, (q, q), 0)
    col = lax.broadcasted_iota(jnp.int32, (q, q), 1)
    left = col < SSD_HEAD_DIM

    def chunk_step(d, c):
        if d == 0:
            incl, edge = col <= row, q - 1
        else:
            incl, edge = col >= row, 0
        t0 = pl.multiple_of(c * q, q)
        bm = b_ref[pl.ds(t0, q), :]
        cm = c_ref[pl.ds(t0, q), :]
        cs = csc_ref[0, pl.ds(t0, q), :][:, d * e_n:(d + 1) * e_n]
        tot_t = csr_ref[0, c][d * e_n:(d + 1) * e_n, :][:, edge:edge + 1]
        cd_t = cdr_ref[0, c][d * e_n:(d + 1) * e_n, :]
        w_t = jnp.exp2(tot_t - cd_t)
        et_b = jnp.broadcast_to(jnp.exp2(tot_t), (e_n, q))
        cmb = cm.astype(BF16)
        cb = lax.dot_general(cmb, bm.astype(BF16), (((1,), (1,)), ((), ())),
                             preferred_element_type=F32)
        bm_t = bm.T
        y_inter = jnp.dot(cmb, s_scr[d].astype(BF16), preferred_element_type=F32)
        for pr in range(e_n // 2):
            sl = slice(pr * 2 * SSD_HEAD_DIM, (pr + 1) * 2 * SSD_HEAD_DIM)
            mix, bw, ecs = [], [], []
            for e in (2 * pr, 2 * pr + 1):
                cs_b = jnp.broadcast_to(cs[:, e:e + 1], (q, q))
                seg = jnp.exp2(jnp.where(incl, cs_b - cd_t[e:e + 1, :], -jnp.inf))
                mix.append((cb * seg).astype(BF16))
                bw.append((bm_t * w_t[e:e + 1, :]).astype(BF16))
                ecs.append(jnp.exp2(cs_b))
            xs = x_ref[pl.ds(t0, q), sl]
            ss = s_scr[d, :, sl]
            lhs1 = jnp.concatenate([jnp.concatenate(mix, axis=1), jnp.concatenate(bw, axis=1)], axis=0)
            rhs1 = jnp.concatenate([jnp.where(left, xs, 0.0), jnp.where(left, 0.0, xs)], axis=0).astype(BF16)
            r1 = jnp.dot(lhs1, rhs1, preferred_element_type=F32)
            y = r1[:q] + y_inter[:, sl] * jnp.where(left, ecs[0], ecs[1])
            dec = jnp.where(left[:1], et_b[2 * pr:2 * pr + 1, :], et_b[2 * pr + 1:2 * pr + 2, :])
            s_scr[d, :, sl] = ss * dec + r1[q:]
            if d == 0:
                y_ref[pl.ds(t0, q), sl] = y + dsk_ref[:, sl] * xs
            else:
                yb_scr[pl.ds(t0, q), sl] = y

    s_scr[...] = s0_ref[0]

    def body(ci, carry):
        chunk_step(0, ci)
        chunk_step(1, nc - 1 - ci)
        return carry

    lax.fori_loop(0, nc, body, 0)
    sfin_ref[0] = s_scr[...]
    y_ref[...] = y_ref[...] + yb_scr[...]


def ssd_scan(xbc, cs_col, cs_row, cd_row, d_skip, s0, seq_len, n_seq, row_block_off):
    g_n, e_n, gw = SSD_GROUPS, SSD_HEADS_PER_GROUP, SSD_GROUP_WIDTH
    nc = seq_len // SSD_CHUNK
    b_off = SSD_D_INNER // SSD_D_STATE
    c_off = (SSD_D_INNER + SSD_BC) // SSD_D_STATE
    dsk = jnp.repeat(d_skip, SSD_HEAD_DIM).reshape(1, SSD_D_INNER)
    row_spec = pl.BlockSpec((1, nc, 2 * e_n, SSD_CHUNK), lambda s, g: (g, s + row_block_off, 0, 0))
    in_specs = [pl.BlockSpec((seq_len, gw), lambda s, g: (s + row_block_off, g)),
                pl.BlockSpec((seq_len, SSD_D_STATE), lambda s, g: (s + row_block_off, b_off + g)),
                pl.BlockSpec((seq_len, SSD_D_STATE), lambda s, g: (s + row_block_off, c_off + g)),
                pl.BlockSpec((1, seq_len, 2 * e_n), lambda s, g: (g, s + row_block_off, 0)),
                row_spec,
                row_spec,
                pl.BlockSpec((1, gw), lambda s, g: (0, g)),
                pl.BlockSpec((1, 2, SSD_D_STATE, gw), lambda s, g: (s * g_n + g, 0, 0, 0))]
    return pl.pallas_call(
        functools.partial(_ssd_scan_kernel, nc=nc),
        grid=(n_seq, g_n),
        in_specs=in_specs,
        out_specs=[pl.BlockSpec((seq_len, gw), lambda s, g: (s, g)),
                   pl.BlockSpec((1, 2, SSD_D_STATE, gw), lambda s, g: (s * g_n + g, 0, 0, 0))],
        out_shape=[jax.ShapeDtypeStruct((n_seq * seq_len, SSD_D_INNER), F32),
                   jax.ShapeDtypeStruct((n_seq * g_n, 2, SSD_D_STATE, gw), F32)],
        scratch_shapes=[pltpu.VMEM((2, SSD_D_STATE, gw), F32), pltpu.VMEM((seq_len, gw), F32)],
        compiler_params=_cparams(("parallel", "parallel")),
        name="ssd_scan",
    )(xbc, xbc, xbc, cs_col, cs_row, cd_row, dsk, s0)


def _ssd_out_kernel(yl_ref, yc_ref, z_ref, g_ref, o_ref, *, n_lat_tiles):
    is_ctx = pl.program_id(0) >= n_lat_tiles
    y = jnp.where(is_ctx, yc_ref[...], yl_ref[...])
    v = y * _silu(z_ref[...])
    n = v * lax.rsqrt(jnp.mean(v * v, axis=-1, keepdims=True) + EPS)
    o_ref[...] = (n * g_ref[...]).astype(o_ref.dtype)


def ssd_out(y_lat, y_ctx, p, norm_g, tl):
    n_lat, n_ctx = y_lat.shape[0] // tl, y_ctx.shape[0] // tl
    rows = y_lat.shape[0] + y_ctx.shape[0]
    return pl.pallas_call(
        functools.partial(_ssd_out_kernel, n_lat_tiles=n_lat),
        grid=(n_lat + n_ctx,),
        in_specs=[pl.BlockSpec((tl, SSD_D_INNER), lambda i: (jnp.minimum(i, n_lat - 1), 0)),
                  pl.BlockSpec((tl, SSD_D_INNER), lambda i: (jnp.maximum(i - n_lat, 0), 0)),
                  pl.BlockSpec((tl, SSD_D_INNER), lambda i: (i, 0)),
                  pl.BlockSpec((1, SSD_D_INNER), lambda i: (0, 0))],
        out_specs=pl.BlockSpec((tl, SSD_D_INNER), lambda i: (i, 0)),
        out_shape=jax.ShapeDtypeStruct((rows, SSD_D_INNER), BF16),
        compiler_params=_cparams(("parallel",)),
        name="ssd_out",
    )(y_lat, y_ctx, p, norm_g.reshape(1, SSD_D_INNER))


def _ret_state_kernel(k_ref, v_ref, ld_ref, s_ref):
    l = k_ref.shape[0]
    k = k_ref[...] * (RET_QK_DIM ** -0.5)
    vb = v_ref[...].astype(BF16)
    pos = lax.broadcasted_iota(jnp.int32, (l, V7X_LANES), 0).astype(F32)
    for d in range(2):
        ld = -jnp.exp(ld_ref[0, d:d + 1, :])
        steps = (l - 1.0 - pos) if d == 0 else pos
        w = jnp.exp(steps * ld)
        kd = (k * jnp.concatenate([w] * (RET_QK_DIM // V7X_LANES), axis=1)).astype(BF16)
        s_ref[0, d] = lax.dot_general(kd, vb, (((0,), (0,)), ((), ())), preferred_element_type=F32)


def ret_ctx_state(p, decay_b, geom):
    l = geom.l_ctx
    row_off = 0
    k_off = 0
    v_off = D_MODEL // RET_V_DIM
    return pl.pallas_call(
        _ret_state_kernel,
        grid=(geom.batch, RET_HEADS),
        in_specs=[pl.BlockSpec((l, RET_QK_DIM), lambda b, h: (b + row_off, k_off + h)),
                  pl.BlockSpec((l, RET_V_DIM), lambda b, h: (b + row_off, v_off + h)),
                  pl.BlockSpec((1, 2, V7X_LANES), lambda b, h: (h, 0, 0))],
        out_specs=pl.BlockSpec((1, 2, RET_QK_DIM, RET_V_DIM), lambda b, h: (b * RET_HEADS + h, 0, 0, 0)),
        out_shape=jax.ShapeDtypeStruct((geom.batch * RET_HEADS, 2, RET_QK_DIM, RET_V_DIM), F32),
        compiler_params=_cparams(("parallel", "parallel")),
        name="ret_ctx_state",
    )(p, p, decay_b)


def _rope(u, cos, sin):
    parts = []
    for j in range(u.shape[1] // V7X_LANES):
        s = u[:, j * V7X_LANES:(j + 1) * V7X_LANES]
        parts.append(pltpu.roll(s, V7X_LANES // 2, axis=1))
    return u * cos + jnp.concatenate(parts, axis=1) * sin


def _ret_scan_kernel(q_ref, k_ref, v_ref, g_ref, cos_ref, sin_ref, ld_ref, gn_ref, s0_ref, o_ref,
                     s_scr, qb_scr, sb_scr, acc_scr, *, nc):
    q = RET_SCAN_CHUNK
    nslab_k = RET_QK_DIM // V7X_LANES
    nslab_v = RET_V_DIM // V7X_LANES
    rel = (lax.broadcasted_iota(jnp.int32, (q, q), 0) - lax.broadcasted_iota(jnp.int32, (q, q), 1)).astype(F32)
    l_i = lax.broadcasted_iota(jnp.int32, (q, V7X_LANES), 0).astype(F32)
    ld_f = -jnp.exp(ld_ref[0, 0:1, :])
    ld_b = -jnp.exp(ld_ref[0, 1:2, :])
    decay_in = (jnp.exp(jnp.where(rel >= 0, rel * ld_f[:, :1], -jnp.inf))
                + jnp.exp(jnp.where(rel <= 0, -rel * ld_b[:, :1], -jnp.inf)))
    from_state = (jnp.exp((l_i + 1.0) * ld_f), jnp.exp((q - l_i) * ld_b))
    to_end = (jnp.exp((q - 1.0 - l_i) * ld_f), jnp.exp(l_i * ld_b))
    chunk_decay = (jnp.exp(q * ld_f), jnp.exp(q * ld_b))

    def roped_key(t0):
        return _rope(k_ref[pl.ds(t0, q), :], cos_ref[pl.ds(t0, q), :], sin_ref[pl.ds(t0, q), :]) * (RET_QK_DIM ** -0.5)

    def update_state(d, k, vb):
        kd = (k * jnp.concatenate([to_end[d]] * nslab_k, axis=1)).astype(BF16)
        s_scr[d] = (s_scr[d] * jnp.concatenate([chunk_decay[d]] * nslab_v, axis=1)
                    + lax.dot_general(kd, vb, (((0,), (0,)), ((), ())), preferred_element_type=F32))

    s_scr[...] = s0_ref[0]

    def scan(ci, carry):
        t0 = pl.multiple_of(ci * q, q)
        qb = _rope(q_ref[pl.ds(t0, q), :], cos_ref[pl.ds(t0, q), :], sin_ref[pl.ds(t0, q), :]).astype(BF16)
        k = roped_key(t0)
        vb = v_ref[pl.ds(t0, q), :].astype(BF16)
        qb_scr[pl.ds(t0, q), :] = qb
        scores = lax.dot_general(qb, k.astype(BF16), (((1,), (1,)), ((), ())),
                                 preferred_element_type=F32) * decay_in
        inter_f = (jnp.dot(qb, s_scr[0].astype(BF16), preferred_element_type=F32)
                   * jnp.concatenate([from_state[0]] * nslab_v, axis=1))
        acc_scr[pl.ds(t0, q), :] = jnp.dot(scores.astype(BF16), vb, preferred_element_type=F32) + inter_f
        update_state(0, k, vb)
        cb = nc - 1 - ci
        tb = pl.multiple_of(cb * q, q)
        sb_scr[cb] = s_scr[1].astype(BF16)
        update_state(1, roped_key(tb), v_ref[pl.ds(tb, q), :].astype(BF16))
        return carry

    lax.fori_loop(0, nc, scan, 0)

    def finish(c, carry):
        t0 = pl.multiple_of(c * q, q)
        o = acc_scr[pl.ds(t0, q), :] + (jnp.dot(qb_scr[pl.ds(t0, q), :], sb_scr[c], preferred_element_type=F32)
                                        * jnp.concatenate([from_state[1]] * nslab_v, axis=1))
        mu = jnp.mean(o, axis=-1, keepdims=True)
        var = jnp.mean(jnp.square(o - mu), axis=-1, keepdims=True)
        on = ((o - mu) * lax.rsqrt(var + EPS)) * gn_ref[...]
        o_ref[pl.ds(t0, q), :] = (on * _silu(g_ref[pl.ds(t0, q), :])).astype(o_ref.dtype)
        return carry

    lax.fori_loop(0, nc, finish, 0, unroll=2 if nc % 2 == 0 else 1)


def ret_scan(p, cos_t, sin_t, decay_b, gn_g, s0, geom):
    l = geom.l_lat
    assert l % RET_SCAN_CHUNK == 0
    nc = l // RET_SCAN_CHUNK
    k_off = D_MODEL // RET_QK_DIM
    v_off = 2 * D_MODEL // RET_V_DIM
    g_off = (2 * D_MODEL + RET_D_V) // RET_V_DIM
    return pl.pallas_call(
        functools.partial(_ret_scan_kernel, nc=nc),
        grid=(geom.batch, RET_HEADS),
        in_specs=[pl.BlockSpec((l, RET_QK_DIM), lambda b, h: (b, h)),
                  pl.BlockSpec((l, RET_QK_DIM), lambda b, h: (b, k_off + h)),
                  pl.BlockSpec((l, RET_V_DIM), lambda b, h: (b, v_off + h)),
                  pl.BlockSpec((l, RET_V_DIM), lambda b, h: (b, g_off + h)),
                  pl.BlockSpec((l, RET_QK_DIM), lambda b, h: (0, 0)),
                  pl.BlockSpec((l, RET_QK_DIM), lambda b, h: (0, 0)),
                  pl.BlockSpec((1, 2, V7X_LANES), lambda b, h: (h, 0, 0)),
                  pl.BlockSpec((1, RET_V_DIM), lambda b, h: (0, h)),
                  pl.BlockSpec((1, 2, RET_QK_DIM, RET_V_DIM), lambda b, h: (b * RET_HEADS + h, 0, 0, 0))],
        out_specs=pl.BlockSpec((l, RET_V_DIM), lambda b, h: (b, h)),
        out_shape=jax.ShapeDtypeStruct((geom.r_lat, RET_D_V), BF16),
        scratch_shapes=[pltpu.VMEM((2, RET_QK_DIM, RET_V_DIM), F32),
                        pltpu.VMEM((l, RET_QK_DIM), BF16),
                        pltpu.VMEM((nc, RET_QK_DIM, RET_V_DIM), BF16),
                        pltpu.VMEM((l, RET_V_DIM), F32)],
        compiler_params=_cparams(("parallel", "parallel")),
        name="ret_scan",
    )(p, p, p, p, cos_t, sin_t, decay_b, gn_g.reshape(1, RET_D_V), s0)


def rope_tables(l_lat):
    half = RET_QK_DIM // 4
    pos = jnp.arange(l_lat)
    freqs = ROPE_BASE ** (-jnp.arange(half, dtype=F32) / half)
    cs, sn = [], []
    for ids in (pos // GRID_W, pos % GRID_W):
        ang = ids.astype(F32)[:, None] * freqs[None, :]
        cs += [jnp.cos(ang), jnp.cos(ang)]
        sn += [-jnp.sin(ang), jnp.sin(ang)]
    return jnp.concatenate(cs, axis=1), jnp.concatenate(sn, axis=1)


def _router_kernel(x_ref, g_ref, sc_ref, sh_ref, wr_ref, f_ref, ar_ref):
    x = x_ref[...]
    y = x * lax.rsqrt(jnp.mean(x * x, axis=-1, keepdims=True) + EPS)
    f = (y * g_ref[...]) * (1.0 + sc_ref[...]) + sh_ref[...]
    f_ref[...] = f
    logits = jnp.dot(f, wr_ref[...], precision=HIGHEST, preferred_element_type=F32)
    lane = lax.broadcasted_iota(jnp.int32, logits.shape, 1)
    logits = jnp.where(lane < N_EXPERTS, logits, -jnp.inf)
    un = jnp.exp(logits - jnp.max(logits, axis=-1, keepdims=True))
    aff = un / jnp.sum(un, axis=-1, keepdims=True)
    ar_ref[...] = aff.T[:N_EXPERTS, :]


def moe_router(x2d, g, mod3, w_router, geom, k_shift, k_scale, rows):
    d = x2d.shape[1]
    tl = geom.row_tile(ROUTER_ROWS)
    wr = jnp.pad(w_router, ((0, 0), (0, ROUTER_LANES - N_EXPERTS)))
    return pl.pallas_call(
        _router_kernel,
        grid=(rows // tl,),
        in_specs=[pl.BlockSpec((tl, d), lambda i: (i, 0)),
                  pl.BlockSpec((1, d), lambda i: (0, 0)),
                  pl.BlockSpec((None, 1, d), lambda i: (geom.mod_row(i, tl), 0, k_scale)),
                  pl.BlockSpec((None, 1, d), lambda i: (geom.mod_row(i, tl), 0, k_shift)),
                  pl.BlockSpec((d, ROUTER_LANES), lambda i: (0, 0))],
        out_specs=[pl.BlockSpec((tl, d), lambda i: (i, 0)),
                   pl.BlockSpec((N_EXPERTS, tl), lambda i: (0, i))],
        out_shape=[jax.ShapeDtypeStruct((rows, d), F32),
                   jax.ShapeDtypeStruct((N_EXPERTS, rows), F32)],
        compiler_params=_cparams(("parallel",)),
        name="moe_router",
    )(x2d, g.reshape(1, d), mod3, mod3, wr)


def _select_kernel(ar_ref, o_ref, slot_scr, *, n, cap):
    e_n, lanes, sub = N_EXPERTS, V7X_LANES, V7X_SUBLANES
    a = ar_ref[...]
    bits = pltpu.bitcast(a, jnp.int32)

    def bisect(i, thr):
        cand = thr | lax.shift_left(jnp.int32(1), 30 - i)
        cnt = jnp.sum(jnp.where(bits >= cand, 1.0, 0.0), axis=1, keepdims=True)
        return jnp.where(cnt >= cap, cand, thr)

    thr = lax.fori_loop(0, 31, bisect, jnp.zeros((e_n, 1), jnp.int32))
    gt = jnp.where(bits > thr, 1.0, 0.0)
    eq = jnp.where(bits == thr, 1.0, 0.0)
    need = cap - jnp.sum(gt, axis=1, keepdims=True)
    upper = (lax.broadcasted_iota(jnp.int32, (lanes, lanes), 0)
             <= lax.broadcasted_iota(jnp.int32, (lanes, lanes), 1)).astype(BF16)
    ties_before = jnp.zeros((e_n, 1), F32)
    taken_before = jnp.zeros((e_n, 1), F32)
    for j in range(n // lanes):
        sl = slice(j * lanes, (j + 1) * lanes)
        eq_j = eq[:, sl]
        tie_incl = jnp.dot(eq_j.astype(BF16), upper, preferred_element_type=F32)
        tie_pos = ties_before + tie_incl - eq_j
        ties_before = ties_before + tie_incl[:, lanes - 1:lanes]
        sel_j = gt[:, sl] + eq_j * jnp.where(tie_pos < need, 1.0, 0.0)
        sel_incl = jnp.dot(sel_j.astype(BF16), upper, preferred_element_type=F32)
        slot_scr[:, sl] = jnp.where(sel_j > 0.0, taken_before + sel_incl - sel_j, -1.0)
        taken_before = taken_before + sel_incl[:, lanes - 1:lanes]

    r_col = lax.broadcasted_iota(jnp.int32, (cap, n), 0).astype(F32)
    t_row = lax.broadcasted_iota(jnp.int32, (cap, n), 1).astype(F32)
    lane_o = lax.broadcasted_iota(jnp.int32, (cap, lanes), 1)
    cap_pad = o_ref.shape[-1]

    def invert(e, carry):
        onehot = slot_scr[pl.ds(e, 1), :] == r_col
        idx = jnp.sum(jnp.where(onehot, t_row, 0.0), axis=1, keepdims=True)
        gate = jnp.sum(jnp.where(onehot, ar_ref[pl.ds(e, 1), :], 0.0), axis=1, keepdims=True)
        packed = jnp.where(lane_o == 0, idx, jnp.where(lane_o == 1, gate, 0.0))
        if cap_pad > cap:
            packed = jnp.concatenate([packed, jnp.zeros((cap_pad - cap, lanes), F32)], axis=0)
        o_ref[e] = packed.T[:sub, :]
        return carry

    lax.fori_loop(0, e_n, invert, 0)


def moe_select(aff_row, n, n_sets, row_block_off):
    cap = EC_CAPACITY * n // N_EXPERTS
    cap_pad = max(cap, V7X_LANES)
    packed = pl.pallas_call(
        functools.partial(_select_kernel, n=n, cap=cap),
        grid=(n_sets,),
        in_specs=[pl.BlockSpec((N_EXPERTS, n), lambda b: (0, b + row_block_off))],
        out_specs=pl.BlockSpec((N_EXPERTS, V7X_SUBLANES, cap_pad), lambda b: (b, 0, 0)),
        out_shape=jax.ShapeDtypeStruct((n_sets * N_EXPERTS, V7X_SUBLANES, cap_pad), F32),
        scratch_shapes=[pltpu.VMEM((N_EXPERTS, n), F32)],
        compiler_params=_cparams(("parallel",)),
        name="moe_select",
    )(aff_row)
    return packed[:, :, :cap]


def _gather_kernel(idx_ref, f_ref, o_ref, buf, *, cap):
    def body(r, carry):
        buf[pl.ds(r, 1), :] = f_ref[pl.ds(idx_ref[0, 0, r], 1), :]
        return carry

    lax.fori_loop(0, cap, body, 0, unroll=8)
    o_ref[...] = buf[...].astype(o_ref.dtype)


def moe_gather(idx, f2d, n, n_blocks):
    d = f2d.shape[1]
    cap = idx.shape[-1]
    dt = 1024
    return pl.pallas_call(
        functools.partial(_gather_kernel, cap=cap),
        grid=(n_blocks, d // dt, N_EXPERTS),
        in_specs=[pl.BlockSpec((1, 1, cap), lambda b, c, e: (b * N_EXPERTS + e, 0, 0), memory_space=pltpu.SMEM),
                  pl.BlockSpec((n, dt), lambda b, c, e: (b, c))],
        out_specs=pl.BlockSpec((None, cap, dt), lambda b, c, e: (e, b, c)),
        out_shape=jax.ShapeDtypeStruct((N_EXPERTS, n_blocks * cap, d), BF16),
        scratch_shapes=[pltpu.VMEM((cap, dt), F32)],
        compiler_params=_cparams(("parallel", "parallel", "arbitrary")),
        name="moe_gather",
    )(idx, f2d)


def _glu_kernel(x_ref, wg_ref, wu_ref, o_ref):
    x = x_ref[...]
    hg = jnp.dot(x, wg_ref[...].astype(BF16), preferred_element_type=F32)
    hu = jnp.dot(x, wu_ref[...].astype(BF16), preferred_element_type=F32)
    o_ref[...] = (_silu(hg) * hu).astype(o_ref.dtype)


def moe_glu(xe, w_gate, w_up, layer):
    e_n, m, d = xe.shape
    f = w_gate.shape[-1]
    tn = 512
    return pl.pallas_call(
        _glu_kernel,
        grid=(e_n, f // tn),
        in_specs=[pl.BlockSpec((None, m, d), lambda e, j: (e, 0, 0)),
                  pl.BlockSpec((None, None, d, tn), lambda e, j: (layer, e, 0, j)),
                  pl.BlockSpec((None, None, d, tn), lambda e, j: (layer, e, 0, j))],
        out_specs=pl.BlockSpec((None, m, tn), lambda e, j: (e, 0, j)),
        out_shape=jax.ShapeDtypeStruct((e_n, m, f), BF16),
        compiler_params=_cparams(("parallel", "arbitrary")),
        name="moe_glu",
    )(xe, w_gate, w_up)


def _down_kernel(h_ref, wd_ref, gate_ref, g_ref, o_ref, *, row_groups):
    acc = jnp.dot(h_ref[...], wd_ref[...].astype(BF16), preferred_element_type=F32)
    reps = acc.shape[1] // V7X_LANES
    for start, size, mrow in row_groups:
        gate = jnp.concatenate([gate_ref[start:start + size, :]] * reps, axis=1)
        o_ref[start:start + size, :] = (acc[start:start + size, :] * gate) * g_ref[mrow]


def moe_down(hid, w_down, gate_col, mod3, k_gate, row_groups, layer):
    e_n, m, f = hid.shape
    d = w_down.shape[-1]
    tn = 512
    nj = d // tn
    return pl.pallas_call(
        functools.partial(_down_kernel, row_groups=row_groups),
        grid=(e_n, nj),
        in_specs=[pl.BlockSpec((None, m, f), lambda e, j: (e, 0, 0)),
                  pl.BlockSpec((None, None, f, tn), lambda e, j: (layer, e, 0, j)),
                  pl.BlockSpec((None, m, V7X_LANES), lambda e, j: (e, 0, 0)),
                  pl.BlockSpec((MOD_ROWS, 1, tn), lambda e, j: (0, 0, k_gate * nj + j))],
        out_specs=pl.BlockSpec((None, m, tn), lambda e, j: (e, 0, j)),
        out_shape=jax.ShapeDtypeStruct((e_n, m, d), F32),
        compiler_params=_cparams(("parallel", "arbitrary")),
        name="moe_down",
    )(hid, w_down, gate_col, mod3)


COMBINE_GROUP = 8


def _combine_kernel(idx_ref, ye_ref, h_ref, *rest, cap):
    o_ref = rest[-1]
    e = pl.program_id(2)

    @pl.when(e == 0)
    def _():
        o_ref[...] = h_ref[...]

    def body(g, carry):
        base = g * COMBINE_GROUP
        toks = [idx_ref[0, 0, base + u] for u in range(COMBINE_GROUP)]
        rows = [o_ref[pl.ds(t, 1), :] + ye_ref[pl.ds(base + u, 1), :] for u, t in enumerate(toks)]
        for t, row in zip(toks, rows):
            o_ref[pl.ds(t, 1), :] = row
        return carry

    lax.fori_loop(0, cap // COMBINE_GROUP, body, 0)


def moe_combine(idx, ye, x2d, n, n_blocks):
    d = x2d.shape[1]
    cap = idx.shape[-1]
    dt = 1024
    return pl.pallas_call(
        functools.partial(_combine_kernel, cap=cap),
        grid=(n_blocks, d // dt, N_EXPERTS),
        in_specs=[pl.BlockSpec((1, 1, cap), lambda b, c, e: (b * N_EXPERTS + e, 0, 0), memory_space=pltpu.SMEM),
                  pl.BlockSpec((None, cap, dt), lambda b, c, e: (e, b, c)),
                  pl.BlockSpec((n, dt), lambda b, c, e: (b, c))],
        out_specs=pl.BlockSpec((n, dt), lambda b, c, e: (b, c)),
        out_shape=jax.ShapeDtypeStruct((n_blocks * n, d), F32),
        compiler_params=_cparams(("parallel", "parallel", "arbitrary")),
        name="moe_combine",
    )(idx, ye, x2d)


def _split_select(packed, n_sets):
    cap = packed.shape[-1]
    idx = packed[:, 0, :].astype(jnp.int32).reshape(n_sets, N_EXPERTS, cap)
    gate = packed[:, 1, :].reshape(n_sets, N_EXPERTS, cap)
    return idx, gate


def ec_moe(x2d, norm_g, mod3, w_router, w_gate, w_up, w_down, layer, geom, with_ctx):
    b_n, e_n = geom.batch, N_EXPERTS
    rows = geom.rows if with_ctx else geom.r_lat
    f2d, aff_row = moe_router(x2d, norm_g, mod3, w_router, geom, 3, 4, rows)
    cap = EC_CAPACITY * geom.l_lat // e_n
    idx, gate = _split_select(moe_select(aff_row, geom.l_lat, b_n, 0), b_n)
    row_groups = [(b * cap, cap, b) for b in range(b_n)]
    if with_ctx:
        cap_c = EC_CAPACITY * geom.l_ctx // e_n
        assert b_n * cap_c == cap and geom.r_ctx == geom.l_lat
        idx_c, gate_c = _split_select(moe_select(aff_row, geom.l_ctx, b_n, geom.r_lat // geom.l_ctx), b_n)
        idx_c = idx_c + (jnp.arange(b_n, dtype=jnp.int32) * geom.l_ctx)[:, None, None]
        idx = jnp.concatenate([idx, idx_c.transpose(1, 0, 2).reshape(1, e_n, cap)], axis=0)
        gate = jnp.concatenate([gate, gate_c.transpose(1, 0, 2).reshape(1, e_n, cap)], axis=0)
        row_groups += [(b_n * cap + b * cap_c, cap_c, b_n) for b in range(b_n)]
    n_blocks = idx.shape[0]
    idx = idx.reshape(n_blocks * e_n, 1, cap)
    gate_col = jnp.broadcast_to(gate.transpose(1, 0, 2).reshape(e_n, n_blocks * cap, 1),
                                (e_n, n_blocks * cap, V7X_LANES))
    xe = moe_gather(idx, f2d, geom.l_lat, n_blocks)
    ye = moe_down(moe_glu(xe, w_gate, w_up, layer), w_down, gate_col, mod3, 5, tuple(row_groups), layer)
    return moe_combine(idx, ye, x2d, geom.l_lat, n_blocks)


def _final_norm_kernel(x_ref, g_ref, o_ref):
    x = x_ref[...]
    o_ref[...] = (x * lax.rsqrt(jnp.mean(x * x, axis=-1, keepdims=True) + EPS)) * g_ref[...]


def final_norm(x2d, g, rows, tl):
    d = x2d.shape[1]
    return pl.pallas_call(
        _final_norm_kernel,
        grid=(rows // tl,),
        in_specs=[pl.BlockSpec((tl, d), lambda i: (i, 0)), pl.BlockSpec((1, d), lambda i: (0, 0))],
        out_specs=pl.BlockSpec((tl, d), lambda i: (i, 0)),
        out_shape=jax.ShapeDtypeStruct((rows, d), F32),
        compiler_params=_cparams(("parallel",)),
        name="final_norm",
    )(x2d, g.reshape(1, d))


def _group_dt(dt, rows):
    g_n, e_n = SSD_GROUPS, SSD_HEADS_PER_GROUP
    dt_col = dt.reshape(rows, 2, g_n, e_n).transpose(2, 0, 1, 3).reshape(g_n, rows, 2 * e_n)
    dt_row = dt_col.reshape(g_n, rows // SSD_CHUNK, SSD_CHUNK, 2 * e_n).transpose(0, 1, 3, 2)
    return dt_col, dt_row


def ssd_layer(stream, mod3, norm_mix_g, w_in, conv_w, conv_b, dt_bias, a_log, d_skip, norm_g, w_out, geom, tm):
    a = norm_mod(stream, norm_mix_g, mod3, geom, 0, 1, geom.rows, BF16)
    n_main = SSD_D_INNER + SSD_XBC
    p = matmul(a, w_in, _pick_tile(geom.rows, IN_PROJ_ROWS), 512, 0, n_main)
    p_dt = matmul(a, w_in, tm, 2 * SSD_HEADS, n_main, 2 * SSD_HEADS)
    ctx_blk = geom.r_lat // geom.l_ctx
    xbc = ssd_conv(p, conv_w, conv_b, geom)
    cs, cd = ssd_dt(p_dt, dt_bias, a_log, tm)
    cs_col, cs_row = _group_dt(cs, geom.rows)
    _, cd_row = _group_dt(cd, geom.rows)
    zero = jnp.zeros((geom.batch * SSD_GROUPS, 2, SSD_D_STATE, SSD_GROUP_WIDTH), F32)
    y_ctx, s_ctx = ssd_scan(xbc, cs_col, cs_row, cd_row, d_skip, zero, geom.l_ctx, geom.batch, ctx_blk)
    y_lat, _ = ssd_scan(xbc, cs_col, cs_row, cd_row, d_skip, s_ctx, geom.l_lat, geom.batch, 0)
    yb = ssd_out(y_lat, y_ctx, p, norm_g, _pick_tile(geom.l_ctx, 256))
    return matmul_resid(yb, w_out, stream, mod3, geom, 2, _pick_tile(geom.r_ctx, OUT_PROJ_ROWS), OUT_PROJ_COLS)


def ret_layer(stream, mod3, norm_mix_g, w_in, decay, gn_g, w_out, geom, tm):
    a = norm_mod(stream, norm_mix_g, mod3, geom, 0, 1, geom.rows, BF16)
    p_lat = matmul(a, w_in, _pick_tile(geom.r_lat, IN_PROJ_ROWS), 512, 0, None, 0, geom.r_lat)
    p_ctx = matmul(a, w_in, _pick_tile(geom.r_ctx, IN_PROJ_ROWS), 512, D_MODEL, D_MODEL + RET_D_V,
                   geom.r_lat, geom.r_ctx)
    decay_b = jnp.broadcast_to(decay.T.reshape(RET_HEADS, 2, 1), (RET_HEADS, 2, V7X_LANES))
    cos_t, sin_t = rope_tables(geom.l_lat)
    s_ctx = ret_ctx_state(p_ctx, decay_b, geom)
    yb = ret_scan(p_lat, cos_t, sin_t, decay_b, gn_g, s_ctx, geom)
    return matmul_resid(yb, w_out, stream, mod3, geom, 2, _pick_tile(geom.r_ctx, OUT_PROJ_ROWS), OUT_PROJ_COLS)


def kernel(x, c, ctx, c_ctx, ada_w, ada_b, norm_mix_g, norm_ffn_g, ssd_w_in, ssd_conv_w, ssd_conv_b,
           ssd_dt_bias, ssd_a_log, ssd_d, ssd_norm_g, ssd_w_out, ret_w_in, ret_decay, ret_gn_g, ret_w_out,
           moe_w_router, moe_w_gate, moe_w_up, moe_w_down, final_norm_g):
    batch, l_lat, d = x.shape
    l_ctx = ctx.shape[1]
    depth = ada_w.shape[0]
    n_mixers = 2
    assert batch + 1 <= MOD_ROWS and d == D_MODEL
    assert depth == n_mixers, "the retention layer must be the last one: its context branch is state-only"
    geom = Geom(batch, l_lat, l_ctx)
    assert geom.r_ctx == l_lat, "all context sequences together must fill exactly one latent-sized row block"
    tm = _pick_tile(geom.r_ctx, 1024)
    stream = (x.reshape(geom.r_lat, d), ctx.reshape(geom.r_ctx, d))
    cc = jnp.zeros((MOD_ROWS, d), F32).at[:batch].set(c).at[batch].set(c_ctx)
    mod = ada_table(cc, ada_w, ada_b)
    for i in range(depth):
        need_ctx = i < depth - 1
        j = i // n_mixers
        mod3 = mod[i].reshape(MOD_ROWS, 1, N_MOD * d)
        if i % n_mixers == 0:
            x2d = ssd_layer(stream, mod3, norm_mix_g[i], ssd_w_in[j], ssd_conv_w[j], ssd_conv_b[j], ssd_dt_bias[j],
                            ssd_a_log[j], ssd_d[j], ssd_norm_g[j], ssd_w_out[j], geom, tm)
        else:
            x2d = ret_layer(stream, mod3, norm_mix_g[i], ret_w_in[j], ret_decay[j], ret_gn_g[j], ret_w_out[j],
                            geom, tm)
        x2d = ec_moe(x2d, norm_ffn_g[i], mod3, moe_w_router[i], moe_w_gate, moe_w_up, moe_w_down, i, geom,
                     need_ctx)
        stream = (x2d,)
    out = final_norm(x2d, final_norm_g, geom.r_lat, geom.row_tile(NORM_ROWS))
    return out.reshape(batch, l_lat, d)
```

```python
import functools
import math

import jax
import jax.numpy as jnp
from jax import lax
from jax.experimental import pallas as pl
from jax.experimental.pallas import tpu as pltpu

F32 = jnp.float32
BF16 = jnp.bfloat16
HIGHEST = lax.Precision.HIGHEST
LOG2_E = 1.4426950408889634

D_MODEL = 2048
N_MOD = 6
EPS = 1e-6

SSD_D_INNER = 2 * D_MODEL
SSD_HEAD_DIM = 64
SSD_HEADS = SSD_D_INNER // SSD_HEAD_DIM
SSD_GROUPS = 8
SSD_HEADS_PER_GROUP = SSD_HEADS // SSD_GROUPS
SSD_D_STATE = 128
SSD_CONV_W = 5
SSD_CHUNK = 128
SSD_BC = SSD_GROUPS * SSD_D_STATE
SSD_XBC = SSD_D_INNER + 2 * SSD_BC
SSD_IN = SSD_D_INNER + SSD_XBC + 2 * SSD_HEADS
SSD_GROUP_WIDTH = SSD_HEADS_PER_GROUP * SSD_HEAD_DIM

RET_HEADS = 8
RET_QK_DIM = D_MODEL // RET_HEADS
RET_V_DIM = 2 * RET_QK_DIM
RET_D_V = RET_HEADS * RET_V_DIM
RET_CHUNK = 128
RET_SCAN_CHUNK = 256
RET_IN = 2 * D_MODEL + 2 * RET_D_V
ROPE_BASE = 10000.0
GRID_W = 64

N_EXPERTS = 16
EC_CAPACITY = 2
D_EXPERT = D_MODEL

V7X_LANES = 128
V7X_SUBLANES = 8
V7X_VMEM_LIMIT_BYTES = 56 * 1024 * 1024
MOD_ROWS = 16
ROUTER_LANES = V7X_LANES
IN_PROJ_ROWS = 2048
OUT_PROJ_ROWS = 1024
OUT_PROJ_COLS = 512
NORM_ROWS = 1024
ROUTER_ROWS = 512


def _cparams(sem, vmem=V7X_VMEM_LIMIT_BYTES):
    return pltpu.CompilerParams(dimension_semantics=sem, vmem_limit_bytes=vmem)


def _silu(x):
    return x * jax.nn.sigmoid(x)


def _pick_tile(n, target):
    t = min(n, target)
    while n % t:
        t //= 2
    return t


def _ada_kernel(c_ref, w_ref, b_ref, o_ref):
    a = _silu(c_ref[...]).astype(BF16)
    o_ref[...] = jnp.dot(a, w_ref[...].astype(BF16), preferred_element_type=F32) + b_ref[...]


def ada_table(cc, ada_w, ada_b):
    depth, d, n = ada_w.shape
    tn = 1024
    return pl.pallas_call(
        _ada_kernel,
        grid=(depth, n // tn),
        in_specs=[pl.BlockSpec((MOD_ROWS, d), lambda i, j: (0, 0)),
                  pl.BlockSpec((None, d, tn), lambda i, j: (i, 0, j)),
                  pl.BlockSpec((None, 1, tn), lambda i, j: (i, 0, j))],
        out_specs=pl.BlockSpec((None, MOD_ROWS, tn), lambda i, j: (i, 0, j)),
        out_shape=jax.ShapeDtypeStruct((depth, MOD_ROWS, n), F32),
        compiler_params=_cparams(("parallel", "parallel")),
        name="ada_table",
    )(cc, ada_w, ada_b.reshape(depth, 1, n))


class Geom:
    def __init__(self, batch, l_lat, l_ctx):
        self.batch, self.l_lat, self.l_ctx = batch, l_lat, l_ctx
        self.r_lat = batch * l_lat
        self.r_ctx = batch * l_ctx
        self.rows = self.r_lat + self.r_ctx

    def row_tile(self, target):
        return _pick_tile(math.gcd(self.l_lat, self.r_ctx), target)

    def mod_row(self, i, tile):
        return jnp.where(i * tile < self.r_lat, (i * tile) // self.l_lat, self.batch)


def _stream_specs(parts, tile, cols, col_of):
    if len(parts) == 1:
        return [pl.BlockSpec((tile, cols), lambda i, *j: (i, col_of(*j)))]
    n0 = parts[0].shape[0] // tile
    return [pl.BlockSpec((tile, cols), lambda i, *j: (jnp.minimum(i, n0 - 1), jnp.where(i < n0, col_of(*j), 0))),
            pl.BlockSpec((tile, cols), lambda i, *j: (jnp.maximum(i - n0, 0), jnp.where(i < n0, 0, col_of(*j))))]


def _stream_tile(x_refs, n_lat_tiles):
    if len(x_refs) == 1:
        return x_refs[0][...]
    return jnp.where(pl.program_id(0) < n_lat_tiles, x_refs[0][...], x_refs[1][...])


def _norm_mod_kernel(*refs, n_lat_tiles):
    x_refs, (g_ref, sc_ref, sh_ref, o_ref) = refs[:-4], refs[-4:]
    x = _stream_tile(x_refs, n_lat_tiles)
    y = x * lax.rsqrt(jnp.mean(x * x, axis=-1, keepdims=True) + EPS)
    o_ref[...] = ((y * g_ref[...]) * (1.0 + sc_ref[...]) + sh_ref[...]).astype(o_ref.dtype)


def norm_mod(parts, g, mod3, geom, k_shift, k_scale, rows, out_dtype):
    d = parts[0].shape[1]
    tl = geom.row_tile(NORM_ROWS)
    return pl.pallas_call(
        functools.partial(_norm_mod_kernel, n_lat_tiles=geom.r_lat // tl),
        grid=(rows // tl,),
        in_specs=_stream_specs(parts, tl, d, lambda: 0) + [
            pl.BlockSpec((1, d), lambda i: (0, 0)),
            pl.BlockSpec((None, 1, d), lambda i: (geom.mod_row(i, tl), 0, k_scale)),
            pl.BlockSpec((None, 1, d), lambda i: (geom.mod_row(i, tl), 0, k_shift))],
        out_specs=pl.BlockSpec((tl, d), lambda i: (i, 0)),
        out_shape=jax.ShapeDtypeStruct((rows, d), out_dtype),
        compiler_params=_cparams(("parallel",)),
        name="norm_mod",
    )(*parts, g.reshape(1, d), mod3, mod3)


def _mm_kernel(a_ref, w_ref, o_ref):
    o_ref[...] = jnp.dot(a_ref[...], w_ref[...].astype(BF16), preferred_element_type=F32)


def matmul(a, w, tm, tn, col_start=0, n=None, row_start=0, m=None):
    k = a.shape[1]
    m = a.shape[0] - row_start if m is None else m
    n = w.shape[1] - col_start if n is None else n
    assert n % tn == 0 and col_start % tn == 0 and m % tm == 0 and row_start % tm == 0
    i0, j0 = row_start // tm, col_start // tn
    return pl.pallas_call(
        _mm_kernel,
        grid=(m // tm, n // tn),
        in_specs=[pl.BlockSpec((tm, k), lambda i, j: (i + i0, 0)),
                  pl.BlockSpec((k, tn), lambda i, j: (0, j + j0))],
        out_specs=pl.BlockSpec((tm, tn), lambda i, j: (i, j)),
        out_shape=jax.ShapeDtypeStruct((m, n), F32),
        compiler_params=_cparams(("parallel", "arbitrary")),
        name="matmul",
    )(a, w)


def _mm_resid_kernel(a_ref, w_ref, *refs, n_lat_tiles):
    r_refs, (g_ref, o_ref) = refs[:-2], refs[-2:]
    acc = jnp.dot(a_ref[...], w_ref[...].astype(BF16), preferred_element_type=F32)
    o_ref[...] = _stream_tile(r_refs, n_lat_tiles) + g_ref[...] * acc


def matmul_resid(a, w, res_parts, mod3, geom, k_gate, tm, tn):
    m, k = a.shape
    n = w.shape[1]
    nj = n // tn
    return pl.pallas_call(
        functools.partial(_mm_resid_kernel, n_lat_tiles=geom.r_lat // tm),
        grid=(m // tm, nj),
        in_specs=[pl.BlockSpec((tm, k), lambda i, j: (i, 0)),
                  pl.BlockSpec((k, tn), lambda i, j: (0, j))]
        + _stream_specs(res_parts, tm, tn, lambda j: j)
        + [pl.BlockSpec((None, 1, tn), lambda i, j: (geom.mod_row(i, tm), 0, k_gate * nj + j))],
        out_specs=pl.BlockSpec((tm, tn), lambda i, j: (i, j)),
        out_shape=jax.ShapeDtypeStruct((m, n), F32),
        compiler_params=_cparams(("parallel", "arbitrary")),
        name="matmul_resid",
    )(a, w, *res_parts, mod3)


def _conv_kernel(u_ref, w_ref, b_ref, o_ref, *, n_lat_blocks, l_ctx):
    u = u_ref[...]
    l, ct = u.shape
    pad = SSD_CONV_W // 2
    t = lax.broadcasted_iota(jnp.int32, (l, V7X_LANES), 0)
    is_ctx = pl.program_id(0) >= n_lat_blocks
    pos = jnp.where(is_ctx, t & (l_ctx - 1), t)
    seq_len = jnp.where(is_ctx, l_ctx, l)
    acc = u * w_ref[pad:pad + 1, :] + b_ref[...]
    for k in range(SSD_CONV_W):
        off = k - pad
        if off == 0:
            continue
        shifted = pltpu.roll(u, (-off) % l, axis=0)
        inside =jnp.where((pos + off >= 0) & (pos + off < seq_len), 1.0, 0.0)
        acc = acc + (shifted * jnp.concatenate([inside] * (ct // V7X_LANES), axis=1)) * w_ref[k:k + 1, :]
    o_ref[...] = _silu(acc)


def ssd_conv(p, conv_w, conv_b, geom):
    rows = p.shape[0]
    ct = 512
    col_off = SSD_D_INNER // ct
    assert geom.l_ctx & (geom.l_ctx - 1) == 0
    return pl.pallas_call(
        functools.partial(_conv_kernel, n_lat_blocks=geom.batch, l_ctx=geom.l_ctx),
        grid=(rows // geom.l_lat, SSD_XBC // ct),
        in_specs=[pl.BlockSpec((geom.l_lat, ct), lambda s, c: (s, c + col_off)),
                  pl.BlockSpec((SSD_CONV_W, ct), lambda s, c: (0, c)),
                  pl.BlockSpec((1, ct), lambda s, c: (0, c))],
        out_specs=pl.BlockSpec((geom.l_lat, ct), lambda s, c: (s, c)),
        out_shape=jax.ShapeDtypeStruct((rows, SSD_XBC), F32),
        compiler_params=_cparams(("parallel", "parallel")),
        name="ssd_conv",
    )(p, conv_w, conv_b.reshape(1, SSD_XBC))


def _dt_kernel(p_ref, b_ref, al_ref, cs_ref, cd_ref):
    q = SSD_CHUNK
    dt = jax.nn.softplus(p_ref[...] + b_ref[...])
    log2_dt = jnp.log2(dt)
    dta = dt * (-jnp.exp(al_ref[...]) * LOG2_E)
    row = lax.broadcasted_iota(jnp.int32, (q, q), 0)
    col = lax.broadcasted_iota(jnp.int32, (q, q), 1)
    prefix = (col <= row).astype(F32)
    suffix = (col >= row).astype(F32)
    fwd_cols = lax.broadcasted_iota(jnp.int32, (q, dt.shape[1]), 1) < SSD_HEADS
    for c in range(dt.shape[0] // q):
        blk = dta[c * q:(c + 1) * q, :]
        cs = jnp.where(fwd_cols,
                       jnp.dot(prefix, blk, precision=HIGHEST, preferred_element_type=F32),
                       jnp.dot(suffix, blk, precision=HIGHEST, preferred_element_type=F32))
        cs_ref[c * q:(c + 1) * q, :] = cs
        cd_ref[c * q:(c + 1) * q, :] = cs - log2_dt[c * q:(c + 1) * q, :]


def ssd_dt(p, dt_bias, a_log, tl):
    rows, w = p.shape
    assert tl % SSD_CHUNK == 0
    return pl.pallas_call(
        _dt_kernel,
        grid=(rows // tl,),
        in_specs=[pl.BlockSpec((tl, w), lambda i: (i, 0)),
                  pl.BlockSpec((1, w), lambda i: (0, 0)),
                  pl.BlockSpec((1, w), lambda i: (0, 0))],
        out_specs=[pl.BlockSpec((tl, w), lambda i: (i, 0)), pl.BlockSpec((tl, w), lambda i: (i, 0))],
        out_shape=[jax.ShapeDtypeStruct((rows, w), F32), jax.ShapeDtypeStruct((rows, w), F32)],
        compiler_params=_cparams(("parallel",)),
        name="ssd_dt",
    )(p, dt_bias.reshape(1, w), a_log.reshape(1, w))


def _ssd_scan_kernel(x_ref, b_ref, c_ref, csc_ref, csr_ref, cdr_ref, dsk_ref, s0_ref, *rest, nc):
    y_ref, sfin_ref, s_scr, yb_scr = rest[-4], rest[-3], rest[-2], rest[-1]
    q = SSD_CHUNK
    e_n = SSD_HEADS_PER_GROUP
    row = lax.broadcasted_iota(jnp.int32, (q, q), 0)
    col = lax.broadcasted_iota(jnp.int32, (q, q), 1)
    left = col < SSD_HEAD_DIM

    def chunk_step(d, c):
        if d == 0:
            incl, edge = col <= row, q - 1
        else:
            incl, edge = col >= row, 0
        t0 = pl.multiple_of(c * q, q)
        bm = b_ref[pl.ds(t0, q), :]
        cm = c_ref[pl.ds(t0, q), :]
        cs = csc_ref[0, pl.ds(t0, q), :][:, d * e_n:(d + 1) * e_n]
        tot_t = csr_ref[0, c][d * e_n:(d + 1) * e_n, :][:, edge:edge + 1]
        cd_t = cdr_ref[0, c][d * e_n:(d + 1) * e_n, :]
        w_t = jnp.exp2(tot_t - cd_t)
        et_b = jnp.broadcast_to(jnp.exp2(tot_t), (e_n, q))
        cmb = cm.astype(BF16)
        cb = lax.dot_general(cmb, bm.astype(BF16), (((1,), (1,)), ((), ())),
                             preferred_element_type=F32)
        bm_t = bm.T
        y_inter = jnp.dot(cmb, s_scr[d].astype(BF16), preferred_element_type=F32)
        for pr in range(e_n // 2):
            sl = slice(pr * 2 * SSD_HEAD_DIM, (pr + 1) * 2 * SSD_HEAD_DIM)
            mix, bw, ecs = [], [], []
            for e in (2 * pr, 2 * pr + 1):
                cs_b = jnp.broadcast_to(cs[:, e:e + 1], (q, q))
                seg = jnp.exp2(jnp.where(incl, cs_b - cd_t[e:e + 1, :], -jnp.inf))
                mix.append((cb * seg).astype(BF16))
                bw.append((bm_t * w_t[e:e + 1, :]).astype(BF16))
                ecs.append(jnp.exp2(cs_b))
            xs = x_ref[pl.ds(t0, q), sl]
            ss = s_scr[d, :, sl]
            lhs1 = jnp.concatenate([jnp.concatenate(mix, axis=1), jnp.concatenate(bw, axis=1)], axis=0)
            rhs1 = jnp.concatenate([jnp.where(left, xs, 0.0), jnp.where(left, 0.0, xs)], axis=0).astype(BF16)
            r1 = jnp.dot(lhs1, rhs1, preferred_element_type=F32)
            y = r1[:q] + y_inter[:, sl] * jnp.where(left, ecs[0], ecs[1])
            dec = jnp.where(left[:1], et_b[2 * pr:2 * pr + 1, :], et_b[2 * pr + 1:2 * pr + 2, :])
            s_scr[d, :, sl] = ss * dec + r1[q:]
            if d == 0:
                y_ref[pl.ds(t0, q), sl] = y + dsk_ref[:, sl] * xs
            else:
                yb_scr[pl.ds(t0, q), sl] = y

    s_scr[...] = s0_ref[0]

    def body(ci, carry):
        chunk_step(0, ci)
        chunk_step(1, nc - 1 - ci)
        return carry

    lax.fori_loop(0, nc, body, 0)
    sfin_ref[0] = s_scr[...]
    y_ref[...] = y_ref[...] + yb_scr[...]


def ssd_scan(xbc, cs_col, cs_row, cd_row, d_skip, s0, seq_len, n_seq, row_block_off):
    g_n, e_n, gw = SSD_GROUPS, SSD_HEADS_PER_GROUP, SSD_GROUP_WIDTH
    nc = seq_len // SSD_CHUNK
    b_off = SSD_D_INNER // SSD_D_STATE
    c_off = (SSD_D_INNER + SSD_BC) // SSD_D_STATE
    dsk = jnp.repeat(d_skip, SSD_HEAD_DIM).reshape(1, SSD_D_INNER)
    row_spec = pl.BlockSpec((1, nc, 2 * e_n, SSD_CHUNK), lambda s, g: (g, s + row_block_off, 0, 0))
    in_specs = [pl.BlockSpec((seq_len, gw), lambda s, g: (s + row_block_off, g)),
                pl.BlockSpec((seq_len, SSD_D_STATE), lambda s, g: (s + row_block_off, b_off + g)),
                pl.BlockSpec((seq_len, SSD_D_STATE), lambda s, g: (s + row_block_off, c_off + g)),
                pl.BlockSpec((1, seq_len, 2 * e_n), lambda s, g: (g, s + row_block_off, 0)),
                row_spec,
                row_spec,
                pl.BlockSpec((1, gw), lambda s, g: (0, g)),
                pl.BlockSpec((1, 2, SSD_D_STATE, gw), lambda s, g: (s * g_n + g, 0, 0, 0))]
    return pl.pallas_call(
        functools.partial(_ssd_scan_kernel, nc=nc),
        grid=(n_seq, g_n),
        in_specs=in_specs,
        out_specs=[pl.BlockSpec((seq_len, gw), lambda s, g: (s, g)),
                   pl.BlockSpec((1, 2, SSD_D_STATE, gw), lambda s, g: (s * g_n + g, 0, 0, 0))],
        out_shape=[jax.ShapeDtypeStruct((n_seq * seq_len, SSD_D_INNER), F32),
                   jax.ShapeDtypeStruct((n_seq * g_n, 2, SSD_D_STATE, gw), F32)],
        scratch_shapes=[pltpu.VMEM((2, SSD_D_STATE, gw), F32), pltpu.VMEM((seq_len, gw), F32)],
        compiler_params=_cparams(("parallel", "parallel")),
        name="ssd_scan",
    )(xbc, xbc, xbc, cs_col, cs_row, cd_row, dsk, s0)


def _ssd_out_kernel(yl_ref, yc_ref, z_ref, g_ref, o_ref, *, n_lat_tiles):
    is_ctx = pl.program_id(0) >= n_lat_tiles
    y = jnp.where(is_ctx, yc_ref[...], yl_ref[...])
    v = y * _silu(z_ref[...])
    n = v * lax.rsqrt(jnp.mean(v * v, axis=-1, keepdims=True) + EPS)
    o_ref[...] = (n * g_ref[...]).astype(o_ref.dtype)


def ssd_out(y_lat, y_ctx, p, norm_g, tl):
    n_lat, n_ctx = y_lat.shape[0] // tl, y_ctx.shape[0] // tl
    rows = y_lat.shape[0] + y_ctx.shape[0]
    return pl.pallas_call(
        functools.partial(_ssd_out_kernel, n_lat_tiles=n_lat),
        grid=(n_lat + n_ctx,),
        in_specs=[pl.BlockSpec((tl, SSD_D_INNER), lambda i: (jnp.minimum(i, n_lat - 1), 0)),
                  pl.BlockSpec((tl, SSD_D_INNER), lambda i: (jnp.maximum(i - n_lat, 0), 0)),
                  pl.BlockSpec((tl, SSD_D_INNER), lambda i: (i, 0)),
                  pl.BlockSpec((1, SSD_D_INNER), lambda i: (0, 0))],
        out_specs=pl.BlockSpec((tl, SSD_D_INNER), lambda i: (i, 0)),
        out_shape=jax.ShapeDtypeStruct((rows, SSD_D_INNER), BF16),
        compiler_params=_cparams(("parallel",)),
        name="ssd_out",
    )(y_lat, y_ctx, p, norm_g.reshape(1, SSD_D_INNER))


def _ret_state_kernel(k_ref, v_ref, ld_ref, s_ref):
    l = k_ref.shape[0]
    k = k_ref[...] * (RET_QK_DIM ** -0.5)
    vb = v_ref[...].astype(BF16)
    pos = lax.broadcasted_iota(jnp.int32, (l, V7X_LANES), 0).astype(F32)
    for d in range(2):
        ld = -jnp.exp(ld_ref[0, d:d + 1, :])
        steps = (l - 1.0 - pos) if d == 0 else pos
        w = jnp.exp(steps * ld)
        kd = (k * jnp.concatenate([w] * (RET_QK_DIM // V7X_LANES), axis=1)).astype(BF16)
        s_ref[0, d] = lax.dot_general(kd, vb, (((0,), (0,)), ((), ())), preferred_element_type=F32)


def ret_ctx_state(p, decay_b, geom):
    l = geom.l_ctx
    row_off = 0
    k_off = 0
    v_off = D_MODEL // RET_V_DIM
    return pl.pallas_call(
        _ret_state_kernel,
        grid=(geom.batch, RET_HEADS),
        in_specs=[pl.BlockSpec((l, RET_QK_DIM), lambda b, h: (b + row_off, k_off + h)),
                  pl.BlockSpec((l, RET_V_DIM), lambda b, h: (b + row_off, v_off + h)),
                  pl.BlockSpec((1, 2, V7X_LANES), lambda b, h: (h, 0, 0))],
        out_specs=pl.BlockSpec((1, 2, RET_QK_DIM, RET_V_DIM), lambda b, h: (b * RET_HEADS + h, 0, 0, 0)),
        out_shape=jax.ShapeDtypeStruct((geom.batch * RET_HEADS, 2, RET_QK_DIM, RET_V_DIM), F32),
        compiler_params=_cparams(("parallel", "parallel")),
        name="ret_ctx_state",
    )(p, p, decay_b)


def _rope(u, cos, sin):
    parts = []
    for j in range(u.shape[1] // V7X_LANES):
        s = u[:, j * V7X_LANES:(j + 1) * V7X_LANES]
        parts.append(pltpu.roll(s, V7X_LANES // 2, axis=1))
    return u * cos + jnp.concatenate(parts, axis=1) * sin


def _ret_scan_kernel(q_ref, k_ref, v_ref, g_ref, cos_ref, sin_ref, ld_ref, gn_ref, s0_ref, o_ref,
                     s_scr, qb_scr, sb_scr, acc_scr, *, nc):
    q = RET_SCAN_CHUNK
    nslab_k = RET_QK_DIM // V7X_LANES
    nslab_v = RET_V_DIM // V7X_LANES
    rel = (lax.broadcasted_iota(jnp.int32, (q, q), 0) - lax.broadcasted_iota(jnp.int32, (q, q), 1)).astype(F32)
    l_i = lax.broadcasted_iota(jnp.int32, (q, V7X_LANES), 0).astype(F32)
    ld_f = -jnp.exp(ld_ref[0, 0:1, :])
    ld_b = -jnp.exp(ld_ref[0, 1:2, :])
    decay_in = (jnp.exp(jnp.where(rel >= 0, rel * ld_f[:, :1], -jnp.inf))
                + jnp.exp(jnp.where(rel <= 0, -rel * ld_b[:, :1], -jnp.inf)))
    from_state = (jnp.exp((l_i + 1.0) * ld_f), jnp.exp((q - l_i) * ld_b))
    to_end = (jnp.exp((q - 1.0 - l_i) * ld_f), jnp.exp(l_i * ld_b))
    chunk_decay = (jnp.exp(q * ld_f), jnp.exp(q * ld_b))

    def roped_key(t0):
        return _rope(k_ref[pl.ds(t0, q), :], cos_ref[pl.ds(t0, q), :], sin_ref[pl.ds(t0, q), :]) * (RET_QK_DIM ** -0.5)

    def update_state(d, k, vb):
        kd = (k * jnp.concatenate([to_end[d]] * nslab_k, axis=1)).astype(BF16)
        s_scr[d] = (s_scr[d] * jnp.concatenate([chunk_decay[d]] * nslab_v, axis=1)
                    + lax.dot_general(kd, vb, (((0,), (0,)), ((), ())), preferred_element_type=F32))

    s_scr[...] = s0_ref[0]

    def scan(ci, carry):
        t0 = pl.multiple_of(ci * q, q)
        qb = _rope(q_ref[pl.ds(t0, q), :], cos_ref[pl.ds(t0, q), :], sin_ref[pl.ds(t0, q), :]).astype(BF16)
        k = roped_key(t0)
        vb = v_ref[pl.ds(t0, q), :].astype(BF16)
        qb_scr[pl.ds(t0, q), :] = qb
        scores = lax.dot_general(qb, k.astype(BF16), (((1,), (1,)), ((), ())),
                                 preferred_element_type=F32) * decay_in
        inter_f = (jnp.dot(qb, s_scr[0].astype(BF16), preferred_element_type=F32)
                   * jnp.concatenate([from_state[0]] * nslab_v, axis=1))
        acc_scr[pl.ds(t0, q), :] = jnp.dot(scores.astype(BF16), vb, preferred_element_type=F32) + inter_f
        update_state(0, k, vb)
        cb = nc - 1 - ci
        tb = pl.multiple_of(cb * q, q)
        sb_scr[cb] = s_scr[1].astype(BF16)
        update_state(1, roped_key(tb), v_ref[pl.ds(tb, q), :].astype(BF16))
        return carry

    lax.fori_loop(0, nc, scan, 0)

    def finish(c, carry):
        t0 = pl.multiple_of(c * q, q)
        o = acc_scr[pl.ds(t0, q), :] + (jnp.dot(qb_scr[pl.ds(t0, q), :], sb_scr[c], preferred_element_type=F32)
                                        * jnp.concatenate([from_state[1]] * nslab_v, axis=1))
        mu = jnp.mean(o, axis=-1, keepdims=True)
        var = jnp.mean(jnp.square(o - mu), axis=-1, keepdims=True)
        on = ((o - mu) * lax.rsqrt(var + EPS)) * gn_ref[...]
        o_ref[pl.ds(t0, q), :] = (on * _silu(g_ref[pl.ds(t0, q), :])).astype(o_ref.dtype)
        return carry

    lax.fori_loop(0, nc, finish, 0, unroll=2 if nc % 2 == 0 else 1)


def ret_scan(p, cos_t, sin_t, decay_b, gn_g, s0, geom):
    l = geom.l_lat
    assert l % RET_SCAN_CHUNK == 0
    nc = l // RET_SCAN_CHUNK
    k_off = D_MODEL // RET_QK_DIM
    v_off = 2 * D_MODEL // RET_V_DIM
    g_off = (2 * D_MODEL + RET_D_V) // RET_V_DIM
    return pl.pallas_call(
        functools.partial(_ret_scan_kernel, nc=nc),
        grid=(geom.batch, RET_HEADS),
        in_specs=[pl.BlockSpec((l, RET_QK_DIM), lambda b, h: (b, h)),
                  pl.BlockSpec((l, RET_QK_DIM), lambda b, h: (b, k_off + h)),
                  pl.BlockSpec((l, RET_V_DIM), lambda b, h: (b, v_off + h)),
                  pl.BlockSpec((l, RET_V_DIM), lambda b, h: (b, g_off + h)),
                  pl.BlockSpec((l, RET_QK_DIM), lambda b, h: (0, 0)),
                  pl.BlockSpec((l, RET_QK_DIM), lambda b, h: (0, 0)),
                  pl.BlockSpec((1, 2, V7X_LANES), lambda b, h: (h, 0, 0)),
                  pl.BlockSpec((1, RET_V_DIM), lambda b, h: (0, h)),
                  pl.BlockSpec((1, 2, RET_QK_DIM, RET_V_DIM), lambda b, h: (b * RET_HEADS + h, 0, 0, 0))],
        out_specs=pl.BlockSpec((l, RET_V_DIM), lambda b, h: (b, h)),
        out_shape=jax.ShapeDtypeStruct((geom.r_lat, RET_D_V), BF16),
        scratch_shapes=[pltpu.VMEM((2, RET_QK_DIM, RET_V_DIM), F32),
                        pltpu.VMEM((l, RET_QK_DIM), BF16),
                        pltpu.VMEM((nc, RET_QK_DIM, RET_V_DIM), BF16),
                        pltpu.VMEM((l, RET_V_DIM), F32)],
        compiler_params=_cparams(("parallel", "parallel")),
        name="ret_scan",
    )(p, p, p, p, cos_t, sin_t, decay_b, gn_g.reshape(1, RET_D_V), s0)


def rope_tables(l_lat):
    half = RET_QK_DIM // 4
    pos = jnp.arange(l_lat)
    freqs = ROPE_BASE ** (-jnp.arange(half, dtype=F32) / half)
    cs, sn = [], []
    for ids in (pos // GRID_W, pos % GRID_W):
        ang = ids.astype(F32)[:, None] * freqs[None, :]
        cs += [jnp.cos(ang), jnp.cos(ang)]
        sn += [-jnp.sin(ang), jnp.sin(ang)]
    return jnp.concatenate(cs, axis=1), jnp.concatenate(sn, axis=1)


def _router_kernel(x_ref, g_ref, sc_ref, sh_ref, wr_ref, f_ref, ar_ref):
    x = x_ref[...]
    y = x * lax.rsqrt(jnp.mean(x * x, axis=-1, keepdims=True) + EPS)
    f = (y * g_ref[...]) * (1.0 + sc_ref[...]) + sh_ref[...]
    f_ref[...] = f
    logits = jnp.dot(f, wr_ref[...], precision=HIGHEST, preferred_element_type=F32)
    lane = lax.broadcasted_iota(jnp.int32, logits.shape, 1)
    logits = jnp.where(lane < N_EXPERTS, logits, -jnp.inf)
    un = jnp.exp(logits - jnp.max(logits, axis=-1, keepdims=True))
    aff = un / jnp.sum(un, axis=-1, keepdims=True)
    ar_ref[...] = aff.T[:N_EXPERTS, :]


def moe_router(x2d, g, mod3, w_router, geom, k_shift, k_scale, rows):
    d = x2d.shape[1]
    tl = geom.row_tile(ROUTER_ROWS)
    wr = jnp.pad(w_router, ((0, 0), (0, ROUTER_LANES - N_EXPERTS)))
    return pl.pallas_call(
        _router_kernel,
        grid=(rows // tl,),
        in_specs=[pl.BlockSpec((tl, d), lambda i: (i, 0)),
                  pl.BlockSpec((1, d), lambda i: (0, 0)),
                  pl.BlockSpec((None, 1, d), lambda i: (geom.mod_row(i, tl), 0, k_scale)),
                  pl.BlockSpec((None, 1, d), lambda i: (geom.mod_row(i, tl), 0, k_shift)),
                  pl.BlockSpec((d, ROUTER_LANES), lambda i: (0, 0))],
        out_specs=[pl.BlockSpec((tl, d), lambda i: (i, 0)),
                   pl.BlockSpec((N_EXPERTS, tl), lambda i: (0, i))],
        out_shape=[jax.ShapeDtypeStruct((rows, d), F32),
                   jax.ShapeDtypeStruct((N_EXPERTS, rows), F32)],
        compiler_params=_cparams(("parallel",)),
        name="moe_router",
    )(x2d, g.reshape(1, d), mod3, mod3, wr)


def _select_kernel(ar_ref, o_ref, slot_scr, *, n, cap):
    e_n, lanes, sub = N_EXPERTS, V7X_LANES, V7X_SUBLANES
    a = ar_ref[...]
    bits = pltpu.bitcast(a, jnp.int32)

    def bisect(i, thr):
        cand = thr | lax.shift_left(jnp.int32(1), 30 - i)
        cnt = jnp.sum(jnp.where(bits >= cand, 1.0, 0.0), axis=1, keepdims=True)
        return jnp.where(cnt >= cap, cand, thr)

    thr = lax.fori_loop(0, 31, bisect, jnp.zeros((e_n, 1), jnp.int32))
    gt = jnp.where(bits > thr, 1.0, 0.0)
    eq = jnp.where(bits == thr, 1.0, 0.0)
    need = cap - jnp.sum(gt, axis=1, keepdims=True)
    upper = (lax.broadcasted_iota(jnp.int32, (lanes, lanes), 0)
             <= lax.broadcasted_iota(jnp.int32, (lanes, lanes), 1)).astype(BF16)
    ties_before = jnp.zeros((e_n, 1), F32)
    taken_before = jnp.zeros((e_n, 1), F32)
    for j in range(n // lanes):
        sl = slice(j * lanes, (j + 1) * lanes)
        eq_j = eq[:, sl]
        tie_incl = jnp.dot(eq_j.astype(BF16), upper, preferred_element_type=F32)
        tie_pos = ties_before + tie_incl - eq_j
        ties_before = ties_before + tie_incl[:, lanes - 1:lanes]
        sel_j = gt[:, sl] + eq_j * jnp.where(tie_pos < need, 1.0, 0.0)
        sel_incl = jnp.dot(sel_j.astype(BF16), upper, preferred_element_type=F32)
        slot_scr[:, sl] = jnp.where(sel_j > 0.0, taken_before + sel_incl - sel_j, -1.0)
        taken_before = taken_before + sel_incl[:, lanes - 1:lanes]

    r_col = lax.broadcasted_iota(jnp.int32, (cap, n), 0).astype(F32)
    t_row = lax.broadcasted_iota(jnp.int32, (cap, n), 1).astype(F32)
    lane_o = lax.broadcasted_iota(jnp.int32, (cap, lanes), 1)
    cap_pad = o_ref.shape[-1]

    def invert(e, carry):
        onehot = slot_scr[pl.ds(e, 1), :] == r_col
        idx = jnp.sum(jnp.where(onehot, t_row, 0.0), axis=1, keepdims=True)
        gate = jnp.sum(jnp.where(onehot, ar_ref[pl.ds(e, 1), :], 0.0), axis=1, keepdims=True)
        packed = jnp.where(lane_o == 0, idx, jnp.where(lane_o == 1, gate, 0.0))
        if cap_pad > cap:
            packed = jnp.concatenate([packed, jnp.zeros((cap_pad - cap, lanes), F32)], axis=0)
        o_ref[e] = packed.T[:sub, :]
        return carry

    lax.fori_loop(0, e_n, invert, 0)


def moe_select(aff_row, n, n_sets, row_block_off):
    cap = EC_CAPACITY * n // N_EXPERTS
    cap_pad = max(cap, V7X_LANES)
    packed = pl.pallas_call(
        functools.partial(_select_kernel, n=n, cap=cap),
        grid=(n_sets,),
        in_specs=[pl.BlockSpec((N_EXPERTS, n), lambda b: (0, b + row_block_off))],
        out_specs=pl.BlockSpec((N_EXPERTS, V7X_SUBLANES, cap_pad), lambda b: (b, 0, 0)),
        out_shape=jax.ShapeDtypeStruct((n_sets * N_EXPERTS, V7X_SUBLANES, cap_pad), F32),
        scratch_shapes=[pltpu.VMEM((N_EXPERTS, n), F32)],
        compiler_params=_cparams(("parallel",)),
        name="moe_select",
    )(aff_row)
    return packed[:, :, :cap]


def _gather_kernel(idx_ref, f_ref, o_ref, buf, *, cap):
    def body(r, carry):
        buf[pl.ds(r, 1), :] = f_ref[pl.ds(idx_ref[0, 0, r], 1), :]
        return carry

    lax.fori_loop(0, cap, body, 0, unroll=8)
    o_ref[...] = buf[...].astype(o_ref.dtype)


def moe_gather(idx, f2d, n, n_blocks):
    d = f2d.shape[1]
    cap = idx.shape[-1]
    dt = d
    return pl.pallas_call(
        functools.partial(_gather_kernel, cap=cap),
        grid=(n_blocks, d // dt, N_EXPERTS),
        in_specs=[pl.BlockSpec((1, 1, cap), lambda b, c, e: (b * N_EXPERTS + e, 0, 0), memory_space=pltpu.SMEM),
                  pl.BlockSpec((n, dt), lambda b, c, e: (b, c))],
        out_specs=pl.BlockSpec((None, cap, dt), lambda b, c, e: (e, b, c)),
        out_shape=jax.ShapeDtypeStruct((N_EXPERTS, n_blocks * cap, d), BF16),
        scratch_shapes=[pltpu.VMEM((cap, dt), F32)],
        compiler_params=_cparams(("parallel", "parallel", "arbitrary")),
        name="moe_gather",
    )(idx, f2d)


def _glu_kernel(x_ref, wg_ref, wu_ref, o_ref):
    x = x_ref[...]
    hg = jnp.dot(x, wg_ref[...].astype(BF16), preferred_element_type=F32)
    hu = jnp.dot(x, wu_ref[...].astype(BF16), preferred_element_type=F32)
    o_ref[...] = (_silu(hg) * hu).astype(o_ref.dtype)


def moe_glu(xe, w_gate, w_up, layer):
    e_n, m, d = xe.shape
    f = w_gate.shape[-1]
    tn = 512
    return pl.pallas_call(
        _glu_kernel,
        grid=(e_n, f // tn),
        in_specs=[pl.BlockSpec((None, m, d), lambda e, j: (e, 0, 0)),
                  pl.BlockSpec((None, None, d, tn), lambda e, j: (layer, e, 0, j)),
                  pl.BlockSpec((None, None, d, tn), lambda e, j: (layer, e, 0, j))],
        out_specs=pl.BlockSpec((None, m, tn), lambda e, j: (e, 0, j)),
        out_shape=jax.ShapeDtypeStruct((e_n, m, f), BF16),
        compiler_params=_cparams(("parallel", "arbitrary")),
        name="moe_glu",
    )(xe, w_gate, w_up)


def _down_kernel(h_ref, wd_ref, gate_ref, g_ref, o_ref, *, row_groups):
    acc = jnp.dot(h_ref[...], wd_ref[...].astype(BF16), preferred_element_type=F32)
    reps = acc.shape[1] // V7X_LANES
    for start, size, mrow in row_groups:
        gate = jnp.concatenate([gate_ref[start:start + size, :]] * reps, axis=1)
        o_ref[start:start + size, :] = (acc[start:start + size, :] * gate) * g_ref[mrow]


def moe_down(hid, w_down, gate_col, mod3, k_gate, row_groups, layer):
    e_n, m, f = hid.shape
    d = w_down.shape[-1]
    tn = 512
    nj = d // tn
    return pl.pallas_call(
        functools.partial(_down_kernel, row_groups=row_groups),
        grid=(e_n, nj),
        in_specs=[pl.BlockSpec((None, m, f), lambda e, j: (e, 0, 0)),
                  pl.BlockSpec((None, None, f, tn), lambda e, j: (layer, e, 0, j)),
                  pl.BlockSpec((None, m, V7X_LANES), lambda e, j: (e, 0, 0)),
                  pl.BlockSpec((MOD_ROWS, 1, tn), lambda e, j: (0, 0, k_gate * nj + j))],
        out_specs=pl.BlockSpec((None, m, tn), lambda e, j: (e, 0, j)),
        out_shape=jax.ShapeDtypeStruct((e_n, m, d), F32),
        compiler_params=_cparams(("parallel", "arbitrary")),
        name="moe_down",
    )(hid, w_down, gate_col, mod3)


COMBINE_GROUP = 8


def _combine_kernel(idx_ref, ye_ref, h_hbm, o_ref, sem, *, cap, n):
    b, e = pl.program_id(0), pl.program_id(1)

    @pl.when(e == 0)
    def _():
        copy = pltpu.make_async_copy(h_hbm.at[pl.ds(pl.multiple_of(b * n, n), n), :], o_ref, sem)
        copy.start()
        copy.wait()

    def body(g, carry):
        base = g * COMBINE_GROUP
        toks = [idx_ref[0, 0, base + u] for u in range(COMBINE_GROUP)]
        rows = [o_ref[pl.ds(t, 1), :] + ye_ref[pl.ds(base + u, 1), :] for u, t in enumerate(toks)]
        for t, row in zip(toks, rows):
            o_ref[pl.ds(t, 1), :] = row
        return carry

    lax.fori_loop(0, cap // COMBINE_GROUP, body, 0)


def moe_combine(idx, ye, x2d, n, n_blocks):
    d = x2d.shape[1]
    cap = idx.shape[-1]
    return pl.pallas_call(
        functools.partial(_combine_kernel, cap=cap, n=n),
        grid=(n_blocks, N_EXPERTS),
        in_specs=[pl.BlockSpec((1, 1, cap), lambda b, e: (b * N_EXPERTS + e, 0, 0), memory_space=pltpu.SMEM),
                  pl.BlockSpec((None, cap, d), lambda b, e: (e, b, 0)),
                  pl.BlockSpec(memory_space=pl.ANY)],
        out_specs=pl.BlockSpec((n, d), lambda b, e: (b, 0)),
        out_shape=jax.ShapeDtypeStruct((n_blocks * n, d), F32),
        scratch_shapes=[pltpu.SemaphoreType.DMA(())],
        compiler_params=_cparams(("parallel", "arbitrary")),
        name="moe_combine",
    )(idx, ye, x2d)


def _split_select(packed, n_sets):
    cap = packed.shape[-1]
    idx = packed[:, 0, :].astype(jnp.int32).reshape(n_sets, N_EXPERTS, cap)
    gate = packed[:, 1, :].reshape(n_sets, N_EXPERTS, cap)
    return idx, gate


def ec_moe(x2d, norm_g, mod3, w_router, w_gate, w_up, w_down, layer, geom, with_ctx):
    b_n, e_n = geom.batch, N_EXPERTS
    rows = geom.rows if with_ctx else geom.r_lat
    f2d, aff_row = moe_router(x2d, norm_g, mod3, w_router, geom, 3, 4, rows)
    cap = EC_CAPACITY * geom.l_lat // e_n
    idx, gate = _split_select(moe_select(aff_row, geom.l_lat, b_n, 0), b_n)
    row_groups = [(b * cap, cap, b) for b in range(b_n)]
    if with_ctx:
        cap_c = EC_CAPACITY * geom.l_ctx // e_n
        assert b_n * cap_c == cap and geom.r_ctx == geom.l_lat
        idx_c, gate_c = _split_select(moe_select(aff_row, geom.l_ctx, b_n, geom.r_lat // geom.l_ctx), b_n)
        idx_c = idx_c + (jnp.arange(b_n, dtype=jnp.int32) * geom.l_ctx)[:, None, None]
        idx = jnp.concatenate([idx, idx_c.transpose(1, 0, 2).reshape(1, e_n, cap)], axis=0)
        gate = jnp.concatenate([gate, gate_c.transpose(1, 0, 2).reshape(1, e_n, cap)], axis=0)
        row_groups += [(b_n * cap + b * cap_c, cap_c, b_n) for b in range(b_n)]
    n_blocks = idx.shape[0]
    idx = idx.reshape(n_blocks * e_n, 1, cap)
    gate_col = jnp.broadcast_to(gate.transpose(1, 0, 2).reshape(e_n, n_blocks * cap, 1),
                                (e_n, n_blocks * cap, V7X_LANES))
    xe = moe_gather(idx, f2d, geom.l_lat, n_blocks)
    ye = moe_down(moe_glu(xe, w_gate, w_up, layer), w_down, gate_col, mod3, 5, tuple(row_groups), layer)
    return moe_combine(idx, ye, x2d, geom.l_lat, n_blocks)


def _final_norm_kernel(x_ref, g_ref, o_ref):
    x = x_ref[...]
    o_ref[...] = (x * lax.rsqrt(jnp.mean(x * x, axis=-1, keepdims=True) + EPS)) * g_ref[...]


def final_norm(x2d, g, rows, tl):
    d = x2d.shape[1]
    return pl.pallas_call(
        _final_norm_kernel,
        grid=(rows // tl,),
        in_specs=[pl.BlockSpec((tl, d), lambda i: (i, 0)), pl.BlockSpec((1, d), lambda i: (0, 0))],
        out_specs=pl.BlockSpec((tl, d), lambda i: (i, 0)),
        out_shape=jax.ShapeDtypeStruct((rows, d), F32),
        compiler_params=_cparams(("parallel",)),
        name="final_norm",
    )(x2d, g.reshape(1, d))


def _group_dt(dt, rows):
    g_n, e_n = SSD_GROUPS, SSD_HEADS_PER_GROUP
    dt_col = dt.reshape(rows, 2, g_n, e_n).transpose(2, 0, 1, 3).reshape(g_n, rows, 2 * e_n)
    dt_row = dt_col.reshape(g_n, rows // SSD_CHUNK, SSD_CHUNK, 2 * e_n).transpose(0, 1, 3, 2)
    return dt_col, dt_row


def ssd_layer(stream, mod3, norm_mix_g, w_in, conv_w, conv_b, dt_bias, a_log, d_skip, norm_g, w_out, geom, tm):
    a = norm_mod(stream, norm_mix_g, mod3, geom, 0, 1, geom.rows, BF16)
    n_main = SSD_D_INNER + SSD_XBC
    p = matmul(a, w_in, _pick_tile(geom.rows, IN_PROJ_ROWS), 512, 0, n_main)
    p_dt = matmul(a, w_in, tm, 2 * SSD_HEADS, n_main, 2 * SSD_HEADS)
    ctx_blk = geom.r_lat // geom.l_ctx
    xbc = ssd_conv(p, conv_w, conv_b, geom)
    cs, cd = ssd_dt(p_dt, dt_bias, a_log, tm)
    cs_col, cs_row = _group_dt(cs, geom.rows)
    _, cd_row = _group_dt(cd, geom.rows)
    zero = jnp.zeros((geom.batch * SSD_GROUPS, 2, SSD_D_STATE, SSD_GROUP_WIDTH), F32)
    y_ctx, s_ctx = ssd_scan(xbc, cs_col, cs_row, cd_row, d_skip, zero, geom.l_ctx, geom.batch, ctx_blk)
    y_lat, _ = ssd_scan(xbc, cs_col, cs_row, cd_row, d_skip, s_ctx, geom.l_lat, geom.batch, 0)
    yb = ssd_out(y_lat, y_ctx, p, norm_g, _pick_tile(geom.l_ctx, 256))
    return matmul_resid(yb, w_out, stream, mod3, geom, 2, _pick_tile(geom.r_ctx, OUT_PROJ_ROWS), OUT_PROJ_COLS)


def ret_layer(stream, mod3, norm_mix_g, w_in, decay, gn_g, w_out, geom, tm):
    a = norm_mod(stream, norm_mix_g, mod3, geom, 0, 1, geom.rows, BF16)
    p_lat = matmul(a, w_in, _pick_tile(geom.r_lat, IN_PROJ_ROWS), 512, 0, None, 0, geom.r_lat)
    p_ctx = matmul(a, w_in, _pick_tile(geom.r_ctx, IN_PROJ_ROWS), 512, D_MODEL, D_MODEL + RET_D_V,
                   geom.r_lat, geom.r_ctx)
    decay_b = jnp.broadcast_to(decay.T.reshape(RET_HEADS, 2, 1), (RET_HEADS, 2, V7X_LANES))
    cos_t, sin_t = rope_tables(geom.l_lat)
    s_ctx = ret_ctx_state(p_ctx, decay_b, geom)
    yb = ret_scan(p_lat, cos_t, sin_t, decay_b, gn_g, s_ctx, geom)
    return matmul_resid(yb, w_out, stream, mod3, geom, 2, _pick_tile(geom.r_ctx, OUT_PROJ_ROWS), OUT_PROJ_COLS)


def kernel(x, c, ctx, c_ctx, ada_w, ada_b, norm_mix_g, norm_ffn_g, ssd_w_in, ssd_conv_w, ssd_conv_b,
           ssd_dt_bias, ssd_a_log, ssd_d, ssd_norm_g, ssd_w_out, ret_w_in, ret_decay, ret_gn_g, ret_w_out,
           moe_w_router, moe_w_gate, moe_w_up, moe_w_down, final_norm_g):
    batch, l_lat, d = x.shape
    l_ctx = ctx.shape[1]
    depth = ada_w.shape[0]
    n_mixers = 2
    assert batch + 1 <= MOD_ROWS and d == D_MODEL
    assert depth == n_mixers, "the retention layer must be the last one: its context branch is state-only"
    geom = Geom(batch, l_lat, l_ctx)
    assert geom.r_ctx == l_lat, "all context sequences together must fill exactly one latent-sized row block"
    tm = _pick_tile(geom.r_ctx, 1024)
    stream = (x.reshape(geom.r_lat, d), ctx.reshape(geom.r_ctx, d))
    cc = jnp.zeros((MOD_ROWS, d), F32).at[:batch].set(c).at[batch].set(c_ctx)
    mod = ada_table(cc, ada_w, ada_b)
    for i in range(depth):
        need_ctx = i < depth - 1
        j = i // n_mixers
        mod3 = mod[i].reshape(MOD_ROWS, 1, N_MOD * d)
        if i % n_mixers == 0:
            x2d = ssd_layer(stream, mod3, norm_mix_g[i], ssd_w_in[j], ssd_conv_w[j], ssd_conv_b[j], ssd_dt_bias[j],
                            ssd_a_log[j], ssd_d[j], ssd_norm_g[j], ssd_w_out[j], geom, tm)
        else:
            x2d = ret_layer(stream, mod3, norm_mix_g[i], ret_w_in[j], ret_decay[j], ret_gn_g[j], ret_w_out[j],
                            geom, tm)
        x2d = ec_moe(x2d, norm_ffn_g[i], mod3, moe_w_router[i], moe_w_gate, moe_w_up, moe_w_down, i, geom,
                     need_ctx)
        stream = (x2d,)
    out = final_norm(x2d, final_norm_g, geom.r_lat, geom.row_tile(NORM_ROWS))
    return out.reshape(batch, l_lat, d)
```

```python
import functools
import math

import jax
import jax.numpy as jnp
from jax import lax
from jax.experimental import pallas as pl
from jax.experimental.pallas import tpu as pltpu

F32 = jnp.float32
BF16 = jnp.bfloat16
HIGHEST = lax.Precision.HIGHEST
LOG2_E = 1.4426950408889634

D_MODEL = 2048
N_MOD = 6
EPS = 1e-6

SSD_D_INNER = 2 * D_MODEL
SSD_HEAD_DIM = 64
SSD_HEADS = SSD_D_INNER // SSD_HEAD_DIM
SSD_GROUPS = 8
SSD_HEADS_PER_GROUP = SSD_HEADS // SSD_GROUPS
SSD_D_STATE = 128
SSD_CONV_W = 5
SSD_CHUNK = 128
SSD_BC = SSD_GROUPS * SSD_D_STATE
SSD_XBC = SSD_D_INNER + 2 * SSD_BC
SSD_IN = SSD_D_INNER + SSD_XBC + 2 * SSD_HEADS
SSD_GROUP_WIDTH = SSD_HEADS_PER_GROUP * SSD_HEAD_DIM

RET_HEADS = 8
RET_QK_DIM = D_MODEL // RET_HEADS
RET_V_DIM = 2 * RET_QK_DIM
RET_D_V = RET_HEADS * RET_V_DIM
RET_CHUNK = 128
RET_SCAN_CHUNK = 256
RET_IN = 2 * D_MODEL + 2 * RET_D_V
ROPE_BASE = 10000.0
GRID_W = 64

N_EXPERTS = 16
EC_CAPACITY = 2
D_EXPERT = D_MODEL

V7X_LANES = 128
V7X_SUBLANES = 8
V7X_VMEM_LIMIT_BYTES = 56 * 1024 * 1024
MOD_ROWS = 16
ROUTER_LANES = V7X_LANES
IN_PROJ_ROWS = 2048
OUT_PROJ_ROWS = 1024
OUT_PROJ_COLS = 512
NORM_ROWS = 1024
ROUTER_ROWS = 512


def _cparams(sem, vmem=V7X_VMEM_LIMIT_BYTES):
    return pltpu.CompilerParams(dimension_semantics=sem, vmem_limit_bytes=vmem)


def _silu(x):
    return x * jax.nn.sigmoid(x)


def _pick_tile(n, target):
    t = min(n, target)
    while n % t:
        t //= 2
    return t


def _ada_kernel(c_ref, w_ref, b_ref, o_ref):
    a = _silu(c_ref[...]).astype(BF16)
    o_ref[...] = jnp.dot(a, w_ref[...].astype(BF16), preferred_element_type=F32) + b_ref[...]


def ada_table(cc, ada_w, ada_b):
    depth, d, n = ada_w.shape
    tn = 1024
    return pl.pallas_call(
        _ada_kernel,
        grid=(depth, n // tn),
        in_specs=[pl.BlockSpec((MOD_ROWS, d), lambda i, j: (0, 0)),
                  pl.BlockSpec((None, d, tn), lambda i, j: (i, 0, j)),
                  pl.BlockSpec((None, 1, tn), lambda i, j: (i, 0, j))],
        out_specs=pl.BlockSpec((None, MOD_ROWS, tn), lambda i, j: (i, 0, j)),
        out_shape=jax.ShapeDtypeStruct((depth, MOD_ROWS, n), F32),
        compiler_params=_cparams(("parallel", "parallel")),
        name="ada_table",
    )(cc, ada_w, ada_b.reshape(depth, 1, n))


class Geom:
    def __init__(self, batch, l_lat, l_ctx):
        self.batch, self.l_lat, self.l_ctx = batch, l_lat, l_ctx
        self.r_lat = batch * l_lat
        self.r_ctx = batch * l_ctx
        self.rows = self.r_lat + self.r_ctx

    def row_tile(self, target):
        return _pick_tile(math.gcd(self.l_lat, self.r_ctx), target)

    def mod_row(self, i, tile):
        return jnp.where(i * tile < self.r_lat, (i * tile) // self.l_lat, self.batch)


def _stream_specs(parts, tile, cols, col_of):
    if len(parts) == 1:
        return [pl.BlockSpec((tile, cols), lambda i, *j: (i, col_of(*j)))]
    n0 = parts[0].shape[0] // tile
    return [pl.BlockSpec((tile, cols), lambda i, *j: (jnp.minimum(i, n0 - 1), jnp.where(i < n0, col_of(*j), 0))),
            pl.BlockSpec((tile, cols), lambda i, *j: (jnp.maximum(i - n0, 0), jnp.where(i < n0, 0, col_of(*j))))]


def _stream_tile(x_refs, n_lat_tiles):
    if len(x_refs) == 1:
        return x_refs[0][...]
    return jnp.where(pl.program_id(0) < n_lat_tiles, x_refs[0][...], x_refs[1][...])


def _norm_mod_kernel(*refs, n_lat_tiles):
    x_refs, (g_ref, sc_ref, sh_ref, o_ref) = refs[:-4], refs[-4:]
    x = _stream_tile(x_refs, n_lat_tiles)
    y = x * lax.rsqrt(jnp.mean(x * x, axis=-1, keepdims=True) + EPS)
    o_ref[...] = ((y * g_ref[...]) * (1.0 + sc_ref[...]) + sh_ref[...]).astype(o_ref.dtype)


def norm_mod(parts, g, mod3, geom, k_shift, k_scale, rows, out_dtype):
    d = parts[0].shape[1]
    tl = geom.row_tile(NORM_ROWS)
    return pl.pallas_call(
        functools.partial(_norm_mod_kernel, n_lat_tiles=geom.r_lat // tl),
        grid=(rows // tl,),
        in_specs=_stream_specs(parts, tl, d, lambda: 0) + [
            pl.BlockSpec((1, d), lambda i: (0, 0)),
            pl.BlockSpec((None, 1, d), lambda i: (geom.mod_row(i, tl), 0, k_scale)),
            pl.BlockSpec((None, 1, d), lambda i: (geom.mod_row(i, tl), 0, k_shift))],
        out_specs=pl.BlockSpec((tl, d), lambda i: (i, 0)),
        out_shape=jax.ShapeDtypeStruct((rows, d), out_dtype),
        compiler_params=_cparams(("parallel",)),
        name="norm_mod",
    )(*parts, g.reshape(1, d), mod3, mod3)


def _mm_kernel(a_ref, w_ref, o_ref):
    o_ref[...] = jnp.dot(a_ref[...], w_ref[...].astype(BF16), preferred_element_type=F32)


def matmul(a, w, tm, tn, col_start=0, n=None, row_start=0, m=None):
    k = a.shape[1]
    m = a.shape[0] - row_start if m is None else m
    n = w.shape[1] - col_start if n is None else n
    assert n % tn == 0 and col_start % tn == 0 and m % tm == 0 and row_start % tm == 0
    i0, j0 = row_start // tm, col_start // tn
    return pl.pallas_call(
        _mm_kernel,
        grid=(m // tm, n // tn),
        in_specs=[pl.BlockSpec((tm, k), lambda i, j: (i + i0, 0)),
                  pl.BlockSpec((k, tn), lambda i, j: (0, j + j0))],
        out_specs=pl.BlockSpec((tm, tn), lambda i, j: (i, j)),
        out_shape=jax.ShapeDtypeStruct((m, n), F32),
        compiler_params=_cparams(("parallel", "arbitrary")),
        name="matmul",
    )(a, w)


def _mm_resid_kernel(a_ref, w_ref, *refs, n_lat_tiles):
    r_refs, (g_ref, o_ref) = refs[:-2], refs[-2:]
    acc = jnp.dot(a_ref[...], w_ref[...].astype(BF16), preferred_element_type=F32)
    o_ref[...] = _stream_tile(r_refs, n_lat_tiles) + g_ref[...] * acc


def matmul_resid(a, w, res_parts, mod3, geom, k_gate, tm, tn):
    m, k = a.shape
    n = w.shape[1]
    nj = n // tn
    return pl.pallas_call(
        functools.partial(_mm_resid_kernel, n_lat_tiles=geom.r_lat // tm),
        grid=(m // tm, nj),
        in_specs=[pl.BlockSpec((tm, k), lambda i, j: (i, 0)),
                  pl.BlockSpec((k, tn), lambda i, j: (0, j))]
        + _stream_specs(res_parts, tm, tn, lambda j: j)
        + [pl.BlockSpec((None, 1, tn), lambda i, j: (geom.mod_row(i, tm), 0, k_gate * nj + j))],
        out_specs=pl.BlockSpec((tm, tn), lambda i, j: (i, j)),
        out_shape=jax.ShapeDtypeStruct((m, n), F32),
        compiler_params=_cparams(("parallel", "arbitrary")),
        name="matmul_resid",
    )(a, w, *res_parts, mod3)


def _conv_kernel(u_ref, w_ref, b_ref, o_ref, *, n_lat_blocks, l_ctx):
    u = u_ref[...]
    l, ct = u.shape
    pad = SSD_CONV_W // 2
    t = lax.broadcasted_iota(jnp.int32, (l, V7X_LANES), 0)
    is_ctx = pl.program_id(0) >= n_lat_blocks
    pos = jnp.where(is_ctx, t & (l_ctx - 1), t)
    seq_len = jnp.where(is_ctx, l_ctx, l)
    acc = u * w_ref[pad:pad + 1, :] + b_ref[...]
    for k in range(SSD_CONV_W):
        off = k - pad
        if off == 0:
            continue
        shifted = pltpu.roll(u, (-off) % l, axis=0)
        inside =jnp.where((pos + off >= 0) & (pos + off < seq_len), 1.0, 0.0)
        acc = acc + (shifted * jnp.concatenate([inside] * (ct // V7X_LANES), axis=1)) * w_ref[k:k + 1, :]
    o_ref[...] = _silu(acc)


def ssd_conv(p, conv_w, conv_b, geom):
    rows = p.shape[0]
    ct = 512
    col_off = SSD_D_INNER // ct
    assert geom.l_ctx & (geom.l_ctx - 1) == 0
    return pl.pallas_call(
        functools.partial(_conv_kernel, n_lat_blocks=geom.batch, l_ctx=geom.l_ctx),
        grid=(rows // geom.l_lat, SSD_XBC // ct),
        in_specs=[pl.BlockSpec((geom.l_lat, ct), lambda s, c: (s, c + col_off)),
                  pl.BlockSpec((SSD_CONV_W, ct), lambda s, c: (0, c)),
                  pl.BlockSpec((1, ct), lambda s, c: (0, c))],
        out_specs=pl.BlockSpec((geom.l_lat, ct), lambda s, c: (s, c)),
        out_shape=jax.ShapeDtypeStruct((rows, SSD_XBC), F32),
        compiler_params=_cparams(("parallel", "parallel")),
        name="ssd_conv",
    )(p, conv_w, conv_b.reshape(1, SSD_XBC))


def _dt_kernel(p_ref, b_ref, al_ref, cs_ref, cd_ref):
    q = SSD_CHUNK
    dt = jax.nn.softplus(p_ref[...] + b_ref[...])
    log2_dt = jnp.log2(dt)
    dta = dt * (-jnp.exp(al_ref[...]) * LOG2_E)
    row = lax.broadcasted_iota(jnp.int32, (q, q), 0)
    col = lax.broadcasted_iota(jnp.int32, (q, q), 1)
    prefix = (col <= row).astype(F32)
    suffix = (col >= row).astype(F32)
    fwd_cols = lax.broadcasted_iota(jnp.int32, (q, dt.shape[1]), 1) < SSD_HEADS
    for c in range(dt.shape[0] // q):
        blk = dta[c * q:(c + 1) * q, :]
        cs = jnp.where(fwd_cols,
                       jnp.dot(prefix, blk, precision=HIGHEST, preferred_element_type=F32),
                       jnp.dot(suffix, blk, precision=HIGHEST, preferred_element_type=F32))
        cs_ref[c * q:(c + 1) * q, :] = cs
        cd_ref[c * q:(c + 1) * q, :] = cs - log2_dt[c * q:(c + 1) * q, :]


def ssd_dt(p, dt_bias, a_log, tl):
    rows, w = p.shape
    assert tl % SSD_CHUNK == 0
    return pl.pallas_call(
        _dt_kernel,
        grid=(rows // tl,),
        in_specs=[pl.BlockSpec((tl, w), lambda i: (i, 0)),
                  pl.BlockSpec((1, w), lambda i: (0, 0)),
                  pl.BlockSpec((1, w), lambda i: (0, 0))],
        out_specs=[pl.BlockSpec((tl, w), lambda i: (i, 0)), pl.BlockSpec((tl, w), lambda i: (i, 0))],
        out_shape=[jax.ShapeDtypeStruct((rows, w), F32), jax.ShapeDtypeStruct((rows, w), F32)],
        compiler_params=_cparams(("parallel",)),
        name="ssd_dt",
    )(p, dt_bias.reshape(1, w), a_log.reshape(1, w))


def _ssd_scan_kernel(x_ref, b_ref, c_ref, csc_ref, csr_ref, cdr_ref, dsk_ref, s0_ref, *rest, nc):
    y_ref, sfin_ref, s_scr, yb_scr = rest[-4], rest[-3], rest[-2], rest[-1]
    q = SSD_CHUNK
    e_n = SSD_HEADS_PER_GROUP
    row = lax.broadcasted_iota(jnp.int32, (q, q), 0)
    col = lax.broadcasted_iota(jnp.int32, (q, q), 1)
    left = col < SSD_HEAD_DIM

    def chunk_step(d, c):
        if d == 0:
            incl, edge = col <= row, q - 1
        else:
            incl, edge = col >= row, 0
        t0 = pl.multiple_of(c * q, q)
        bm = b_ref[pl.ds(t0, q), :]
        cm = c_ref[pl.ds(t0, q), :]
        cs = csc_ref[0, pl.ds(t0, q), :][:, d * e_n:(d + 1) * e_n]
        tot_t = csr_ref[0, c][d * e_n:(d + 1) * e_n, :][:, edge:edge + 1]
        cd_t = cdr_ref[0, c][d * e_n:(d + 1) * e_n, :]
        w_t = jnp.exp2(tot_t - cd_t)
        et_b = jnp.broadcast_to(jnp.exp2(tot_t), (e_n, q))
        cmb = cm.astype(BF16)
        cb = lax.dot_general(cmb, bm.astype(BF16), (((1,), (1,)), ((), ())),
                             preferred_element_type=F32)
        bm_t = bm.T
        y_inter = jnp.dot(cmb, s_scr[d].astype(BF16), preferred_element_type=F32)
        for pr in range(e_n // 2):
            sl = slice(pr * 2 * SSD_HEAD_DIM, (pr + 1) * 2 * SSD_HEAD_DIM)
            mix, bw, ecs = [], [], []
            for e in (2 * pr, 2 * pr + 1):
                cs_b = jnp.broadcast_to(cs[:, e:e + 1], (q, q))
                seg = jnp.exp2(jnp.where(incl, cs_b - cd_t[e:e + 1, :], -jnp.inf))
                mix.append((cb * seg).astype(BF16))
                bw.append((bm_t * w_t[e:e + 1, :]).astype(BF16))
                ecs.append(jnp.exp2(cs_b))
            xs = x_ref[pl.ds(t0, q), sl]
            ss = s_scr[d, :, sl]
            lhs1 = jnp.concatenate([jnp.concatenate(mix, axis=1), jnp.concatenate(bw, axis=1)], axis=0)
            rhs1 = jnp.concatenate([jnp.where(left, xs, 0.0), jnp.where(left, 0.0, xs)], axis=0).astype(BF16)
            r1 = jnp.dot(lhs1, rhs1, preferred_element_type=F32)
            y = r1[:q] + y_inter[:, sl] * jnp.where(left, ecs[0], ecs[1])
            dec = jnp.where(left[:1], et_b[2 * pr:2 * pr + 1, :], et_b[2 * pr + 1:2 * pr + 2, :])
            s_scr[d, :, sl] = ss * dec + r1[q:]
            if d == 0:
                y_ref[pl.ds(t0, q), sl] = y + dsk_ref[:, sl] * xs
            else:
                yb_scr[pl.ds(t0, q), sl] = y

    s_scr[...] = s0_ref[0]

    def body(ci, carry):
        chunk_step(0, ci)
        chunk_step(1, nc - 1 - ci)
        return carry

    lax.fori_loop(0, nc, body, 0, unroll=2 if nc % 2 == 0 else 1)
    sfin_ref[0] = s_scr[...]
    y_ref[...] = y_ref[...] + yb_scr[...]


def ssd_scan(xbc, cs_col, cs_row, cd_row, d_skip, s0, seq_len, n_seq, row_block_off):
    g_n, e_n, gw = SSD_GROUPS, SSD_HEADS_PER_GROUP, SSD_GROUP_WIDTH
    nc = seq_len // SSD_CHUNK
    b_off = SSD_D_INNER // SSD_D_STATE
    c_off = (SSD_D_INNER + SSD_BC) // SSD_D_STATE
    dsk = jnp.repeat(d_skip, SSD_HEAD_DIM).reshape(1, SSD_D_INNER)
    row_spec = pl.BlockSpec((1, nc, 2 * e_n, SSD_CHUNK), lambda s, g: (g, s + row_block_off, 0, 0))
    in_specs = [pl.BlockSpec((seq_len, gw), lambda s, g: (s + row_block_off, g)),
                pl.BlockSpec((seq_len, SSD_D_STATE), lambda s, g: (s + row_block_off, b_off + g)),
                pl.BlockSpec((seq_len, SSD_D_STATE), lambda s, g: (s + row_block_off, c_off + g)),
                pl.BlockSpec((1, seq_len, 2 * e_n), lambda s, g: (g, s + row_block_off, 0)),
                row_spec,
                row_spec,
                pl.BlockSpec((1, gw), lambda s, g: (0, g)),
                pl.BlockSpec((1, 2, SSD_D_STATE, gw), lambda s, g: (s * g_n + g, 0, 0, 0))]
    return pl.pallas_call(
        functools.partial(_ssd_scan_kernel, nc=nc),
        grid=(n_seq, g_n),
        in_specs=in_specs,
        out_specs=[pl.BlockSpec((seq_len, gw), lambda s, g: (s, g)),
                   pl.BlockSpec((1, 2, SSD_D_STATE, gw), lambda s, g: (s * g_n + g, 0, 0, 0))],
        out_shape=[jax.ShapeDtypeStruct((n_seq * seq_len, SSD_D_INNER), F32),
                   jax.ShapeDtypeStruct((n_seq * g_n, 2, SSD_D_STATE, gw), F32)],
        scratch_shapes=[pltpu.VMEM((2, SSD_D_STATE, gw), F32), pltpu.VMEM((seq_len, gw), F32)],
        compiler_params=_cparams(("parallel", "parallel")),
        name="ssd_scan",
    )(xbc, xbc, xbc, cs_col, cs_row, cd_row, dsk, s0)


def _ssd_out_kernel(yl_ref, yc_ref, z_ref, g_ref, o_ref, *, n_lat_tiles):
    is_ctx = pl.program_id(0) >= n_lat_tiles
    y = jnp.where(is_ctx, yc_ref[...], yl_ref[...])
    v = y * _silu(z_ref[...])
    n = v * lax.rsqrt(jnp.mean(v * v, axis=-1, keepdims=True) + EPS)
    o_ref[...] = (n * g_ref[...]).astype(o_ref.dtype)


def ssd_out(y_lat, y_ctx, p, norm_g, tl):
    n_lat, n_ctx = y_lat.shape[0] // tl, y_ctx.shape[0] // tl
    rows = y_lat.shape[0] + y_ctx.shape[0]
    return pl.pallas_call(
        functools.partial(_ssd_out_kernel, n_lat_tiles=n_lat),
        grid=(n_lat + n_ctx,),
        in_specs=[pl.BlockSpec((tl, SSD_D_INNER), lambda i: (jnp.minimum(i, n_lat - 1), 0)),
                  pl.BlockSpec((tl, SSD_D_INNER), lambda i: (jnp.maximum(i - n_lat, 0), 0)),
                  pl.BlockSpec((tl, SSD_D_INNER), lambda i: (i, 0)),
                  pl.BlockSpec((1, SSD_D_INNER), lambda i: (0, 0))],
        out_specs=pl.BlockSpec((tl, SSD_D_INNER), lambda i: (i, 0)),
        out_shape=jax.ShapeDtypeStruct((rows, SSD_D_INNER), BF16),
        compiler_params=_cparams(("parallel",)),
        name="ssd_out",
    )(y_lat, y_ctx, p, norm_g.reshape(1, SSD_D_INNER))


def _ret_state_kernel(k_ref, v_ref, ld_ref, s_ref):
    l = k_ref.shape[0]
    k = k_ref[...] * (RET_QK_DIM ** -0.5)
    vb = v_ref[...].astype(BF16)
    pos = lax.broadcasted_iota(jnp.int32, (l, V7X_LANES), 0).astype(F32)
    for d in range(2):
        ld = -jnp.exp(ld_ref[0, d:d + 1, :])
        steps = (l - 1.0 - pos) if d == 0 else pos
        w = jnp.exp(steps * ld)
        kd = (k * jnp.concatenate([w] * (RET_QK_DIM // V7X_LANES), axis=1)).astype(BF16)
        s_ref[0, d] = lax.dot_general(kd, vb, (((0,), (0,)), ((), ())), preferred_element_type=F32)


def ret_ctx_state(p, decay_b, geom):
    l = geom.l_ctx
    row_off = 0
    k_off = 0
    v_off = D_MODEL // RET_V_DIM
    return pl.pallas_call(
        _ret_state_kernel,
        grid=(geom.batch, RET_HEADS),
        in_specs=[pl.BlockSpec((l, RET_QK_DIM), lambda b, h: (b + row_off, k_off + h)),
                  pl.BlockSpec((l, RET_V_DIM), lambda b, h: (b + row_off, v_off + h)),
                  pl.BlockSpec((1, 2, V7X_LANES), lambda b, h: (h, 0, 0))],
        out_specs=pl.BlockSpec((1, 2, RET_QK_DIM, RET_V_DIM), lambda b, h: (b * RET_HEADS + h, 0, 0, 0)),
        out_shape=jax.ShapeDtypeStruct((geom.batch * RET_HEADS, 2, RET_QK_DIM, RET_V_DIM), F32),
        compiler_params=_cparams(("parallel", "parallel")),
        name="ret_ctx_state",
    )(p, p, decay_b)


def _rope(u, cos, sin):
    parts = []
    for j in range(u.shape[1] // V7X_LANES):
        s = u[:, j * V7X_LANES:(j + 1) * V7X_LANES]
        parts.append(pltpu.roll(s, V7X_LANES // 2, axis=1))
    return u * cos + jnp.concatenate(parts, axis=1) * sin


def _ret_scan_kernel(q_ref, k_ref, v_ref, g_ref, cos_ref, sin_ref, ld_ref, gn_ref, s0_ref, o_ref,
                     s_scr, qb_scr, sb_scr, acc_scr, *, nc):
    q = RET_SCAN_CHUNK
    nslab_k = RET_QK_DIM // V7X_LANES
    nslab_v = RET_V_DIM // V7X_LANES
    rel = (lax.broadcasted_iota(jnp.int32, (q, q), 0) - lax.broadcasted_iota(jnp.int32, (q, q), 1)).astype(F32)
    l_i = lax.broadcasted_iota(jnp.int32, (q, V7X_LANES), 0).astype(F32)
    ld_f = -jnp.exp(ld_ref[0, 0:1, :])
    ld_b = -jnp.exp(ld_ref[0, 1:2, :])
    decay_in = (jnp.exp(jnp.where(rel >= 0, rel * ld_f[:, :1], -jnp.inf))
                + jnp.exp(jnp.where(rel <= 0, -rel * ld_b[:, :1], -jnp.inf)))
    from_state = (jnp.exp((l_i + 1.0) * ld_f), jnp.exp((q - l_i) * ld_b))
    to_end = (jnp.exp((q - 1.0 - l_i) * ld_f), jnp.exp(l_i * ld_b))
    chunk_decay = (jnp.exp(q * ld_f), jnp.exp(q * ld_b))

    def roped_key(t0):
        return _rope(k_ref[pl.ds(t0, q), :], cos_ref[pl.ds(t0, q), :], sin_ref[pl.ds(t0, q), :]) * (RET_QK_DIM ** -0.5)

    def update_state(d, k, vb):
        kd = (k * jnp.concatenate([to_end[d]] * nslab_k, axis=1)).astype(BF16)
        s_scr[d] = (s_scr[d] * jnp.concatenate([chunk_decay[d]] * nslab_v, axis=1)
                    + lax.dot_general(kd, vb, (((0,), (0,)), ((), ())), preferred_element_type=F32))

    s_scr[...] = s0_ref[0]

    def scan(ci, carry):
        t0 = pl.multiple_of(ci * q, q)
        qb = _rope(q_ref[pl.ds(t0, q), :], cos_ref[pl.ds(t0, q), :], sin_ref[pl.ds(t0, q), :]).astype(BF16)
        k = roped_key(t0)
        vb = v_ref[pl.ds(t0, q), :].astype(BF16)
        qb_scr[pl.ds(t0, q), :] = qb
        scores = lax.dot_general(qb, k.astype(BF16), (((1,), (1,)), ((), ())),
                                 preferred_element_type=F32) * decay_in
        inter_f = (jnp.dot(qb, s_scr[0].astype(BF16), preferred_element_type=F32)
                   * jnp.concatenate([from_state[0]] * nslab_v, axis=1))
        acc_scr[pl.ds(t0, q), :] = jnp.dot(scores.astype(BF16), vb, preferred_element_type=F32) + inter_f
        update_state(0, k, vb)
        cb = nc - 1 - ci
        tb = pl.multiple_of(cb * q, q)
        sb_scr[cb] = s_scr[1].astype(BF16)
        update_state(1, roped_key(tb), v_ref[pl.ds(tb, q), :].astype(BF16))
        return carry

    lax.fori_loop(0, nc, scan, 0)

    def finish(c, carry):
        t0 = pl.multiple_of(c * q, q)
        o = acc_scr[pl.ds(t0, q), :] + (jnp.dot(qb_scr[pl.ds(t0, q), :], sb_scr[c], preferred_element_type=F32)
                                        * jnp.concatenate([from_state[1]] * nslab_v, axis=1))
        mu = jnp.mean(o, axis=-1, keepdims=True)
        var = jnp.mean(jnp.square(o - mu), axis=-1, keepdims=True)
        on = ((o - mu) * lax.rsqrt(var + EPS)) * gn_ref[...]
        o_ref[pl.ds(t0, q), :] = (on * _silu(g_ref[pl.ds(t0, q), :])).astype(o_ref.dtype)
        return carry

    lax.fori_loop(0, nc, finish, 0, unroll=2 if nc % 2 == 0 else 1)


def ret_scan(p, cos_t, sin_t, decay_b, gn_g, s0, geom):
    l = geom.l_lat
    assert l % RET_SCAN_CHUNK == 0
    nc = l // RET_SCAN_CHUNK
    k_off = D_MODEL // RET_QK_DIM
    v_off = 2 * D_MODEL // RET_V_DIM
    g_off = (2 * D_MODEL + RET_D_V) // RET_V_DIM
    return pl.pallas_call(
        functools.partial(_ret_scan_kernel, nc=nc),
        grid=(geom.batch, RET_HEADS),
        in_specs=[pl.BlockSpec((l, RET_QK_DIM), lambda b, h: (b, h)),
                  pl.BlockSpec((l, RET_QK_DIM), lambda b, h: (b, k_off + h)),
                  pl.BlockSpec((l, RET_V_DIM), lambda b, h: (b, v_off + h)),
                  pl.BlockSpec((l, RET_V_DIM), lambda b, h: (b, g_off + h)),
                  pl.BlockSpec((l, RET_QK_DIM), lambda b, h: (0, 0)),
                  pl.BlockSpec((l, RET_QK_DIM), lambda b, h: (0, 0)),
                  pl.BlockSpec((1, 2, V7X_LANES), lambda b, h: (h, 0, 0)),
                  pl.BlockSpec((1, RET_V_DIM), lambda b, h: (0, h)),
                  pl.BlockSpec((1, 2, RET_QK_DIM, RET_V_DIM), lambda b, h: (b * RET_HEADS + h, 0, 0, 0))],
        out_specs=pl.BlockSpec((l, RET_V_DIM), lambda b, h: (b, h)),
        out_shape=jax.ShapeDtypeStruct((geom.r_lat, RET_D_V), BF16),
        scratch_shapes=[pltpu.VMEM((2, RET_QK_DIM, RET_V_DIM), F32),
                        pltpu.VMEM((l, RET_QK_DIM), BF16),
                        pltpu.VMEM((nc, RET_QK_DIM, RET_V_DIM), BF16),
                        pltpu.VMEM((l, RET_V_DIM), F32)],
        compiler_params=_cparams(("parallel", "parallel")),
        name="ret_scan",
    )(p, p, p, p, cos_t, sin_t, decay_b, gn_g.reshape(1, RET_D_V), s0)


def rope_tables(l_lat):
    half = RET_QK_DIM // 4
    pos = jnp.arange(l_lat)
    freqs = ROPE_BASE ** (-jnp.arange(half, dtype=F32) / half)
    cs, sn = [], []
    for ids in (pos // GRID_W, pos % GRID_W):
        ang = ids.astype(F32)[:, None] * freqs[None, :]
        cs += [jnp.cos(ang), jnp.cos(ang)]
        sn += [-jnp.sin(ang), jnp.sin(ang)]
    return jnp.concatenate(cs, axis=1), jnp.concatenate(sn, axis=1)


def _router_kernel(x_ref, g_ref, sc_ref, sh_ref, wr_ref, f_ref, ar_ref):
    x = x_ref[...]
    y = x * lax.rsqrt(jnp.mean(x * x, axis=-1, keepdims=True) + EPS)
    f = (y * g_ref[...]) * (1.0 + sc_ref[...]) + sh_ref[...]
    f_ref[...] = f
    logits = jnp.dot(f, wr_ref[...], precision=HIGHEST, preferred_element_type=F32)
    lane = lax.broadcasted_iota(jnp.int32, logits.shape, 1)
    logits = jnp.where(lane < N_EXPERTS, logits, -jnp.inf)
    un = jnp.exp(logits - jnp.max(logits, axis=-1, keepdims=True))
    aff = un / jnp.sum(un, axis=-1, keepdims=True)
    ar_ref[...] = aff.T[:N_EXPERTS, :]


def moe_router(x2d, g, mod3, w_router, geom, k_shift, k_scale, rows):
    d = x2d.shape[1]
    tl = geom.row_tile(ROUTER_ROWS)
    wr = jnp.pad(w_router, ((0, 0), (0, ROUTER_LANES - N_EXPERTS)))
    return pl.pallas_call(
        _router_kernel,
        grid=(rows // tl,),
        in_specs=[pl.BlockSpec((tl, d), lambda i: (i, 0)),
                  pl.BlockSpec((1, d), lambda i: (0, 0)),
                  pl.BlockSpec((None, 1, d), lambda i: (geom.mod_row(i, tl), 0, k_scale)),
                  pl.BlockSpec((None, 1, d), lambda i: (geom.mod_row(i, tl), 0, k_shift)),
                  pl.BlockSpec((d, ROUTER_LANES), lambda i: (0, 0))],
        out_specs=[pl.BlockSpec((tl, d), lambda i: (i, 0)),
                   pl.BlockSpec((N_EXPERTS, tl), lambda i: (0, i))],
        out_shape=[jax.ShapeDtypeStruct((rows, d), F32),
                   jax.ShapeDtypeStruct((N_EXPERTS, rows), F32)],
        compiler_params=_cparams(("parallel",)),
        name="moe_router",
    )(x2d, g.reshape(1, d), mod3, mod3, wr)


def _select_kernel(ar_ref, o_ref, slot_scr, *, n, cap):
    e_n, lanes, sub = N_EXPERTS, V7X_LANES, V7X_SUBLANES
    a = ar_ref[...]
    bits = pltpu.bitcast(a, jnp.int32)

    def bisect(i, thr):
        cand = thr | lax.shift_left(jnp.int32(1), 30 - i)
        cnt = jnp.sum(jnp.where(bits >= cand, 1.0, 0.0), axis=1, keepdims=True)
        return jnp.where(cnt >= cap, cand, thr)

    thr = lax.fori_loop(0, 31, bisect, jnp.zeros((e_n, 1), jnp.int32))
    gt = jnp.where(bits > thr, 1.0, 0.0)
    eq = jnp.where(bits == thr, 1.0, 0.0)
    need = cap - jnp.sum(gt, axis=1, keepdims=True)
    upper = (lax.broadcasted_iota(jnp.int32, (lanes, lanes), 0)
             <= lax.broadcasted_iota(jnp.int32, (lanes, lanes), 1)).astype(BF16)
    ties_before = jnp.zeros((e_n, 1), F32)
    taken_before = jnp.zeros((e_n, 1), F32)
    for j in range(n // lanes):
        sl = slice(j * lanes, (j + 1) * lanes)
        eq_j = eq[:, sl]
        tie_incl = jnp.dot(eq_j.astype(BF16), upper, preferred_element_type=F32)
        tie_pos = ties_before + tie_incl - eq_j
        ties_before = ties_before + tie_incl[:, lanes - 1:lanes]
        sel_j = gt[:, sl] + eq_j * jnp.where(tie_pos < need, 1.0, 0.0)
        sel_incl = jnp.dot(sel_j.astype(BF16), upper, preferred_element_type=F32)
        slot_scr[:, sl] = jnp.where(sel_j > 0.0, taken_before + sel_incl - sel_j, -1.0)
        taken_before = taken_before + sel_incl[:, lanes - 1:lanes]

    r_col = lax.broadcasted_iota(jnp.int32, (cap, n), 0).astype(F32)
    t_row = lax.broadcasted_iota(jnp.int32, (cap, n), 1).astype(F32)
    lane_o = lax.broadcasted_iota(jnp.int32, (cap, lanes), 1)
    cap_pad = o_ref.shape[-1]

    def invert(e, carry):
        onehot = slot_scr[pl.ds(e, 1), :] == r_col
        idx = jnp.sum(jnp.where(onehot, t_row, 0.0), axis=1, keepdims=True)
        gate = jnp.sum(jnp.where(onehot, ar_ref[pl.ds(e, 1), :], 0.0), axis=1, keepdims=True)
        packed = jnp.where(lane_o == 0, idx, jnp.where(lane_o == 1, gate, 0.0))
        if cap_pad > cap:
            packed = jnp.concatenate([packed, jnp.zeros((cap_pad - cap, lanes), F32)], axis=0)
        o_ref[e] = packed.T[:sub, :]
        return carry

    lax.fori_loop(0, e_n, invert, 0)


def moe_select(aff_row, n, n_sets, row_block_off):
    cap = EC_CAPACITY * n // N_EXPERTS
    cap_pad = max(cap, V7X_LANES)
    packed = pl.pallas_call(
        functools.partial(_select_kernel, n=n, cap=cap),
        grid=(n_sets,),
        in_specs=[pl.BlockSpec((N_EXPERTS, n), lambda b: (0, b + row_block_off))],
        out_specs=pl.BlockSpec((N_EXPERTS, V7X_SUBLANES, cap_pad), lambda b: (b, 0, 0)),
        out_shape=jax.ShapeDtypeStruct((n_sets * N_EXPERTS, V7X_SUBLANES, cap_pad), F32),
        scratch_shapes=[pltpu.VMEM((N_EXPERTS, n), F32)],
        compiler_params=_cparams(("parallel",)),
        name="moe_select",
    )(aff_row)
    return packed[:, :, :cap]


def _gather_kernel(idx_ref, f_ref, o_ref, buf, *, cap):
    def body(r, carry):
        buf[pl.ds(r, 1), :] = f_ref[pl.ds(idx_ref[0, 0, r], 1), :]
        return carry

    lax.fori_loop(0, cap, body, 0, unroll=8)
    o_ref[...] = buf[...].astype(o_ref.dtype)


def moe_gather(idx, f2d, n, n_blocks):
    d = f2d.shape[1]
    cap = idx.shape[-1]
    dt = d
    return pl.pallas_call(
        functools.partial(_gather_kernel, cap=cap),
        grid=(n_blocks, d // dt, N_EXPERTS),
        in_specs=[pl.BlockSpec((1, 1, cap), lambda b, c, e: (b * N_EXPERTS + e, 0, 0), memory_space=pltpu.SMEM),
                  pl.BlockSpec((n, dt), lambda b, c, e: (b, c))],
        out_specs=pl.BlockSpec((None, cap, dt), lambda b, c, e: (e, b, c)),
        out_shape=jax.ShapeDtypeStruct((N_EXPERTS, n_blocks * cap, d), BF16),
        scratch_shapes=[pltpu.VMEM((cap, dt), F32)],
        compiler_params=_cparams(("parallel", "parallel", "arbitrary")),
        name="moe_gather",
    )(idx, f2d)


def _glu_kernel(x_ref, wg_ref, wu_ref, o_ref):
    x = x_ref[...]
    hg = jnp.dot(x, wg_ref[...].astype(BF16), preferred_element_type=F32)
    hu = jnp.dot(x, wu_ref[...].astype(BF16), preferred_element_type=F32)
    o_ref[...] = (_silu(hg) * hu).astype(o_ref.dtype)


def moe_glu(xe, w_gate, w_up, layer):
    e_n, m, d = xe.shape
    f = w_gate.shape[-1]
    tn = 512
    return pl.pallas_call(
        _glu_kernel,
        grid=(e_n, f // tn),
        in_specs=[pl.BlockSpec((None, m, d), lambda e, j: (e, 0, 0)),
                  pl.BlockSpec((None, None, d, tn), lambda e, j: (layer, e, 0, j)),
                  pl.BlockSpec((None, None, d, tn), lambda e, j: (layer, e, 0, j))],
        out_specs=pl.BlockSpec((None, m, tn), lambda e, j: (e, 0, j)),
        out_shape=jax.ShapeDtypeStruct((e_n, m, f), BF16),
        compiler_params=_cparams(("parallel", "arbitrary")),
        name="moe_glu",
    )(xe, w_gate, w_up)


def _down_kernel(h_ref, wd_ref, gate_ref, g_ref, o_ref, *, row_groups):
    acc = jnp.dot(h_ref[...], wd_ref[...].astype(BF16), preferred_element_type=F32)
    reps = acc.shape[1] // V7X_LANES
    for start, size, mrow in row_groups:
        gate = jnp.concatenate([gate_ref[start:start + size, :]] * reps, axis=1)
        o_ref[start:start + size, :] = (acc[start:start + size, :] * gate) * g_ref[mrow]


def moe_down(hid, w_down, gate_col, mod3, k_gate, row_groups, layer):
    e_n, m, f = hid.shape
    d = w_down.shape[-1]
    tn = 512
    nj = d // tn
    return pl.pallas_call(
        functools.partial(_down_kernel, row_groups=row_groups),
        grid=(e_n, nj),
        in_specs=[pl.BlockSpec((None, m, f), lambda e, j: (e, 0, 0)),
                  pl.BlockSpec((None, None, f, tn), lambda e, j: (layer, e, 0, j)),
                  pl.BlockSpec((None, m, V7X_LANES), lambda e, j: (e, 0, 0)),
                  pl.BlockSpec((MOD_ROWS, 1, tn), lambda e, j: (0, 0, k_gate * nj + j))],
        out_specs=pl.BlockSpec((None, m, tn), lambda e, j: (e, 0, j)),
        out_shape=jax.ShapeDtypeStruct((e_n, m, d), F32),
        compiler_params=_cparams(("parallel", "arbitrary")),
        name="moe_down",
    )(hid, w_down, gate_col, mod3)


COMBINE_GROUP = 8


def _combine_kernel(idx_ref, ye_ref, h_hbm, o_ref, sem, *, cap, n):
    b, e = pl.program_id(0), pl.program_id(1)

    @pl.when(e == 0)
    def _():
        copy = pltpu.make_async_copy(h_hbm.at[pl.ds(pl.multiple_of(b * n, n), n), :], o_ref, sem)
        copy.start()
        copy.wait()

    def body(g, carry):
        base = g * COMBINE_GROUP
        toks = [idx_ref[0, 0, base + u] for u in range(COMBINE_GROUP)]
        rows = [o_ref[pl.ds(t, 1), :] + ye_ref[pl.ds(base + u, 1), :] for u, t in enumerate(toks)]
        for t, row in zip(toks, rows):
            o_ref[pl.ds(t, 1), :] = row
        return carry

    lax.fori_loop(0, cap // COMBINE_GROUP, body, 0)


def moe_combine(idx, ye, x2d, n, n_blocks):
    d = x2d.shape[1]
    cap = idx.shape[-1]
    return pl.pallas_call(
        functools.partial(_combine_kernel, cap=cap, n=n),
        grid=(n_blocks, N_EXPERTS),
        in_specs=[pl.BlockSpec((1, 1, cap), lambda b, e: (b * N_EXPERTS + e, 0, 0), memory_space=pltpu.SMEM),
                  pl.BlockSpec((None, cap, d), lambda b, e: (e, b, 0)),
                  pl.BlockSpec(memory_space=pl.ANY)],
        out_specs=pl.BlockSpec((n, d), lambda b, e: (b, 0)),
        out_shape=jax.ShapeDtypeStruct((n_blocks * n, d), F32),
        scratch_shapes=[pltpu.SemaphoreType.DMA(())],
        compiler_params=_cparams(("parallel", "arbitrary")),
        name="moe_combine",
    )(idx, ye, x2d)


def _split_select(packed, n_sets):
    cap = packed.shape[-1]
    idx = packed[:, 0, :].astype(jnp.int32).reshape(n_sets, N_EXPERTS, cap)
    gate = packed[:, 1, :].reshape(n_sets, N_EXPERTS, cap)
    return idx, gate


def ec_moe(x2d, norm_g, mod3, w_router, w_gate, w_up, w_down, layer, geom, with_ctx):
    b_n, e_n = geom.batch, N_EXPERTS
    rows = geom.rows if with_ctx else geom.r_lat
    f2d, aff_row = moe_router(x2d, norm_g, mod3, w_router, geom, 3, 4, rows)
    cap = EC_CAPACITY * geom.l_lat // e_n
    idx, gate = _split_select(moe_select(aff_row, geom.l_lat, b_n, 0), b_n)
    row_groups = [(b * cap, cap, b) for b in range(b_n)]
    if with_ctx:
        cap_c = EC_CAPACITY * geom.l_ctx // e_n
        assert b_n * cap_c == cap and geom.r_ctx == geom.l_lat
        idx_c, gate_c = _split_select(moe_select(aff_row, geom.l_ctx, b_n, geom.r_lat // geom.l_ctx), b_n)
        idx_c = idx_c + (jnp.arange(b_n, dtype=jnp.int32) * geom.l_ctx)[:, None, None]
        idx = jnp.concatenate([idx, idx_c.transpose(1, 0, 2).reshape(1, e_n, cap)], axis=0)
        gate = jnp.concatenate([gate, gate_c.transpose(1, 0, 2).reshape(1, e_n, cap)], axis=0)
        row_groups += [(b_n * cap + b * cap_c, cap_c, b_n) for b in range(b_n)]
    n_blocks = idx.shape[0]
    idx = idx.reshape(n_blocks * e_n, 1, cap)
    gate_col = jnp.broadcast_to(gate.transpose(1, 0, 2).reshape(e_n, n_blocks * cap, 1),
                                (e_n, n_blocks * cap, V7X_LANES))
    xe = moe_gather(idx, f2d, geom.l_lat, n_blocks)
    ye = moe_down(moe_glu(xe, w_gate, w_up, layer), w_down, gate_col, mod3, 5, tuple(row_groups), layer)
    return moe_combine(idx, ye, x2d, geom.l_lat, n_blocks)


def _final_norm_kernel(x_ref, g_ref, o_ref):
    x = x_ref[...]
    o_ref[...] = (x * lax.rsqrt(jnp.mean(x * x, axis=-1, keepdims=True) + EPS)) * g_ref[...]


def final_norm(x2d, g, rows, tl):
    d = x2d.shape[1]
    return pl.pallas_call(
        _final_norm_kernel,
        grid=(rows // tl,),
        in_specs=[pl.BlockSpec((tl, d), lambda i: (i, 0)), pl.BlockSpec((1, d), lambda i: (0, 0))],
        out_specs=pl.BlockSpec((tl, d), lambda i: (i, 0)),
        out_shape=jax.ShapeDtypeStruct((rows, d), F32),
        compiler_params=_cparams(("parallel",)),
        name="final_norm",
    )(x2d, g.reshape(1, d))


def _group_dt(dt, rows):
    g_n, e_n = SSD_GROUPS, SSD_HEADS_PER_GROUP
    dt_col = dt.reshape(rows, 2, g_n, e_n).transpose(2, 0, 1, 3).reshape(g_n, rows, 2 * e_n)
    dt_row = dt_col.reshape(g_n, rows // SSD_CHUNK, SSD_CHUNK, 2 * e_n).transpose(0, 1, 3, 2)
    return dt_col, dt_row


def ssd_layer(stream, mod3, norm_mix_g, w_in, conv_w, conv_b, dt_bias, a_log, d_skip, norm_g, w_out, geom, tm):
    a = norm_mod(stream, norm_mix_g, mod3, geom, 0, 1, geom.rows, BF16)
    n_main = SSD_D_INNER + SSD_XBC
    p = matmul(a, w_in, _pick_tile(geom.rows, IN_PROJ_ROWS), 512, 0, n_main)
    p_dt = matmul(a, w_in, tm, 2 * SSD_HEADS, n_main, 2 * SSD_HEADS)
    ctx_blk = geom.r_lat // geom.l_ctx
    xbc = ssd_conv(p, conv_w, conv_b, geom)
    cs, cd = ssd_dt(p_dt, dt_bias, a_log, tm)
    cs_col, cs_row = _group_dt(cs, geom.rows)
    _, cd_row = _group_dt(cd, geom.rows)
    zero = jnp.zeros((geom.batch * SSD_GROUPS, 2, SSD_D_STATE, SSD_GROUP_WIDTH), F32)
    y_ctx, s_ctx = ssd_scan(xbc, cs_col, cs_row, cd_row, d_skip, zero, geom.l_ctx, geom.batch, ctx_blk)
    y_lat, _ = ssd_scan(xbc, cs_col, cs_row, cd_row, d_skip, s_ctx, geom.l_lat, geom.batch, 0)
    yb = ssd_out(y_lat, y_ctx, p, norm_g, _pick_tile(geom.l_ctx, 256))
    return matmul_resid(yb, w_out, stream, mod3, geom, 2, _pick_tile(geom.r_ctx, OUT_PROJ_ROWS), OUT_PROJ_COLS)


def ret_layer(stream, mod3, norm_mix_g, w_in, decay, gn_g, w_out, geom, tm):
    a = norm_mod(stream, norm_mix_g, mod3, geom, 0, 1, geom.rows, BF16)
    p_lat = matmul(a, w_in, _pick_tile(geom.r_lat, IN_PROJ_ROWS), 512, 0, None, 0, geom.r_lat)
    p_ctx = matmul(a, w_in, _pick_tile(geom.r_ctx, IN_PROJ_ROWS), 512, D_MODEL, D_MODEL + RET_D_V,
                   geom.r_lat, geom.r_ctx)
    decay_b = jnp.broadcast_to(decay.T.reshape(RET_HEADS, 2, 1), (RET_HEADS, 2, V7X_LANES))
    cos_t, sin_t = rope_tables(geom.l_lat)
    s_ctx = ret_ctx_state(p_ctx, decay_b, geom)
    yb = ret_scan(p_lat, cos_t, sin_t, decay_b, gn_g, s_ctx, geom)
    return matmul_resid(yb, w_out, stream, mod3, geom, 2, _pick_tile(geom.r_ctx, OUT_PROJ_ROWS), OUT_PROJ_COLS)


def kernel(x, c, ctx, c_ctx, ada_w, ada_b, norm_mix_g, norm_ffn_g, ssd_w_in, ssd_conv_w, ssd_conv_b,
           ssd_dt_bias, ssd_a_log, ssd_d, ssd_norm_g, ssd_w_out, ret_w_in, ret_decay, ret_gn_g, ret_w_out,
           moe_w_router, moe_w_gate, moe_w_up, moe_w_down, final_norm_g):
    batch, l_lat, d = x.shape
    l_ctx = ctx.shape[1]
    depth = ada_w.shape[0]
    n_mixers = 2
    assert batch + 1 <= MOD_ROWS and d == D_MODEL
    assert depth == n_mixers, "the retention layer must be the last one: its context branch is state-only"
    geom = Geom(batch, l_lat, l_ctx)
    assert geom.r_ctx == l_lat, "all context sequences together must fill exactly one latent-sized row block"
    tm = _pick_tile(geom.r_ctx, 1024)
    stream = (x.reshape(geom.r_lat, d), ctx.reshape(geom.r_ctx, d))
    cc = jnp.zeros((MOD_ROWS, d), F32).at[:batch].set(c).at[batch].set(c_ctx)
    mod = ada_table(cc, ada_w, ada_b)
    for i in range(depth):
        need_ctx = i < depth - 1
        j = i // n_mixers
        mod3 = mod[i].reshape(MOD_ROWS, 1, N_MOD * d)
        if i % n_mixers == 0:
            x2d = ssd_layer(stream, mod3, norm_mix_g[i], ssd_w_in[j], ssd_conv_w[j], ssd_conv_b[j], ssd_dt_bias[j],
                            ssd_a_log[j], ssd_d[j], ssd_norm_g[j], ssd_w_out[j], geom, tm)
        else:
            x2d = ret_layer(stream, mod3, norm_mix_g[i], ret_w_in[j], ret_decay[j], ret_gn_g[j], ret_w_out[j],
                            geom, tm)
        x2d = ec_moe(x2d, norm_ffn_g[i], mod3, moe_w_router[i], moe_w_gate, moe_w_up, moe_w_down, i, geom,
                     need_ctx)
        stream = (x2d,)
    out = final_norm(x2d, final_norm_g, geom.r_lat, geom.row_tile(NORM_ROWS))
    return out.reshape(batch, l_lat, d)
```
